```python
import jax, jax.numpy as jnp
from jax import lax
import numpy as np

D_MODEL = 1024
BATCH = 2
SEQ = 8192
DEPTH = 4

N_MEM = 256
FOX_HEADS = 8
FOX_HEAD_DIM = D_MODEL // 16
FOX_WIDTH = FOX_HEADS * FOX_HEAD_DIM
Q_BLOCK = 128
HGRN_HEADS = 4
HGRN_KEY_DIM = D_MODEL // 16
HGRN_VAL_DIM = D_MODEL // 16
HGRN_KEY_WIDTH = HGRN_HEADS * HGRN_KEY_DIM
HGRN_WIDTH = HGRN_HEADS * HGRN_VAL_DIM
HGRN_CHUNK = 16
CONV_CH = D_MODEL // 4
CONV_GROUPS = 4
CONV_WIDTH = 31
MIX_WIDTH = FOX_WIDTH + HGRN_WIDTH + CONV_CH
IN_SIZES = (FOX_WIDTH, FOX_WIDTH, FOX_WIDTH, FOX_HEADS,
            HGRN_KEY_WIDTH, HGRN_KEY_WIDTH, HGRN_WIDTH, HGRN_WIDTH,
            CONV_CH, CONV_CH)
IN_WIDTH = 3 * FOX_WIDTH + FOX_HEADS + 2 * HGRN_KEY_WIDTH + 2 * HGRN_WIDTH + 2 * CONV_CH
XATT_HEADS = 4
XATT_HEAD_DIM = D_MODEL // XATT_HEADS
N_GROUPS = 4
EXPERTS_PER_GROUP = 4
EXPERT_TOPK = 2
EXPERT_FF = D_MODEL // 4
EPS = 1e-6

kernel_name = "hybrid_fox_hgrn2_conformer_hmoe"


def rmsnorm(x, g):
    xf = x.astype(jnp.float32)
    y = xf * lax.rsqrt(jnp.mean(xf * xf, axis=-1, keepdims=True) + EPS)
    return (y * g.astype(jnp.float32)).astype(x.dtype)


def split_columns(y, sizes):
    out, start = [], 0
    for n in sizes:
        out.append(y[..., start:start + n])
        start += n
    return out


def forgetting_attention(q, k, v, log_f):
    B, H, S, d = q.shape
    c = jnp.cumsum(log_f, axis=-1)
    nb = S // Q_BLOCK
    qb = q.reshape(B, H, nb, Q_BLOCK, d).transpose(2, 0, 1, 3, 4)
    cb = c.reshape(B, H, nb, Q_BLOCK).transpose(2, 0, 1, 3)
    qpos = jnp.arange(S, dtype=jnp.int32).reshape(nb, Q_BLOCK)
    kpos = jnp.arange(S, dtype=jnp.int32)
    scale = d ** -0.5

    def block(args):
        qi, ci, pi = args
        s = jnp.einsum('bhqd,bhkd->bhqk', qi, k).astype(jnp.float32) * scale
        s = s + (ci[..., None] - c[:, :, None, :])
        s = jnp.where(pi[:, None] >= kpos[None, :], s, -jnp.inf)
        p = jax.nn.softmax(s, axis=-1)
        return jnp.einsum('bhqk,bhkd->bhqd', p.astype(v.dtype), v)

    out = lax.map(block, (qb, cb, qpos))
    return out.transpose(1, 2, 0, 3, 4).reshape(B, H, S, d)


def hgrn2_recurrence(q, f_pre, v, lb):
    B, S, H, K = q.shape
    V = v.shape[-1]
    C = HGRN_CHUNK
    N = S // C
    f = lb + (1.0 - lb) * jax.nn.sigmoid(f_pre.astype(jnp.float32))
    g = jnp.log(f)
    k = 1.0 - f

    def chunks(t):
        return t.reshape(B, N, C, H, t.shape[-1]).transpose(0, 3, 1, 2, 4)

    qc = chunks(q.astype(jnp.float32))
    kc = chunks(k)
    vc = chunks(v.astype(jnp.float32))
    b = jnp.cumsum(chunks(g), axis=3)
    b_last = b[:, :, :, -1:, :]
    b_mid = b[:, :, :, C // 2:C // 2 + 1, :]
    attn = jnp.einsum('bhncK,bhnsK->bhncs', qc * jnp.exp(b - b_mid), kc * jnp.exp(b_mid - b))
    causal = jnp.tril(jnp.ones((C, C), dtype=bool))
    attn = jnp.where(causal, attn, 0.0)
    o_intra = jnp.einsum('bhncs,bhnsv->bhncv', attn, vc)
    u = jnp.einsum('bhncK,bhncv->bhnKv', kc * jnp.exp(b_last - b), vc)
    decay = jnp.exp(b_last[:, :, :, 0, :])

    def step(state, inp):
        dec, uu = inp
        return state * dec[..., None] + uu, state

    _, s_prev = lax.scan(step, jnp.zeros((B, H, K, V), jnp.float32),
                         (decay.transpose(2, 0, 1, 3), u.transpose(2, 0, 1, 3, 4)))
    s_prev = s_prev.transpose(1, 2, 0, 3, 4)
    o_inter = jnp.einsum('bhncK,bhnKv->bhncv', qc * jnp.exp(b), s_prev)
    o = (o_intra + o_inter).transpose(0, 2, 3, 1, 4).reshape(B, S, H, V)
    return o.astype(q.dtype)


def causal_depthwise_conv(a, w, bias):
    y = lax.conv_general_dilated(a, w[:, None, :].astype(a.dtype), window_strides=(1,),
                                 padding=[(CONV_WIDTH - 1, 0)],
                                 dimension_numbers=('NWC', 'WIO', 'NWC'),
                                 feature_group_count=a.shape[-1])
    return y + bias.astype(a.dtype)


def channel_group_norm(a, g, bias):
    B, S, C = a.shape
    af = a.astype(jnp.float32).reshape(B, S, CONV_GROUPS, C // CONV_GROUPS)
    mu = jnp.mean(af, axis=-1, keepdims=True)
    var = jnp.mean(jnp.square(af - mu), axis=-1, keepdims=True)
    y = ((af - mu) * lax.rsqrt(var + EPS)).reshape(B, S, C)
    return (y * g + bias).astype(a.dtype)


def hybrid_mixer(h, w_in, fox_f_bias, fox_norm_g, lb, hgrn_norm_g,
                 conv_w, conv_b, conv_norm_g, conv_norm_b, w_out):
    B, S, _ = h.shape
    y = h @ w_in
    fq, fk, fv, ff, hq, hf, hi, hg, cu, cg = split_columns(y, IN_SIZES)
    to_heads = lambda t: t.reshape(B, S, FOX_HEADS, FOX_HEAD_DIM).transpose(0, 2, 1, 3)
    log_f = jax.nn.log_sigmoid((ff + fox_f_bias).astype(jnp.float32)).transpose(0, 2, 1)
    fo = forgetting_attention(to_heads(fq), to_heads(fk), to_heads(fv), log_f).transpose(0, 2, 1, 3)
    fo = rmsnorm(fo, fox_norm_g.reshape(FOX_HEADS, FOX_HEAD_DIM)).reshape(B, S, FOX_WIDTH)
    ho = hgrn2_recurrence(hq.reshape(B, S, HGRN_HEADS, HGRN_KEY_DIM),
                          hf.reshape(B, S, HGRN_HEADS, HGRN_KEY_DIM),
                          hi.reshape(B, S, HGRN_HEADS, HGRN_VAL_DIM),
                          lb.reshape(HGRN_HEADS, HGRN_KEY_DIM))
    ho = rmsnorm(ho, hgrn_norm_g.reshape(HGRN_HEADS, HGRN_VAL_DIM)).reshape(B, S, HGRN_WIDTH) * jax.nn.silu(hg)
    a = cu * jax.nn.sigmoid(cg)
    a = causal_depthwise_conv(a, conv_w, conv_b)
    co = jax.nn.silu(channel_group_norm(a, conv_norm_g, conv_norm_b))
    return jnp.concatenate([fo, ho, co], axis=-1) @ w_out


def cross_attention(h, mem_n, w_xq, w_xkv, w_xo):
    B, S, D = h.shape
    M = mem_n.shape[1]
    q = (h @ w_xq).reshape(B, S, XATT_HEADS, XATT_HEAD_DIM)
    k, v = jnp.split(mem_n @ w_xkv, 2, axis=-1)
    k = k.reshape(B, M, XATT_HEADS, XATT_HEAD_DIM)
    v = v.reshape(B, M, XATT_HEADS, XATT_HEAD_DIM)
    s = jnp.einsum('bshd,bmhd->bhsm', q, k).astype(jnp.float32) * (XATT_HEAD_DIM ** -0.5)
    p = jax.nn.softmax(s, axis=-1)
    o = jnp.einsum('bhsm,bmhd->bshd', p.astype(v.dtype), v).reshape(B, S, D)
    return o @ w_xo


def hierarchical_moe(h, w_group, b_group, w_router, b_router, w_gate, w_up, w_down):
    B, S, D = h.shape
    t = h.reshape(-1, D)
    n = t.shape[0]
    g_logits = (t @ w_group).astype(jnp.float32) + b_group
    p_group = jax.nn.softmax(g_logits, axis=-1)
    g_idx = jnp.argmax(g_logits, axis=-1)
    p_gsel = jnp.take_along_axis(p_group, g_idx[:, None], axis=1)[:, 0]
    e_logits = ((t @ w_router).astype(jnp.float32) + b_router).reshape(n, N_GROUPS, EXPERTS_PER_GROUP)
    e_sel = jnp.take_along_axis(e_logits, g_idx[:, None, None], axis=1)[:, 0]
    p_exp = jax.nn.softmax(e_sel, axis=-1)
    top_v, top_i = lax.top_k(p_exp, EXPERT_TOPK)
    top_v = top_v / jnp.sum(top_v, axis=-1, keepdims=True)
    e_w = jnp.sum(jax.nn.one_hot(top_i, EXPERTS_PER_GROUP, dtype=jnp.float32) * top_v[..., None], axis=1)
    combine = (p_gsel[:, None, None] * jax.nn.one_hot(g_idx, N_GROUPS, dtype=jnp.float32)[:, :, None]
               * e_w[:, None, :]).astype(t.dtype)
    out = jnp.zeros_like(t)
    for grp in range(N_GROUPS):
        hid = jax.nn.silu(jnp.einsum('nd,edf->nef', t, w_gate[grp])) * jnp.einsum('nd,edf->nef', t, w_up[grp])
        out = out + jnp.einsum('nef,efd->nd', hid * combine[:, grp, :, None], w_down[grp])
    return out.reshape(B, S, D)


def setup_inputs(seed: int = 0) -> dict:
    key = jax.random.key(seed)
    ks = jax.random.split(key, 32)
    nrm = lambda k, shape, scale: jax.random.normal(k, shape, jnp.float32) * scale
    gain = lambda k, shape: 1.0 + 0.02 * jax.random.normal(k, shape, jnp.float32)
    G, E, F = N_GROUPS, EXPERTS_PER_GROUP, EXPERT_FF
    return {
        "x": nrm(ks[0], (BATCH, SEQ, D_MODEL), 1.0),
        "mem": nrm(ks[1], (BATCH, N_MEM, D_MODEL), 1.0),
        "mix_norm_g": gain(ks[2], (DEPTH, D_MODEL)),
        "w_in": nrm(ks[3], (DEPTH, D_MODEL, IN_WIDTH), D_MODEL ** -0.5),
        "fox_f_bias": 2.0 + 0.5 * jax.random.normal(ks[4], (DEPTH, FOX_HEADS), jnp.float32),
        "fox_norm_g": gain(ks[5], (DEPTH, FOX_WIDTH)),
        "hgrn_lb_logits": nrm(ks[6], (DEPTH, HGRN_KEY_WIDTH), 0.5),
        "hgrn_norm_g": gain(ks[7], (DEPTH, HGRN_WIDTH)),
        "conv_w": nrm(ks[8], (DEPTH, CONV_WIDTH, CONV_CH), CONV_WIDTH ** -0.5),
        "conv_b": nrm(ks[9], (DEPTH, CONV_CH), 0.02),
        "conv_norm_g": gain(ks[10], (DEPTH, CONV_CH)),
        "conv_norm_b": nrm(ks[11], (DEPTH, CONV_CH), 0.02),
        "w_out": nrm(ks[12], (DEPTH, MIX_WIDTH, D_MODEL), MIX_WIDTH ** -0.5),
        "xatt_norm_g": gain(ks[13], (DEPTH, D_MODEL)),
        "mem_norm_g": gain(ks[14], (D_MODEL,)),
        "w_xq": nrm(ks[15], (DEPTH, D_MODEL, D_MODEL), D_MODEL ** -0.5),
        "w_xkv": nrm(ks[16], (DEPTH, D_MODEL, 2 * D_MODEL), D_MODEL ** -0.5),
        "w_xo": nrm(ks[17], (DEPTH, D_MODEL, D_MODEL), D_MODEL ** -0.5),
        "ffn_norm_g": gain(ks[18], (DEPTH, D_MODEL)),
        "w_group": nrm(ks[19], (DEPTH, D_MODEL, G), D_MODEL ** -0.5),
        "b_group": nrm(ks[20], (DEPTH, G), 0.01),
        "w_router": nrm(ks[21], (DEPTH, D_MODEL, G * E), D_MODEL ** -0.5),
        "b_router": nrm(ks[22], (DEPTH, G * E), 0.01),
        "w_gate": nrm(ks[23], (DEPTH, G, E, D_MODEL, F), D_MODEL ** -0.5),
        "w_up": nrm(ks[24], (DEPTH, G, E, D_MODEL, F), D_MODEL ** -0.5),
        "w_down": nrm(ks[25], (DEPTH, G, E, F, D_MODEL), F ** -0.5),
        "final_norm_g": gain(ks[26], (D_MODEL,)),
    }


def reference(x, mem, mix_norm_g, w_in, fox_f_bias, fox_norm_g, hgrn_lb_logits, hgrn_norm_g,
              conv_w, conv_b, conv_norm_g, conv_norm_b, w_out, xatt_norm_g, mem_norm_g,
              w_xq, w_xkv, w_xo, ffn_norm_g, w_group, b_group, w_router, b_router,
              w_gate, w_up, w_down, final_norm_g):
    mem_n = rmsnorm(mem, mem_norm_g)
    lb_all = jnp.cumsum(jax.nn.softmax(hgrn_lb_logits.astype(jnp.float32), axis=0), axis=0)
    lb_all = lb_all - lb_all[0:1]
    for l in range(DEPTH):
        x = x + hybrid_mixer(rmsnorm(x, mix_norm_g[l]), w_in[l], fox_f_bias[l], fox_norm_g[l],
                             lb_all[l], hgrn_norm_g[l], conv_w[l], conv_b[l],
                             conv_norm_g[l], conv_norm_b[l], w_out[l])
        x = x + cross_attention(rmsnorm(x, xatt_norm_g[l]), mem_n, w_xq[l], w_xkv[l], w_xo[l])
        x = x + hierarchical_moe(rmsnorm(x, ffn_norm_g[l]), w_group[l], b_group[l], w_router[l],
                                 b_router[l], w_gate[l], w_up[l], w_down[l])
    return rmsnorm(x, final_norm_g)
```

```python
import functools

import jax
import jax.numpy as jnp
from jax import lax
from jax.experimental import pallas as pl
from jax.experimental.pallas import tpu as pltpu

F32 = jnp.float32
BF16 = jnp.bfloat16
I32 = jnp.int32

EPS = 1e-6
LANES = 128
FOX_HEADS = 8
FOX_HEAD_DIM = 64
HGRN_HEADS = 4
HGRN_DIM = 64
HGRN_CHUNK = 16
HGRN_BLOCK = 128
CONV_WIDTH = 31
CONV_HALO = 32
CONV_GROUP = 64
XATT_HEADS = 4
N_GROUPS = 4
EXPERTS_PER_GROUP = 4
ROUTER_ROWS = 32
VMEM_LIMIT_BYTES = 56 * 1024 * 1024

NT_DIMS = (((1,), (1,)), ((), ()))


def _cparams(*sem):
    return pltpu.CompilerParams(dimension_semantics=sem, vmem_limit_bytes=VMEM_LIMIT_BYTES)


def _dot(a, b):
    return jnp.dot(a, b, preferred_element_type=F32)


def _dot_nt(a, b):
    return lax.dot_general(a, b, NT_DIMS, preferred_element_type=F32)


def _rms(x, g):
    ms = jnp.mean(x * x, axis=-1, keepdims=True)
    return x * lax.rsqrt(ms + EPS) * g


def _sigmoid(x):
    return 1.0 / (1.0 + jnp.exp(-x))


def _split2(x):
    hi = x.astype(BF16)
    lo = (x - hi.astype(F32)).astype(BF16)
    return hi, lo


def _split3(x):
    hi = x.astype(BF16)
    r = x - hi.astype(F32)
    mid = r.astype(BF16)
    lo = (r - mid.astype(F32)).astype(BF16)
    return hi, mid, lo


def _halves_select(lane_lo_mask, a, b):
    return jnp.where(lane_lo_mask, a, b)


def _pair_rms(o, lo_mask, width):
    o2 = o * o
    s_all = jnp.sum(o2, axis=-1, keepdims=True)
    s_lo = jnp.sum(jnp.where(lo_mask, o2, 0.0), axis=-1, keepdims=True)
    ms = jnp.where(lo_mask, s_lo, s_all - s_lo) * (1.0 / width)
    return o * lax.rsqrt(ms + EPS)


def _mem_kv_kernel(mem_ref, g_ref, w_ref, o_ref):
    t = _rms(mem_ref[...], g_ref[...]).astype(BF16)
    o_ref[...] = _dot(t, w_ref[...]).astype(BF16)


def _mem_kv(mem2d, g, w_xkv):
    depth, d, d2 = w_xkv.shape
    rows = mem2d.shape[0]
    return pl.pallas_call(
        _mem_kv_kernel,
        grid=(depth,),
        in_specs=[pl.BlockSpec((rows, d), lambda l: (0, 0)),
                  pl.BlockSpec((1, d), lambda l: (0, 0)),
                  pl.BlockSpec((None, d, d2), lambda l: (l, 0, 0))],
        out_specs=pl.BlockSpec((None, rows, d2), lambda l: (l, 0, 0)),
        out_shape=jax.ShapeDtypeStruct((depth, rows, d2), BF16),
        compiler_params=_cparams("arbitrary"),
        name="mem_kv",
    )(mem2d, g, w_xkv)


def _mix_in_kernel(x_ref, g_ref, wqkv_ref, wh_ref, wc_ref, wff_ref, fb_ref,
                   qkv_ref, hy_ref, cy_ref, lf_ref):
    t = _rms(x_ref[...], g_ref[...]).astype(BF16)
    qkv_ref[...] = _dot(t, wqkv_ref[...]).astype(BF16)
    hy_ref[...] = _dot(t, wh_ref[...]).astype(BF16)
    cy_ref[...] = _dot(t, wc_ref[...]).astype(BF16)
    z = _dot_nt(wff_ref[...], t) + fb_ref[...]
    lf_ref[...] = jnp.minimum(z, 0.0) - jnp.log1p(jnp.exp(-jnp.abs(z)))


def _mix_in(x2d, g, wqkv, wh, wc, wff, fb, batch, seq, tm):
    n, d = batch * seq, x2d.shape[1]
    nb = seq // tm
    full = lambda a: pl.BlockSpec(a.shape, lambda r: (0,) * a.ndim)
    return pl.pallas_call(
        _mix_in_kernel,
        grid=(n // tm,),
        in_specs=[pl.BlockSpec((tm, d), lambda r: (r, 0)), full(g), full(wqkv), full(wh), full(wc),
                  full(wff), full(fb)],
        out_specs=[pl.BlockSpec((tm, wqkv.shape[1]), lambda r: (r, 0)),
                   pl.BlockSpec((tm, wh.shape[1]), lambda r: (r, 0)),
                   pl.BlockSpec((tm, wc.shape[1]), lambda r: (r, 0)),
                   pl.BlockSpec((None, FOX_HEADS, tm), lambda r: (r // nb, 0, r % nb))],
        out_shape=[jax.ShapeDtypeStruct((n, wqkv.shape[1]), BF16),
                   jax.ShapeDtypeStruct((n, wh.shape[1]), BF16),
                   jax.ShapeDtypeStruct((n, wc.shape[1]), BF16),
                   jax.ShapeDtypeStruct((batch, FOX_HEADS, seq), F32)],
        compiler_params=_cparams("arbitrary"),
        name="mix_in",
    )(x2d, g, wqkv, wh, wc, wff, fb)


def _fox_cumsum_kernel(lf_ref, c_ref, *, blk):
    seq = lf_ref.shape[-1]
    r = lax.broadcasted_iota(I32, (blk, blk), 0)
    c = lax.broadcasted_iota(I32, (blk, blk), 1)
    upper = jnp.where(r <= c, 1.0, 0.0).astype(BF16)
    carry = jnp.zeros((FOX_HEADS, 1), F32)
    for b in range(seq // blk):
        x = lf_ref[:, b * blk:(b + 1) * blk]
        hi, mid, lo = _split3(x)
        cb = _dot(hi, upper) + _dot(mid, upper) + _dot(lo, upper) + carry
        for h in range(FOX_HEADS):
            c_ref[h, :, b * blk:(b + 1) * blk] = cb[h:h + 1, :]
        carry = cb[:, blk - 1:blk]


def _fox_cumsum(lf, blk=512):
    batch, heads, seq = lf.shape
    return pl.pallas_call(
        functools.partial(_fox_cumsum_kernel, blk=blk),
        grid=(batch,),
        in_specs=[pl.BlockSpec((None, heads, seq), lambda b: (b, 0, 0))],
        out_specs=pl.BlockSpec((None, heads, 1, seq), lambda b: (b, 0, 0, 0)),
        out_shape=jax.ShapeDtypeStruct((batch, heads, 1, seq), F32),
        compiler_params=_cparams("arbitrary"),
        name="fox_cumsum",
    )(lf)


def _fox_kernel(q_ref, k_ref, v_ref, c_ref, g_ref, o_ref, *, tq):
    i = pl.program_id(2)
    lane = lax.broadcasted_iota(I32, (1, LANES), 1)
    lo_mask = lane < FOX_HEAD_DIM
    q = q_ref[...]
    zero = jnp.zeros_like(q)
    qh = (jnp.where(lo_mask, q, zero), jnp.where(lo_mask, zero, q))
    q0 = pl.multiple_of(i * tq, tq)
    cq0 = tuple(c_ref[h, :, pl.ds(q0, LANES)][:, 0:1] for h in range(2))
    row = lax.broadcasted_iota(I32, (tq, tq), 0)
    col = lax.broadcasted_iota(I32, (tq, tq), 1)
    causal = row >= col

    def block(j, carry, masked):
        k0 = pl.multiple_of(j * tq, tq)
        kb = k_ref[pl.ds(k0, tq), :]
        vb = v_ref[pl.ds(k0, tq), :]
        out = []
        for h in range(2):
            m, l, acc = carry[h]
            s = _dot_nt(qh[h], kb) + (cq0[h] - c_ref[h, :, pl.ds(k0, tq)])
            if masked:
                s = jnp.where(causal, s, -jnp.inf)
            m_new = jnp.maximum(m, jnp.max(s, axis=-1, keepdims=True))
            alpha = jnp.exp(m - m_new)
            p = jnp.exp(s - m_new)
            l = alpha * l + jnp.sum(p, axis=-1, keepdims=True)
            acc = alpha * acc + _dot(p.astype(BF16), vb)
            out.append((m_new, l, acc))
        return tuple(out)

    init = tuple((jnp.full((tq, 1), -jnp.inf, F32), jnp.zeros((tq, 1), F32), jnp.zeros((tq, LANES), F32))
                 for _ in range(2))
    carry = block(i, init, True)
    carry = lax.fori_loop(0, i, lambda j, c: block(j, c, False), carry)
    (_, l0, a0), (_, l1, a1) = carry
    o = jnp.where(lo_mask, a0 * (1.0 / l0), a1 * (1.0 / l1))
    o_ref[...] = (_pair_rms(o, lo_mask, FOX_HEAD_DIM) * g_ref[...]).astype(o_ref.dtype)


def _fox_attention(qkv, c, gain, batch, seq, tq):
    n = qkv.shape[0]
    pairs = FOX_HEADS // 2
    nq = seq // tq
    return pl.pallas_call(
        functools.partial(_fox_kernel, tq=tq),
        grid=(batch, pairs, nq),
        in_specs=[pl.BlockSpec((tq, LANES), lambda b, p, i: (b * nq + i, p)),
                  pl.BlockSpec((seq, LANES), lambda b, p, i: (b, pairs + p)),
                  pl.BlockSpec((seq, LANES), lambda b, p, i: (b, 2 * pairs + p)),
                  pl.BlockSpec((None, 2, 1, seq), lambda b, p, i: (b, p, 0, 0)),
                  pl.BlockSpec((1, LANES), lambda b, p, i: (0, p))],
        out_specs=pl.BlockSpec((tq, LANES), lambda b, p, i: (b * nq + i, p)),
        out_shape=jax.ShapeDtypeStruct((n, pairs * LANES), BF16),
        compiler_params=_cparams("parallel", "parallel", "arbitrary"),
        name="fox_attention",
    )(qkv, qkv, qkv, c, gain)


def _hgrn_kernel(q_ref, f_ref, v_ref, gate_ref, lbz_ref, gain_ref, o_ref, st_ref, *, layer, n_sub):
    T, C = HGRN_BLOCK, HGRN_CHUNK
    nchunk = T // C
    i = pl.program_id(2)

    @pl.when(i == 0)
    def _():
        st_ref[...] = jnp.zeros_like(st_ref)

    z = lbz_ref[...]
    e = jnp.exp(z - jnp.max(z, axis=0, keepdims=True))
    pz = e / jnp.sum(e, axis=0, keepdims=True)
    lb = jnp.zeros((1, LANES), F32)
    for j in range(1, layer + 1):
        lb = lb + pz[j:j + 1, :]

    lane = lax.broadcasted_iota(I32, (1, LANES), 1)
    lo_mask = lane < HGRN_DIM
    r = lax.broadcasted_iota(I32, (T, T), 0)
    c = lax.broadcasted_iota(I32, (T, T), 1)
    same_chunk = (r // C) == (c // C)
    one = lambda m: jnp.where(m, 1.0, 0.0).astype(BF16)
    scan_mat = jnp.concatenate([one(same_chunk & (c <= r)),
                                one(same_chunk & ((c % C) <= C // 2)),
                                one(same_chunk)], axis=0)
    intra_mask = same_chunk & (c <= r)
    chunk_of_row = lax.broadcasted_iota(I32, (T, LANES), 0) // C
    vr = lax.broadcasted_iota(I32, (LANES, nchunk * LANES), 0)
    kc = lax.broadcasted_iota(I32, (LANES, nchunk * LANES), 1)
    same_head = (vr < HGRN_DIM) == ((kc % LANES) < HGRN_DIM)

    state = st_ref[...]
    for sb in range(n_sub):
        rows = pl.ds(sb * T, T)
        q = q_ref[rows, :].astype(F32)
        f = lb + (1.0 - lb) * _sigmoid(f_ref[rows, :].astype(F32))
        lg = jnp.log(f)
        kk = 1.0 - f
        hi, lo = _split2(lg)
        sc = _dot(scan_mat, jnp.concatenate([hi, lo], axis=1))
        sc = sc[:, :LANES] + sc[:, LANES:]
        b, b_mid, b_last = sc[:T], sc[T:2 * T], sc[2 * T:]
        q_in = (q * jnp.exp(b - b_mid)).astype(BF16)
        k_in = (kk * jnp.exp(b_mid - b)).astype(BF16)
        k_out = kk * jnp.exp(b_last - b)
        q_out = (q * jnp.exp(b)).astype(BF16)
        decay = jnp.exp(b_last)
        v = v_ref[rows, :]
        zq = jnp.zeros_like(q_in)
        o_heads = []
        for h, qm in enumerate((jnp.where(lo_mask, q_in, zq), jnp.where(lo_mask, zq, q_in))):
            a = jnp.where(intra_mask, _dot_nt(qm, k_in), 0.0)
            o_heads.append(_dot(a.astype(BF16), v))
        o = jnp.where(lo_mask, o_heads[0], o_heads[1])
        k_exp = jnp.concatenate([jnp.where(chunk_of_row == j, k_out, 0.0) for j in range(nchunk)],
                                axis=1).astype(BF16)
        v_t = v.astype(F32).T.astype(BF16)
        u_t = jnp.where(same_head, _dot(v_t, k_exp), 0.0)
        prev = []
        for j in range(nchunk):
            prev.append(state.astype(BF16))
            state = state * decay[j * C:j * C + 1, :] + u_t[:, j * LANES:(j + 1) * LANES]
        o_all = _dot_nt(q_out, jnp.concatenate(prev, axis=0))
        for j in range(nchunk):
            o = o + jnp.where(chunk_of_row == j, o_all[:, j * LANES:(j + 1) * LANES], 0.0)
        gate = gate_ref[rows, :].astype(F32)
        y = _pair_rms(o, lo_mask, HGRN_DIM) * gain_ref[...] * (gate * _sigmoid(gate))
        o_ref[rows, :] = y.astype(o_ref.dtype)
    st_ref[...] = state


def _hgrn(hy, lb_logits, gain, layer, batch, seq, tg):
    n = hy.shape[0]
    pairs = HGRN_HEADS // 2
    nb = seq // tg
    depth = lb_logits.shape[0]
    col = lambda k: pl.BlockSpec((tg, LANES), lambda b, p, i, k=k: (b * nb + i, k * pairs + p))
    return pl.pallas_call(
        functools.partial(_hgrn_kernel, layer=layer, n_sub=tg // HGRN_BLOCK),
        grid=(batch, pairs, nb),
        in_specs=[col(0), col(1), col(2), col(3),
                  pl.BlockSpec((depth, LANES), lambda b, p, i: (0, p)),
                  pl.BlockSpec((1, LANES), lambda b, p, i: (0, p))],
        out_specs=pl.BlockSpec((tg, LANES), lambda b, p, i: (b * nb + i, p)),
        out_shape=jax.ShapeDtypeStruct((n, pairs * LANES), BF16),
        scratch_shapes=[pltpu.VMEM((LANES, LANES), F32)],
        compiler_params=_cparams("parallel", "parallel", "arbitrary"),
        name="hgrn",
    )(hy, hy, hy, hy, lb_logits, gain)


def _conv_kernel(cu_ref, cg_ref, w_ref, b_ref, ng_ref, nb_ref, o_ref, a_scr, *, tm):
    i = pl.program_id(1)

    @pl.when(i == 0)
    def _():
        a_scr[0:CONV_HALO, :] = jnp.zeros((CONV_HALO, a_scr.shape[1]), F32)

    @pl.when(i > 0)
    def _():
        a_scr[0:CONV_HALO, :] = a_scr[tm:tm + CONV_HALO, :]

    a_scr[CONV_HALO:CONV_HALO + tm, :] = cu_ref[...].astype(F32) * _sigmoid(cg_ref[...].astype(F32))
    acc = jnp.zeros((tm, a_scr.shape[1]), F32) + b_ref[...]
    first = CONV_HALO - (CONV_WIDTH - 1)
    for w in range(CONV_WIDTH):
        acc = acc + a_scr[first + w:first + w + tm, :] * w_ref[w:w + 1, :]
    lane = lax.broadcasted_iota(I32, (1, LANES), 1)
    lo_mask = lane < CONV_GROUP
    halves = []
    for hh in range(acc.shape[1] // LANES):
        xh = acc[:, hh * LANES:(hh + 1) * LANES]
        s_all = jnp.sum(xh, axis=-1, keepdims=True)
        s_lo = jnp.sum(jnp.where(lo_mask, xh, 0.0), axis=-1, keepdims=True)
        d = xh - jnp.where(lo_mask, s_lo, s_all - s_lo) * (1.0 / CONV_GROUP)
        halves.append(_pair_rms(d, lo_mask, CONV_GROUP))
    y = jnp.concatenate(halves, axis=1) * ng_ref[...] + nb_ref[...]
    o_ref[...] = (y * _sigmoid(y)).astype(o_ref.dtype)


def _conv(cy, w, b, ng, nb_, batch, seq, tm):
    n = cy.shape[0]
    ch = cy.shape[1] // 2
    nb = seq // tm
    full = lambda a: pl.BlockSpec(a.shape, lambda bb, i: (0,) * a.ndim)
    return pl.pallas_call(
        functools.partial(_conv_kernel, tm=tm),
        grid=(batch, nb),
        in_specs=[pl.BlockSpec((tm, ch), lambda bb, i: (bb * nb + i, 0)),
                  pl.BlockSpec((tm, ch), lambda bb, i: (bb * nb + i, 1)),
                  full(w), full(b), full(ng), full(nb_)],
        out_specs=pl.BlockSpec((tm, ch), lambda bb, i: (bb * nb + i, 0)),
        out_shape=jax.ShapeDtypeStruct((n, ch), BF16),
        scratch_shapes=[pltpu.VMEM((CONV_HALO + tm, ch), F32)],
        compiler_params=_cparams("parallel", "arbitrary"),
        name="conv",
    )(cy, cy, w, b, ng, nb_)


def _mix_out_kernel(fo_ref, ho_ref, co_ref, x_ref, wo_ref, g_ref, wq_ref, x1_ref, xq_ref):
    nf, nh = fo_ref.shape[1], ho_ref.shape[1]
    y = (_dot(fo_ref[...], wo_ref[0:nf, :]) + _dot(ho_ref[...], wo_ref[nf:nf + nh, :])
         + _dot(co_ref[...], wo_ref[nf + nh:, :]))
    x1 = x_ref[...] + y
    x1_ref[...] = x1
    xq_ref[...] = _dot(_rms(x1, g_ref[...]).astype(BF16), wq_ref[...]).astype(BF16)


def _mix_out(fo, ho, co, x2d, wo, g, wq, tm):
    n, d = fo.shape[0], x2d.shape[1]
    full = lambda a: pl.BlockSpec(a.shape, lambda r: (0,) * a.ndim)
    rowblk = lambda a: pl.BlockSpec((tm, a.shape[1]), lambda r: (r, 0))
    return pl.pallas_call(
        _mix_out_kernel,
        grid=(n // tm,),
        in_specs=[rowblk(fo), rowblk(ho), rowblk(co), rowblk(x2d), full(wo), full(g), full(wq)],
        out_specs=[pl.BlockSpec((tm, d), lambda r: (r, 0)), pl.BlockSpec((tm, d), lambda r: (r, 0))],
        out_shape=[jax.ShapeDtypeStruct((n, d), F32), jax.ShapeDtypeStruct((n, d), BF16)],
        compiler_params=_cparams("arbitrary"),
        name="mix_out",
    )(fo, ho, co, x2d, wo, g, wq)


def _router_logits_t(t, wr_hi, wr_lo, br):
    th, tl = _split2(t)
    return _dot_nt(wr_hi, th) + _dot_nt(wr_lo, th) + _dot_nt(wr_hi, tl) + br


def _xattn_kernel(xq_ref, kv_ref, x1_ref, wxo_ref, g_ref, wrh_ref, wrl_ref, br_ref,
                  x2_ref, route_ref, cnt_ref, cnt_scr, *, tm):
    step = pl.program_id(0)

    @pl.when(step == 0)
    def _():
        cnt_scr[...] = jnp.zeros_like(cnt_scr)

    d = xq_ref.shape[1]
    hd = d // XATT_HEADS
    heads = []
    for h in range(XATT_HEADS):
        qh = xq_ref[:, h * hd:(h + 1) * hd]
        kh = kv_ref[:, h * hd:(h + 1) * hd]
        vh = kv_ref[:, d + h * hd:d + (h + 1) * hd]
        s = _dot_nt(qh, kh)
        p = jnp.exp(s - jnp.max(s, axis=-1, keepdims=True))
        l = jnp.sum(p, axis=-1, keepdims=True)
        heads.append((_dot(p.astype(BF16), vh) * (1.0 / l)).astype(BF16))
    x2 = x1_ref[...] + _dot(jnp.concatenate(heads, axis=1), wxo_ref[...])
    x2_ref[...] = x2
    logits = _router_logits_t(_rms(x2, g_ref[...]), wrh_ref[...], wrl_ref[...], br_ref[...])
    gl = [logits[g:g + 1, :] for g in range(N_GROUPS)]
    gmax = jnp.maximum(jnp.maximum(gl[0], gl[1]), jnp.maximum(gl[2], gl[3]))
    gidx = jnp.where(gl[0] >= gmax, 0, jnp.where(gl[1] >= gmax, 1, jnp.where(gl[2] >= gmax, 2, 3)))
    rows8 = lax.broadcasted_iota(I32, (8, tm), 0)
    onehot = jnp.where(rows8 == gidx, 1.0, 0.0)
    r = lax.broadcasted_iota(I32, (tm, tm), 0)
    c = lax.broadcasted_iota(I32, (tm, tm), 1)
    incl = _dot(onehot.astype(BF16), jnp.where(r <= c, 1.0, 0.0).astype(BF16))
    before = cnt_scr[:, 0:1]
    rank = jnp.sum(onehot * (incl - 1.0 + before), axis=0, keepdims=True)
    route_ref[...] = jnp.where(rows8 == 0, gidx, jnp.where(rows8 == 1, rank.astype(I32), 0))
    cnt_scr[...] = cnt_scr[...] + incl[:, tm - 1:tm]
    cnt_ref[...] = cnt_scr[...].astype(I32)


def _xattn(xq, kv, x1, wxo, g, wrh, wrl, br, batch, seq, tm):
    n, d = x1.shape
    nb = seq // tm
    mem = kv.shape[1]
    full = lambda a: pl.BlockSpec(a.shape, lambda r: (0,) * a.ndim)
    return pl.pallas_call(
        functools.partial(_xattn_kernel, tm=tm),
        grid=(n // tm,),
        in_specs=[pl.BlockSpec((tm, d), lambda r: (r, 0)),
                  pl.BlockSpec((None, mem, 2 * d), lambda r: (r // nb, 0, 0)),
                  pl.BlockSpec((tm, d), lambda r: (r, 0)),
                  full(wxo), full(g), full(wrh), full(wrl), full(br)],
        out_specs=[pl.BlockSpec((tm, d), lambda r: (r, 0)),
                   pl.BlockSpec((8, tm), lambda r: (0, r)),
                   pl.BlockSpec((8, LANES), lambda r: (0, 0))],
        out_shape=[jax.ShapeDtypeStruct((n, d), F32),
                   jax.ShapeDtypeStruct((8, n), I32),
                   jax.ShapeDtypeStruct((8, LANES), I32)],
        scratch_shapes=[pltpu.VMEM((8, LANES), F32)],
        compiler_params=_cparams("arbitrary"),
        name="xattn",
    )(xq, kv, x1, wxo, g, wrh, wrl, br)


def _plan_kernel(gidx_ref, rank_ref, cnt_ref, tos_ref, tg_ref, off_scr, *, n_tok, tmm):
    n_slots = tos_ref.shape[0]
    n_tiles = tg_ref.shape[0]
    off = 0
    last_group = 0
    ends = []
    for g in range(N_GROUPS):
        off_scr[g] = off
        cnt = cnt_ref[g]
        off = off + ((cnt + (tmm - 1)) // tmm) * tmm
        ends.append(off)
        last_group = jnp.where(cnt > 0, g, last_group)

    def clear(s, _):
        tos_ref[s] = -1
        return 0

    lax.fori_loop(0, n_slots, clear, 0, unroll=8)

    def place(t, _):
        tos_ref[off_scr[gidx_ref[t]] + rank_ref[t]] = t
        return 0

    lax.fori_loop(0, n_tok, place, 0, unroll=8)

    def tile(k, _):
        start = k * tmm
        g = ((start >= ends[0]).astype(I32) + (start >= ends[1]).astype(I32) + (start >= ends[2]).astype(I32))
        tg_ref[k] = jnp.minimum(g, last_group)
        return 0

    lax.fori_loop(0, n_tiles, tile, 0)


def _route_plan(gidx, rank, cnt, n_tok, tmm):
    n_tiles = n_tok // tmm + N_GROUPS
    smem = pl.BlockSpec(memory_space=pltpu.SMEM)
    return pl.pallas_call(
        functools.partial(_plan_kernel, n_tok=n_tok, tmm=tmm),
        in_specs=[smem, smem, smem],
        out_specs=[smem, smem],
        out_shape=[jax.ShapeDtypeStruct((n_tiles * tmm,), I32), jax.ShapeDtypeStruct((n_tiles,), I32)],
        scratch_shapes=[pltpu.SMEM((N_GROUPS,), I32)],
        name="route_plan",
    )(gidx, rank, cnt)


def _moe_kernel(tos_ref, tg_ref, x2_hbm, g_ref, wrh_ref, wrl_ref, br_ref, wg_ref, wu_ref, wd_ref, fg_ref,
                x3_hbm, xb, ob, gsem, ssem, *, tmm, n_tok, final):
    k = pl.program_id(0)
    nk = pl.num_programs(0)
    slot = k % 2

    def gather_copy(tile, sl, r):
        tok = jnp.maximum(tos_ref[tile * tmm + r], 0)
        return pltpu.make_async_copy(x2_hbm.at[pl.ds(tok, 1)], xb.at[sl, pl.ds(r, 1)], gsem.at[sl])

    def scatter_copy(tile, sl, r):
        tok = tos_ref[tile * tmm + r]
        dst = jnp.where(tok >= 0, tok, n_tok + sl * tmm + r)
        return pltpu.make_async_copy(ob.at[sl, pl.ds(r, 1)], x3_hbm.at[pl.ds(dst, 1)], ssem.at[sl])

    def for_rows(fn):
        def body(r, _):
            fn(r)
            return 0
        lax.fori_loop(0, tmm, body, 0, unroll=8)

    @pl.when(k == 0)
    def _():
        for_rows(lambda r: gather_copy(0, 0, r).start())

    for_rows(lambda r: gather_copy(k, slot, r).wait())

    @pl.when(k + 1 < nk)
    def _():
        for_rows(lambda r: gather_copy(k + 1, 1 - slot, r).start())

    @pl.when(k >= 2)
    def _():
        for_rows(lambda r: scatter_copy(k - 2, slot, r).wait())

    x = xb[slot]
    t = _rms(x, g_ref[...])
    tb = t.astype(BF16)
    th, tl = _split2(t)
    logits = _dot(th, wrh_ref[...]) + _dot(th, wrl_ref[...]) + _dot(tl, wrh_ref[...]) + br_ref[...]
    grp = tg_ref[k]
    lane = lax.broadcasted_iota(I32, (1, LANES), 1)
    lanef = lane.astype(F32)
    rmax = lambda a: jnp.max(a, axis=-1, keepdims=True)
    rsum = lambda a: jnp.sum(a, axis=-1, keepdims=True)
    gmask = lane < N_GROUPS
    gmax = rmax(jnp.where(gmask, logits, -jnp.inf))
    zg = rsum(jnp.where(gmask, jnp.exp(logits - gmax), 0.0))
    p_group = jnp.exp(rsum(jnp.where(lane == grp, logits, 0.0)) - gmax) / zg
    e_lo = N_GROUPS + EXPERTS_PER_GROUP * grp
    emask = (lane >= e_lo) & (lane < e_lo + EXPERTS_PER_GROUP)
    em = jnp.where(emask, logits, -jnp.inf)
    e1 = rmax(em)
    i1 = jnp.min(jnp.where(em == e1, lanef, 1e9), axis=-1, keepdims=True)
    em2 = jnp.where(lanef == i1, -jnp.inf, em)
    e2 = rmax(em2)
    i2 = jnp.min(jnp.where(em2 == e2, lanef, 1e9), axis=-1, keepdims=True)
    ze = rsum(jnp.where(emask, jnp.exp(logits - e1), 0.0))
    p1 = 1.0 / ze
    p2 = jnp.exp(e2 - e1) / ze
    w1 = p_group * (p1 / (p1 + p2))
    w2 = p_group * (p2 / (p1 + p2))
    hid = _dot(tb, wg_ref[...])
    hid = hid * _sigmoid(hid) * _dot(tb, wu_ref[...])
    ff = hid.shape[1] // EXPERTS_PER_GROUP
    cw_lanes = jnp.where(lanef == i1, w1, 0.0) + jnp.where(lanef == i2, w2, 0.0)
    parts = []
    for e in range(EXPERTS_PER_GROUP):
        cw = rsum(jnp.where(lane == e_lo + e, cw_lanes, 0.0))
        parts.append((hid[:, e * ff:(e + 1) * ff] * cw).astype(BF16))
    y = x + _dot(jnp.concatenate(parts, axis=1), wd_ref[...])
    if final:
        y = _rms(y, fg_ref[...])
    ob[slot] = y

    for_rows(lambda r: scatter_copy(k, slot, r).start())

    @pl.when(k == nk - 1)
    def _():
        @pl.when(k >= 1)
        def _():
            for_rows(lambda r: scatter_copy(k - 1, 1 - slot, r).wait())
        for_rows(lambda r: scatter_copy(k, slot, r).wait())


def _moe(tos, tg, x2, g, wrh_t, wrl_t, br_row, wg, wu, wd, fg, tmm, final):
    n_tok, d = x2.shape
    n_tiles = tg.shape[0]
    full = lambda a: pl.BlockSpec(a.shape, lambda k, tos, tg: (0,) * a.ndim)
    grp = lambda a: pl.BlockSpec((None,) + a.shape[1:], lambda k, tos, tg: (tg[k], 0, 0))
    grid_spec = pltpu.PrefetchScalarGridSpec(
        num_scalar_prefetch=2,
        grid=(n_tiles,),
        in_specs=[pl.BlockSpec(memory_space=pl.ANY), full(g), full(wrh_t), full(wrl_t), full(br_row),
                  grp(wg), grp(wu), grp(wd), full(fg)],
        out_specs=pl.BlockSpec(memory_space=pl.ANY),
        scratch_shapes=[pltpu.VMEM((2, tmm, d), F32), pltpu.VMEM((2, tmm, d), F32),
                        pltpu.SemaphoreType.DMA((2,)), pltpu.SemaphoreType.DMA((2,))],
    )
    return pl.pallas_call(
        functools.partial(_moe_kernel, tmm=tmm, n_tok=n_tok, final=final),
        grid_spec=grid_spec,
        out_shape=jax.ShapeDtypeStruct((n_tok + 2 * tmm, d), F32),
        compiler_params=_cparams("arbitrary"),
        name="moe",
    )(tos, tg, x2, g, wrh_t, wrl_t, br_row, wg, wu, wd, fg)


def _row(v):
    return v.reshape(1, -1)


def kernel(x, mem, mix_norm_g, w_in, fox_f_bias, fox_norm_g, hgrn_lb_logits, hgrn_norm_g, conv_w, conv_b,
           conv_norm_g, conv_norm_b, w_out, xatt_norm_g, mem_norm_g, w_xq, w_xkv, w_xo, ffn_norm_g,
           w_group, b_group, w_router, b_router, w_gate, w_up, w_down, final_norm_g):
    batch, seq, d = x.shape
    depth = w_in.shape[0]
    n = batch * seq
    tm = min(512, seq)
    tmm = min(512, seq)
    fw = FOX_HEADS * FOX_HEAD_DIM
    hw = HGRN_HEADS * HGRN_DIM
    cw = conv_w.shape[-1]
    assert seq % tm == 0 and seq % HGRN_BLOCK == 0 and d == fw + hw + cw

    offs = [0]
    for width in (fw, fw, fw, FOX_HEADS, hw, hw, hw, hw, cw, cw):
        offs.append(offs[-1] + width)
    seg = lambda a, i, j: a[..., offs[i]:offs[j]]
    wqkv = jnp.concatenate([seg(w_in, 0, 1) * (FOX_HEAD_DIM ** -0.5), seg(w_in, 1, 3)], axis=-1).astype(BF16)
    wff = jnp.swapaxes(seg(w_in, 3, 4), 1, 2).astype(BF16)
    wh = jnp.concatenate([seg(w_in, 4, 5), seg(w_in, 5, 6), seg(w_in, 6, 7), seg(w_in, 7, 8)], axis=-1).astype(BF16)
    wc = seg(w_in, 8, 10).astype(BF16)
    wo = w_out.astype(BF16)
    wxq = (w_xq * ((d // XATT_HEADS) ** -0.5)).astype(BF16)
    wxkv = w_xkv.astype(BF16)
    wxo = w_xo.astype(BF16)
    n_exp = N_GROUPS * EXPERTS_PER_GROUP
    wr = jnp.concatenate([w_group, w_router], axis=-1)
    wr_hi = wr.astype(BF16)
    wr_lo = (wr - wr_hi.astype(F32)).astype(BF16)
    pad_rows = lambda a: jnp.pad(jnp.swapaxes(a, 1, 2), ((0, 0), (0, ROUTER_ROWS - a.shape[2]), (0, 0)))
    pad_cols = lambda a: jnp.pad(a, ((0, 0), (0, 0), (0, LANES - a.shape[2])))
    br = jnp.concatenate([b_group, b_router], axis=-1)
    br_col = jnp.pad(br, ((0, 0), (0, ROUTER_ROWS - br.shape[1])))[:, :, None]
    br_row = jnp.pad(br, ((0, 0), (0, LANES - br.shape[1])))[:, None, :]
    ff = w_gate.shape[-1]
    to_cols = lambda w: jnp.transpose(w, (0, 1, 3, 2, 4)).reshape(depth, N_GROUPS, d, EXPERTS_PER_GROUP * ff).astype(BF16)
    wg, wu = to_cols(w_gate), to_cols(w_up)
    wd = w_down.reshape(depth, N_GROUPS, EXPERTS_PER_GROUP * ff, d).astype(BF16)
    conv_w_pad = jnp.pad(conv_w, ((0, 0), (0, CONV_HALO - CONV_WIDTH), (0, 0)))
    wrh_rows, wrl_rows, wrh_cols, wrl_cols = pad_rows(wr_hi), pad_rows(wr_lo), pad_cols(wr_hi), pad_cols(wr_lo)

    kv_all = _mem_kv(mem.reshape(-1, d), _row(mem_norm_g), wxkv).reshape(depth, batch, mem.shape[1], 2 * d)

    xs = x.reshape(n, d)
    for l in range(depth):
        qkv, hy, cy, lf = _mix_in(xs, _row(mix_norm_g[l]), wqkv[l], wh[l], wc[l], wff[l],
                                  fox_f_bias[l].reshape(-1, 1), batch, seq, tm)
        c = _fox_cumsum(lf, blk=min(512, seq))
        fo = _fox_attention(qkv, c, _row(fox_norm_g[l]), batch, seq, tm)
        ho = _hgrn(hy, hgrn_lb_logits, _row(hgrn_norm_g[l]), l, batch, seq, tm)
        co = _conv(cy, conv_w_pad[l], _row(conv_b[l]), _row(conv_norm_g[l]), _row(conv_norm_b[l]), batch, seq, tm)
        x1, xq = _mix_out(fo, ho, co, xs, wo[l], _row(xatt_norm_g[l]), wxq[l], tm)
        x2, route, cnt = _xattn(xq, kv_all[l], x1, wxo[l], _row(ffn_norm_g[l]),
                                wrh_rows[l], wrl_rows[l], br_col[l], batch, seq, tm)
        tos, tg = _route_plan(route[0], route[1], cnt[:N_GROUPS, 0], n, tmm)
        xs = _moe(tos, tg, x2, _row(ffn_norm_g[l]), wrh_cols[l], wrl_cols[l], br_row[l],
                  wg[l], wu[l], wd[l], _row(final_norm_g), tmm, final=(l == depth - 1))
    return xs[:n].reshape(batch, seq, d)
```

```python
import functools

import jax
import jax.numpy as jnp
from jax import lax
from jax.experimental import pallas as pl
from jax.experimental.pallas import tpu as pltpu

F32 = jnp.float32
BF16 = jnp.bfloat16
I32 = jnp.int32

EPS = 1e-6
LANES = 128
FOX_HEADS = 8
FOX_HEAD_DIM = 64
FOX_SKIP_EXPONENT = -45.0
FOX_NORM_SLACK = 1.02
HGRN_HEADS = 4
HGRN_DIM = 64
HGRN_CHUNK = 16
HGRN_BLOCK = 128
CONV_WIDTH = 31
CONV_HALO = 32
CONV_GROUP = 64
XATT_HEADS = 4
N_GROUPS = 4
EXPERTS_PER_GROUP = 4
ROUTER_ROWS = 32
VMEM_LIMIT_BYTES = 56 * 1024 * 1024

NT_DIMS = (((1,), (1,)), ((), ()))


def _cparams(*sem):
    return pltpu.CompilerParams(dimension_semantics=sem, vmem_limit_bytes=VMEM_LIMIT_BYTES)


def _dot(a, b):
    return jnp.dot(a, b, preferred_element_type=F32)


def _dot_nt(a, b):
    return lax.dot_general(a, b, NT_DIMS, preferred_element_type=F32)


def _rms(x, g):
    ms = jnp.mean(x * x, axis=-1, keepdims=True)
    return x * lax.rsqrt(ms + EPS) * g


def _sigmoid(x):
    return 1.0 / (1.0 + jnp.exp(-x))


def _split2(x):
    hi = x.astype(BF16)
    lo = (x - hi.astype(F32)).astype(BF16)
    return hi, lo


def _split3(x):
    hi = x.astype(BF16)
    r = x - hi.astype(F32)
    mid = r.astype(BF16)
    lo = (r - mid.astype(F32)).astype(BF16)
    return hi, mid, lo


def _halves_select(lane_lo_mask, a, b):
    return jnp.where(lane_lo_mask, a, b)


def _pair_rms(o, lo_mask, width):
    o2 = o * o
    s_all = jnp.sum(o2, axis=-1, keepdims=True)
    s_lo = jnp.sum(jnp.where(lo_mask, o2, 0.0), axis=-1, keepdims=True)
    ms = jnp.where(lo_mask, s_lo, s_all - s_lo) * (1.0 / width)
    return o * lax.rsqrt(ms + EPS)


def _mem_kv_kernel(mem_ref, g_ref, w_ref, o_ref):
    t = _rms(mem_ref[...], g_ref[...]).astype(BF16)
    o_ref[...] = _dot(t, w_ref[...]).astype(BF16)


def _mem_kv(mem2d, g, w_xkv):
    depth, d, d2 = w_xkv.shape
    rows = mem2d.shape[0]
    return pl.pallas_call(
        _mem_kv_kernel,
        grid=(depth,),
        in_specs=[pl.BlockSpec((rows, d), lambda l: (0, 0)),
                  pl.BlockSpec((1, d), lambda l: (0, 0)),
                  pl.BlockSpec((None, d, d2), lambda l: (l, 0, 0))],
        out_specs=pl.BlockSpec((None, rows, d2), lambda l: (l, 0, 0)),
        out_shape=jax.ShapeDtypeStruct((depth, rows, d2), BF16),
        compiler_params=_cparams("arbitrary"),
        name="mem_kv",
    )(mem2d, g, w_xkv)


def _head_norm_max(y, first_col):
    lane = lax.broadcasted_iota(I32, (1, LANES), 1)
    lo_mask = lane < FOX_HEAD_DIM
    head_row = lax.broadcasted_iota(I32, (FOX_HEADS, LANES), 0)
    out = jnp.zeros((FOX_HEADS, LANES), F32)
    for p in range(FOX_HEADS // 2):
        blk = y[:, first_col + p * LANES:first_col + (p + 1) * LANES]
        sq = blk * blk
        for h, m in ((2 * p, lo_mask), (2 * p + 1, jnp.logical_not(lo_mask))):
            n2 = jnp.sum(jnp.where(m, sq, 0.0), axis=-1, keepdims=True)
            out = jnp.where(head_row == h, jnp.sqrt(jnp.max(n2, axis=0, keepdims=True)), out)
    return out


def _mix_in_kernel(x_ref, g_ref, wqkv_ref, wh_ref, wc_ref, wff_ref, fb_ref,
                   qkv_ref, hy_ref, cy_ref, lf_ref, qn_ref, kn_ref, kn_scr, *, nb):
    t = _rms(x_ref[...], g_ref[...]).astype(BF16)
    y = _dot(t, wqkv_ref[...])
    qkv_ref[...] = y.astype(BF16)
    hy_ref[...] = _dot(t, wh_ref[...]).astype(BF16)
    cy_ref[...] = _dot(t, wc_ref[...]).astype(BF16)
    z = _dot_nt(wff_ref[...], t) + fb_ref[...]
    lf_ref[...] = jnp.minimum(z, 0.0) - jnp.log1p(jnp.exp(-jnp.abs(z)))
    width = FOX_HEADS * FOX_HEAD_DIM
    qn_ref[...] = _head_norm_max(y, 0)

    @pl.when(pl.program_id(0) % nb == 0)
    def _():
        kn_scr[...] = jnp.zeros_like(kn_scr)

    kn_scr[...] = jnp.maximum(kn_scr[...], _head_norm_max(y, width))
    kn_ref[...] = kn_scr[...]


def _mix_in(x2d, g, wqkv, wh, wc, wff, fb, batch, seq, tm):
    n, d = batch * seq, x2d.shape[1]
    nb = seq // tm
    full = lambda a: pl.BlockSpec(a.shape, lambda r: (0,) * a.ndim)
    return pl.pallas_call(
        functools.partial(_mix_in_kernel, nb=nb),
        grid=(n // tm,),
        in_specs=[pl.BlockSpec((tm, d), lambda r: (r, 0)), full(g), full(wqkv), full(wh), full(wc),
                  full(wff), full(fb)],
        out_specs=[pl.BlockSpec((tm, wqkv.shape[1]), lambda r: (r, 0)),
                   pl.BlockSpec((tm, wh.shape[1]), lambda r: (r, 0)),
                   pl.BlockSpec((tm, wc.shape[1]), lambda r: (r, 0)),
                   pl.BlockSpec((None, FOX_HEADS, tm), lambda r: (r // nb, 0, r % nb)),
                   pl.BlockSpec((None, FOX_HEADS, LANES), lambda r: (r, 0, 0)),
                   pl.BlockSpec((None, FOX_HEADS, LANES), lambda r: (r // nb, 0, 0))],
        out_shape=[jax.ShapeDtypeStruct((n, wqkv.shape[1]), BF16),
                   jax.ShapeDtypeStruct((n, wh.shape[1]), BF16),
                   jax.ShapeDtypeStruct((n, wc.shape[1]), BF16),
                   jax.ShapeDtypeStruct((batch, FOX_HEADS, seq), F32),
                   jax.ShapeDtypeStruct((n // tm, FOX_HEADS, LANES), F32),
                   jax.ShapeDtypeStruct((batch, FOX_HEADS, LANES), F32)],
        scratch_shapes=[pltpu.VMEM((FOX_HEADS, LANES), F32)],
        compiler_params=_cparams("arbitrary"),
        name="mix_in",
    )(x2d, g, wqkv, wh, wc, wff, fb)


def _fox_plan_kernel(lf_ref, qn_ref, kn_ref, c_ref, keep_ref, *, blk):
    seq = lf_ref.shape[-1]
    nblk = seq // blk
    r = lax.broadcasted_iota(I32, (blk, blk), 0)
    c = lax.broadcasted_iota(I32, (blk, blk), 1)
    upper = jnp.where(r <= c, 1.0, 0.0).astype(BF16)
    lane = lax.broadcasted_iota(I32, (FOX_HEADS, LANES), 1)
    carry = jnp.zeros((FOX_HEADS, 1), F32)
    first = jnp.zeros((FOX_HEADS, LANES), F32)
    last = jnp.zeros((FOX_HEADS, LANES), F32)
    qn = jnp.zeros((FOX_HEADS, LANES), F32)
    for b in range(nblk):
        x = lf_ref[:, b * blk:(b + 1) * blk]
        hi, mid, lo = _split3(x)
        cb = _dot(hi, upper) + _dot(mid, upper) + _dot(lo, upper) + carry
        for h in range(FOX_HEADS):
            c_ref[h, :, b * blk:(b + 1) * blk] = cb[h:h + 1, :]
        carry = cb[:, blk - 1:blk]
        first = jnp.where(lane == b, cb[:, 0:1], first)
        last = jnp.where(lane == b, carry, last)
        qn = jnp.where(lane == b, qn_ref[b], qn)
    bound = FOX_NORM_SLACK * 2.0 * qn * kn_ref[...] + first
    keep = jnp.zeros((FOX_HEADS, LANES), I32)
    for i in range(nblk):
        live = (bound[:, i:i + 1] - last >= FOX_SKIP_EXPONENT) & (lane < i)
        keep = jnp.where(lane == i, jnp.sum(live.astype(I32), axis=1, keepdims=True), keep)
    keep_ref[...] = keep


def _fox_plan(lf, qn, kn, blk):
    batch, heads, seq = lf.shape
    nblk = seq // blk
    return pl.pallas_call(
        functools.partial(_fox_plan_kernel, blk=blk),
        grid=(batch,),
        in_specs=[pl.BlockSpec((None, heads, seq), lambda b: (b, 0, 0)),
                  pl.BlockSpec((nblk, heads, LANES), lambda b: (b, 0, 0)),
                  pl.BlockSpec((None, heads, LANES), lambda b: (b, 0, 0))],
        out_specs=[pl.BlockSpec((None, heads, 1, seq), lambda b: (b, 0, 0, 0)),
                   pl.BlockSpec((None, heads, LANES), lambda b: (b, 0, 0))],
        out_shape=[jax.ShapeDtypeStruct((batch, heads, 1, seq), F32),
                   jax.ShapeDtypeStruct((batch, heads, LANES), I32)],
        compiler_params=_cparams("arbitrary"),
        name="fox_plan",
    )(lf, qn, kn)


def _fox_kernel(keep_ref, q_ref, k_ref, v_ref, c_ref, g_ref, o_ref, *, tq):
    i = pl.program_id(2)
    head0 = (pl.program_id(0) * FOX_HEADS + 2 * pl.program_id(1)) * LANES
    n_prev = jnp.maximum(keep_ref[head0 + i], keep_ref[head0 + LANES + i])
    lane = lax.broadcasted_iota(I32, (1, LANES), 1)
    lo_mask = lane < FOX_HEAD_DIM
    q = q_ref[...]
    zero = jnp.zeros_like(q)
    qh = (jnp.where(lo_mask, q, zero), jnp.where(lo_mask, zero, q))
    q0 = pl.multiple_of(i * tq, tq)
    cq0 = tuple(c_ref[h, :, pl.ds(q0, LANES)][:, 0:1] for h in range(2))
    row = lax.broadcasted_iota(I32, (tq, tq), 0)
    col = lax.broadcasted_iota(I32, (tq, tq), 1)
    causal = row >= col

    def block(j, carry, masked):
        k0 = pl.multiple_of(j * tq, tq)
        kb = k_ref[pl.ds(k0, tq), :]
        vb = v_ref[pl.ds(k0, tq), :]
        out = []
        for h in range(2):
            m, l, acc = carry[h]
            s = _dot_nt(qh[h], kb) + (cq0[h] - c_ref[h, :, pl.ds(k0, tq)])
            if masked:
                s = jnp.where(causal, s, -jnp.inf)
            m_new = jnp.maximum(m, jnp.max(s, axis=-1, keepdims=True))
            alpha = jnp.exp(m - m_new)
            p = jnp.exp(s - m_new)
            l = alpha * l + jnp.sum(p, axis=-1, keepdims=True)
            acc = alpha * acc + _dot(p.astype(BF16), vb)
            out.append((m_new, l, acc))
        return tuple(out)

    init = tuple((jnp.full((tq, 1), -jnp.inf, F32), jnp.zeros((tq, 1), F32), jnp.zeros((tq, LANES), F32))
                 for _ in range(2))
    carry = block(i, init, True)
    carry = lax.fori_loop(i - n_prev, i, lambda j, c: block(j, c, False), carry)
    (_, l0, a0), (_, l1, a1) = carry
    o = jnp.where(lo_mask, a0 * (1.0 / l0), a1 * (1.0 / l1))
    o_ref[...] = (_pair_rms(o, lo_mask, FOX_HEAD_DIM) * g_ref[...]).astype(o_ref.dtype)


def _fox_attention(keep, qkv, c, gain, batch, seq, tq):
    n = qkv.shape[0]
    pairs = FOX_HEADS // 2
    nq = seq // tq
    grid_spec = pltpu.PrefetchScalarGridSpec(
        num_scalar_prefetch=1,
        grid=(batch, pairs, nq),
        in_specs=[pl.BlockSpec((tq, LANES), lambda b, p, i, keep: (b * nq + i, p)),
                  pl.BlockSpec((seq, LANES), lambda b, p, i, keep: (b, pairs + p)),
                  pl.BlockSpec((seq, LANES), lambda b, p, i, keep: (b, 2 * pairs + p)),
                  pl.BlockSpec((None, 2, 1, seq), lambda b, p, i, keep: (b, p, 0, 0)),
                  pl.BlockSpec((1, LANES), lambda b, p, i, keep: (0, p))],
        out_specs=pl.BlockSpec((tq, LANES), lambda b, p, i, keep: (b * nq + i, p)),
    )
    return pl.pallas_call(
        functools.partial(_fox_kernel, tq=tq),
        grid_spec=grid_spec,
        out_shape=jax.ShapeDtypeStruct((n, pairs * LANES), BF16),
        compiler_params=_cparams("parallel", "parallel", "arbitrary"),
        name="fox_attention",
    )(keep, qkv, qkv, qkv, c, gain)


def _hgrn_kernel(q_ref, f_ref, v_ref, gate_ref, lbz_ref, gain_ref, o_ref, st_ref, *, layer, n_sub):
    T, C = HGRN_BLOCK, HGRN_CHUNK
    nchunk = T // C
    i = pl.program_id(2)

    @pl.when(i == 0)
    def _():
        st_ref[...] = jnp.zeros_like(st_ref)

    z = lbz_ref[...]
    e = jnp.exp(z - jnp.max(z, axis=0, keepdims=True))
    pz = e / jnp.sum(e, axis=0, keepdims=True)
    lb = jnp.zeros((1, LANES), F32)
    for j in range(1, layer + 1):
        lb = lb + pz[j:j + 1, :]

    lane = lax.broadcasted_iota(I32, (1, LANES), 1)
    lo_mask = lane < HGRN_DIM
    r = lax.broadcasted_iota(I32, (T, T), 0)
    c = lax.broadcasted_iota(I32, (T, T), 1)
    same_chunk = (r // C) == (c // C)
    one = lambda m: jnp.where(m, 1.0, 0.0).astype(BF16)
    scan_mat = jnp.concatenate([one(same_chunk & (c <= r)),
                                one(same_chunk & ((c % C) <= C // 2)),
                                one(same_chunk)], axis=0)
    intra_mask = same_chunk & (c <= r)
    chunk_of_row = lax.broadcasted_iota(I32, (T, LANES), 0) // C
    vr = lax.broadcasted_iota(I32, (LANES, nchunk * LANES), 0)
    kc = lax.broadcasted_iota(I32, (LANES, nchunk * LANES), 1)
    same_head = (vr < HGRN_DIM) == ((kc % LANES) < HGRN_DIM)

    state = st_ref[...]
    for sb in range(n_sub):
        rows = pl.ds(sb * T, T)
        q = q_ref[rows, :].astype(F32)
        f = lb + (1.0 - lb) * _sigmoid(f_ref[rows, :].astype(F32))
        lg = jnp.log(f)
        kk = 1.0 - f
        hi, lo = _split2(lg)
        sc = _dot(scan_mat, jnp.concatenate([hi, lo], axis=1))
        sc = sc[:, :LANES] + sc[:, LANES:]
        b, b_mid, b_last = sc[:T], sc[T:2 * T], sc[2 * T:]
        q_in = (q * jnp.exp(b - b_mid)).astype(BF16)
        k_in = (kk * jnp.exp(b_mid - b)).astype(BF16)
        k_out = kk * jnp.exp(b_last - b)
        q_out = (q * jnp.exp(b)).astype(BF16)
        decay = jnp.exp(b_last)
        v = v_ref[rows, :]
        zq = jnp.zeros_like(q_in)
        o_heads = []
        for h, qm in enumerate((jnp.where(lo_mask, q_in, zq), jnp.where(lo_mask, zq, q_in))):
            a = jnp.where(intra_mask, _dot_nt(qm, k_in), 0.0)
            o_heads.append(_dot(a.astype(BF16), v))
        o = jnp.where(lo_mask, o_heads[0], o_heads[1])
        k_exp = jnp.concatenate([jnp.where(chunk_of_row == j, k_out, 0.0) for j in range(nchunk)],
                                axis=1).astype(BF16)
        v_t = v.astype(F32).T.astype(BF16)
        u_t = jnp.where(same_head, _dot(v_t, k_exp), 0.0)
        prev = []
        for j in range(nchunk):
            prev.append(state.astype(BF16))
            state = state * decay[j * C:j * C + 1, :] + u_t[:, j * LANES:(j + 1) * LANES]
        o_all = _dot_nt(q_out, jnp.concatenate(prev, axis=0))
        for j in range(nchunk):
            o = o + jnp.where(chunk_of_row == j, o_all[:, j * LANES:(j + 1) * LANES], 0.0)
        gate = gate_ref[rows, :].astype(F32)
        y = _pair_rms(o, lo_mask, HGRN_DIM) * gain_ref[...] * (gate * _sigmoid(gate))
        o_ref[rows, :] = y.astype(o_ref.dtype)
    st_ref[...] = state


def _hgrn(hy, lb_logits, gain, layer, batch, seq, tg):
    n = hy.shape[0]
    pairs = HGRN_HEADS // 2
    nb = seq // tg
    depth = lb_logits.shape[0]
    col = lambda k: pl.BlockSpec((tg, LANES), lambda b, p, i, k=k: (b * nb + i, k * pairs + p))
    return pl.pallas_call(
        functools.partial(_hgrn_kernel, layer=layer, n_sub=tg // HGRN_BLOCK),
        grid=(batch, pairs, nb),
        in_specs=[col(0), col(1), col(2), col(3),
                  pl.BlockSpec((depth, LANES), lambda b, p, i: (0, p)),
                  pl.BlockSpec((1, LANES), lambda b, p, i: (0, p))],
        out_specs=pl.BlockSpec((tg, LANES), lambda b, p, i: (b * nb + i, p)),
        out_shape=jax.ShapeDtypeStruct((n, pairs * LANES), BF16),
        scratch_shapes=[pltpu.VMEM((LANES, LANES), F32)],
        compiler_params=_cparams("parallel", "parallel", "arbitrary"),
        name="hgrn",
    )(hy, hy, hy, hy, lb_logits, gain)


def _conv_kernel(cu_ref, cg_ref, w_ref, b_ref, ng_ref, nb_ref, o_ref, a_scr, *, tm):
    i = pl.program_id(1)

    @pl.when(i == 0)
    def _():
        a_scr[0:CONV_HALO, :] = jnp.zeros((CONV_HALO, a_scr.shape[1]), F32)

    @pl.when(i > 0)
    def _():
        a_scr[0:CONV_HALO, :] = a_scr[tm:tm + CONV_HALO, :]

    a_scr[CONV_HALO:CONV_HALO + tm, :] = cu_ref[...].astype(F32) * _sigmoid(cg_ref[...].astype(F32))
    acc = jnp.zeros((tm, a_scr.shape[1]), F32) + b_ref[...]
    first = CONV_HALO - (CONV_WIDTH - 1)
    for w in range(CONV_WIDTH):
        acc = acc + a_scr[first + w:first + w + tm, :] * w_ref[w:w + 1, :]
    lane = lax.broadcasted_iota(I32, (1, LANES), 1)
    lo_mask = lane < CONV_GROUP
    halves = []
    for hh in range(acc.shape[1] // LANES):
        xh = acc[:, hh * LANES:(hh + 1) * LANES]
        s_all = jnp.sum(xh, axis=-1, keepdims=True)
        s_lo = jnp.sum(jnp.where(lo_mask, xh, 0.0), axis=-1, keepdims=True)
        d = xh - jnp.where(lo_mask, s_lo, s_all - s_lo) * (1.0 / CONV_GROUP)
        halves.append(_pair_rms(d, lo_mask, CONV_GROUP))
    y = jnp.concatenate(halves, axis=1) * ng_ref[...] + nb_ref[...]
    o_ref[...] = (y * _sigmoid(y)).astype(o_ref.dtype)


def _conv(cy, w, b, ng, nb_, batch, seq, tm):
    n = cy.shape[0]
    ch = cy.shape[1] // 2
    nb = seq // tm
    full = lambda a: pl.BlockSpec(a.shape, lambda bb, i: (0,) * a.ndim)
    return pl.pallas_call(
        functools.partial(_conv_kernel, tm=tm),
        grid=(batch, nb),
        in_specs=[pl.BlockSpec((tm, ch), lambda bb, i: (bb * nb + i, 0)),
                  pl.BlockSpec((tm, ch), lambda bb, i: (bb * nb + i, 1)),
                  full(w), full(b), full(ng), full(nb_)],
        out_specs=pl.BlockSpec((tm, ch), lambda bb, i: (bb * nb + i, 0)),
        out_shape=jax.ShapeDtypeStruct((n, ch), BF16),
        scratch_shapes=[pltpu.VMEM((CONV_HALO + tm, ch), F32)],
        compiler_params=_cparams("parallel", "arbitrary"),
        name="conv",
    )(cy, cy, w, b, ng, nb_)


def _mix_out_kernel(fo_ref, ho_ref, co_ref, x_ref, wo_ref, g_ref, wq_ref, x1_ref, xq_ref):
    nf, nh = fo_ref.shape[1], ho_ref.shape[1]
    y = (_dot(fo_ref[...], wo_ref[0:nf, :]) + _dot(ho_ref[...], wo_ref[nf:nf + nh, :])
         + _dot(co_ref[...], wo_ref[nf + nh:, :]))
    x1 = x_ref[...] + y
    x1_ref[...] = x1
    xq_ref[...] = _dot(_rms(x1, g_ref[...]).astype(BF16), wq_ref[...]).astype(BF16)


def _mix_out(fo, ho, co, x2d, wo, g, wq, tm):
    n, d = fo.shape[0], x2d.shape[1]
    full = lambda a: pl.BlockSpec(a.shape, lambda r: (0,) * a.ndim)
    rowblk = lambda a: pl.BlockSpec((tm, a.shape[1]), lambda r: (r, 0))
    return pl.pallas_call(
        _mix_out_kernel,
        grid=(n // tm,),
        in_specs=[rowblk(fo), rowblk(ho), rowblk(co), rowblk(x2d), full(wo), full(g), full(wq)],
        out_specs=[pl.BlockSpec((tm, d), lambda r: (r, 0)), pl.BlockSpec((tm, d), lambda r: (r, 0))],
        out_shape=[jax.ShapeDtypeStruct((n, d), F32), jax.ShapeDtypeStruct((n, d), BF16)],
        compiler_params=_cparams("arbitrary"),
        name="mix_out",
    )(fo, ho, co, x2d, wo, g, wq)


def _router_logits_t(t, wr_hi, wr_lo, br):
    th, tl = _split2(t)
    return _dot_nt(wr_hi, th) + _dot_nt(wr_lo, th) + _dot_nt(wr_hi, tl) + br


def _xattn_kernel(xq_ref, kv_ref, x1_ref, wxo_ref, g_ref, wrh_ref, wrl_ref, br_ref,
                  x2_ref, route_ref, cnt_ref, cnt_scr, *, tm):
    step = pl.program_id(0)

    @pl.when(step == 0)
    def _():
        cnt_scr[...] = jnp.zeros_like(cnt_scr)

    d = xq_ref.shape[1]
    hd = d // XATT_HEADS
    heads = []
    for h in range(XATT_HEADS):
        qh = xq_ref[:, h * hd:(h + 1) * hd]
        kh = kv_ref[:, h * hd:(h + 1) * hd]
        vh = kv_ref[:, d + h * hd:d + (h + 1) * hd]
        s = _dot_nt(qh, kh)
        p = jnp.exp(s - jnp.max(s, axis=-1, keepdims=True))
        l = jnp.sum(p, axis=-1, keepdims=True)
        heads.append((_dot(p.astype(BF16), vh) * (1.0 / l)).astype(BF16))
    x2 = x1_ref[...] + _dot(jnp.concatenate(heads, axis=1), wxo_ref[...])
    x2_ref[...] = x2
    logits = _router_logits_t(_rms(x2, g_ref[...]), wrh_ref[...], wrl_ref[...], br_ref[...])
    gl = [logits[g:g + 1, :] for g in range(N_GROUPS)]
    gmax = jnp.maximum(jnp.maximum(gl[0], gl[1]), jnp.maximum(gl[2], gl[3]))
    gidx = jnp.where(gl[0] >= gmax, 0, jnp.where(gl[1] >= gmax, 1, jnp.where(gl[2] >= gmax, 2, 3)))
    rows8 = lax.broadcasted_iota(I32, (8, tm), 0)
    onehot = jnp.where(rows8 == gidx, 1.0, 0.0)
    r = lax.broadcasted_iota(I32, (tm, tm), 0)
    c = lax.broadcasted_iota(I32, (tm, tm), 1)
    incl = _dot(onehot.astype(BF16), jnp.where(r <= c, 1.0, 0.0).astype(BF16))
    before = cnt_scr[:, 0:1]
    rank = jnp.sum(onehot * (incl - 1.0 + before), axis=0, keepdims=True)
    route_ref[...] = jnp.where(rows8 == 0, gidx, jnp.where(rows8 == 1, rank.astype(I32), 0))
    cnt_scr[...] = cnt_scr[...] + incl[:, tm - 1:tm]
    cnt_ref[...] = cnt_scr[...].astype(I32)


def _xattn(xq, kv, x1, wxo, g, wrh, wrl, br, batch, seq, tm):
    n, d = x1.shape
    nb = seq // tm
    mem = kv.shape[1]
    full = lambda a: pl.BlockSpec(a.shape, lambda r: (0,) * a.ndim)
    return pl.pallas_call(
        functools.partial(_xattn_kernel, tm=tm),
        grid=(n // tm,),
        in_specs=[pl.BlockSpec((tm, d), lambda r: (r, 0)),
                  pl.BlockSpec((None, mem, 2 * d), lambda r: (r // nb, 0, 0)),
                  pl.BlockSpec((tm, d), lambda r: (r, 0)),
                  full(wxo), full(g), full(wrh), full(wrl), full(br)],
        out_specs=[pl.BlockSpec((tm, d), lambda r: (r, 0)),
                   pl.BlockSpec((8, tm), lambda r: (0, r)),
                   pl.BlockSpec((8, LANES), lambda r: (0, 0))],
        out_shape=[jax.ShapeDtypeStruct((n, d), F32),
                   jax.ShapeDtypeStruct((8, n), I32),
                   jax.ShapeDtypeStruct((8, LANES), I32)],
        scratch_shapes=[pltpu.VMEM((8, LANES), F32)],
        compiler_params=_cparams("arbitrary"),
        name="xattn",
    )(xq, kv, x1, wxo, g, wrh, wrl, br)


def _plan_kernel(gidx_ref, rank_ref, cnt_ref, tos_ref, tg_ref, off_scr, *, n_tok, tmm):
    n_slots = tos_ref.shape[0]
    n_tiles = tg_ref.shape[0]
    off = 0
    last_group = 0
    ends = []
    for g in range(N_GROUPS):
        off_scr[g] = off
        cnt = cnt_ref[g]
        off = off + ((cnt + (tmm - 1)) // tmm) * tmm
        ends.append(off)
        last_group = jnp.where(cnt > 0, g, last_group)

    def clear(s, _):
        tos_ref[s] = -1
        return 0

    lax.fori_loop(0, n_slots, clear, 0, unroll=8)

    def place(t, _):
        tos_ref[off_scr[gidx_ref[t]] + rank_ref[t]] = t
        return 0

    lax.fori_loop(0, n_tok, place, 0, unroll=8)

    def tile(k, _):
        start = k * tmm
        g = ((start >= ends[0]).astype(I32) + (start >= ends[1]).astype(I32) + (start >= ends[2]).astype(I32))
        tg_ref[k] = jnp.minimum(g, last_group)
        return 0

    lax.fori_loop(0, n_tiles, tile, 0)


def _route_plan(gidx, rank, cnt, n_tok, tmm):
    n_tiles = n_tok // tmm + N_GROUPS
    smem = pl.BlockSpec(memory_space=pltpu.SMEM)
    return pl.pallas_call(
        functools.partial(_plan_kernel, n_tok=n_tok, tmm=tmm),
        in_specs=[smem, smem, smem],
        out_specs=[smem, smem],
        out_shape=[jax.ShapeDtypeStruct((n_tiles * tmm,), I32), jax.ShapeDtypeStruct((n_tiles,), I32)],
        scratch_shapes=[pltpu.SMEM((N_GROUPS,), I32)],
        name="route_plan",
    )(gidx, rank, cnt)


def _moe_kernel(tos_ref, tg_ref, x2_hbm, g_ref, wrh_ref, wrl_ref, br_ref, wg_ref, wu_ref, wd_ref, fg_ref,
                x3_hbm, xb, ob, gsem, ssem, *, tmm, n_tok, final):
    k = pl.program_id(0)
    nk = pl.num_programs(0)
    slot = k % 2

    def gather_copy(tile, sl, r):
        tok = jnp.maximum(tos_ref[tile * tmm + r], 0)
        return pltpu.make_async_copy(x2_hbm.at[pl.ds(tok, 1)], xb.at[sl, pl.ds(r, 1)], gsem.at[sl])

    def scatter_copy(tile, sl, r):
        tok = tos_ref[tile * tmm + r]
        dst = jnp.where(tok >= 0, tok, n_tok + sl * tmm + r)
        return pltpu.make_async_copy(ob.at[sl, pl.ds(r, 1)], x3_hbm.at[pl.ds(dst, 1)], ssem.at[sl])

    def for_rows(fn):
        def body(r, _):
            fn(r)
            return 0
        lax.fori_loop(0, tmm, body, 0, unroll=8)

    @pl.when(k == 0)
    def _():
        for_rows(lambda r: gather_copy(0, 0, r).start())

    for_rows(lambda r: gather_copy(k, slot, r).wait())

    @pl.when(k + 1 < nk)
    def _():
        for_rows(lambda r: gather_copy(k + 1, 1 - slot, r).start())

    @pl.when(k >= 2)
    def _():
        for_rows(lambda r: scatter_copy(k - 2, slot, r).wait())

    x = xb[slot]
    t = _rms(x, g_ref[...])
    tb = t.astype(BF16)
    th, tl = _split2(t)
    logits = _dot(th, wrh_ref[...]) + _dot(th, wrl_ref[...]) + _dot(tl, wrh_ref[...]) + br_ref[...]
    grp = tg_ref[k]
    lane = lax.broadcasted_iota(I32, (1, LANES), 1)
    lanef = lane.astype(F32)
    rmax = lambda a: jnp.max(a, axis=-1, keepdims=True)
    rsum = lambda a: jnp.sum(a, axis=-1, keepdims=True)
    gmask = lane < N_GROUPS
    gmax = rmax(jnp.where(gmask, logits, -jnp.inf))
    zg = rsum(jnp.where(gmask, jnp.exp(logits - gmax), 0.0))
    p_group = jnp.exp(rsum(jnp.where(lane == grp, logits, 0.0)) - gmax) / zg
    e_lo = N_GROUPS + EXPERTS_PER_GROUP * grp
    emask = (lane >= e_lo) & (lane < e_lo + EXPERTS_PER_GROUP)
    em = jnp.where(emask, logits, -jnp.inf)
    e1 = rmax(em)
    i1 = jnp.min(jnp.where(em == e1, lanef, 1e9), axis=-1, keepdims=True)
    em2 = jnp.where(lanef == i1, -jnp.inf, em)
    e2 = rmax(em2)
    i2 = jnp.min(jnp.where(em2 == e2, lanef, 1e9), axis=-1, keepdims=True)
    ze = rsum(jnp.where(emask, jnp.exp(logits - e1), 0.0))
    p1 = 1.0 / ze
    p2 = jnp.exp(e2 - e1) / ze
    w1 = p_group * (p1 / (p1 + p2))
    w2 = p_group * (p2 / (p1 + p2))
    hid = _dot(tb, wg_ref[...])
    hid = hid * _sigmoid(hid) * _dot(tb, wu_ref[...])
    ff = hid.shape[1] // EXPERTS_PER_GROUP
    cw_lanes = jnp.where(lanef == i1, w1, 0.0) + jnp.where(lanef == i2, w2, 0.0)
    parts = []
    for e in range(EXPERTS_PER_GROUP):
        cw = rsum(jnp.where(lane == e_lo + e, cw_lanes, 0.0))
        parts.append((hid[:, e * ff:(e + 1) * ff] * cw).astype(BF16))
    y = x + _dot(jnp.concatenate(parts, axis=1), wd_ref[...])
    if final:
        y = _rms(y, fg_ref[...])
    ob[slot] = y

    for_rows(lambda r: scatter_copy(k, slot, r).start())

    @pl.when(k == nk - 1)
    def _():
        @pl.when(k >= 1)
        def _():
            for_rows(lambda r: scatter_copy(k - 1, 1 - slot, r).wait())
        for_rows(lambda r: scatter_copy(k, slot, r).wait())


def _moe(tos, tg, x2, g, wrh_t, wrl_t, br_row, wg, wu, wd, fg, tmm, final):
    n_tok, d = x2.shape
    n_tiles = tg.shape[0]
    full = lambda a: pl.BlockSpec(a.shape, lambda k, tos, tg: (0,) * a.ndim)
    grp = lambda a: pl.BlockSpec((None,) + a.shape[1:], lambda k, tos, tg: (tg[k], 0, 0))
    grid_spec = pltpu.PrefetchScalarGridSpec(
        num_scalar_prefetch=2,
        grid=(n_tiles,),
        in_specs=[pl.BlockSpec(memory_space=pl.ANY), full(g), full(wrh_t), full(wrl_t), full(br_row),
                  grp(wg), grp(wu), grp(wd), full(fg)],
        out_specs=pl.BlockSpec(memory_space=pl.ANY),
        scratch_shapes=[pltpu.VMEM((2, tmm, d), F32), pltpu.VMEM((2, tmm, d), F32),
                        pltpu.SemaphoreType.DMA((2,)), pltpu.SemaphoreType.DMA((2,))],
    )
    return pl.pallas_call(
        functools.partial(_moe_kernel, tmm=tmm, n_tok=n_tok, final=final),
        grid_spec=grid_spec,
        out_shape=jax.ShapeDtypeStruct((n_tok + 2 * tmm, d), F32),
        compiler_params=_cparams("arbitrary"),
        name="moe",
    )(tos, tg, x2, g, wrh_t, wrl_t, br_row, wg, wu, wd, fg)


def _row(v):
    return v.reshape(1, -1)


def kernel(x, mem, mix_norm_g, w_in, fox_f_bias, fox_norm_g, hgrn_lb_logits, hgrn_norm_g, conv_w, conv_b,
           conv_norm_g, conv_norm_b, w_out, xatt_norm_g, mem_norm_g, w_xq, w_xkv, w_xo, ffn_norm_g,
           w_group, b_group, w_router, b_router, w_gate, w_up, w_down, final_norm_g):
    batch, seq, d = x.shape
    depth = w_in.shape[0]
    n = batch * seq
    tm = min(512, seq)
    tmm = min(512, seq)
    fw = FOX_HEADS * FOX_HEAD_DIM
    hw = HGRN_HEADS * HGRN_DIM
    cw = conv_w.shape[-1]
    assert seq % tm == 0 and seq % HGRN_BLOCK == 0 and d == fw + hw + cw

    offs = [0]
    for width in (fw, fw, fw, FOX_HEADS, hw, hw, hw, hw, cw, cw):
        offs.append(offs[-1] + width)
    seg = lambda a, i, j: a[..., offs[i]:offs[j]]
    wqkv = jnp.concatenate([seg(w_in, 0, 1) * (FOX_HEAD_DIM ** -0.5), seg(w_in, 1, 3)], axis=-1).astype(BF16)
    wff = jnp.swapaxes(seg(w_in, 3, 4), 1, 2).astype(BF16)
    wh = jnp.concatenate([seg(w_in, 4, 5), seg(w_in, 5, 6), seg(w_in, 6, 7), seg(w_in, 7, 8)], axis=-1).astype(BF16)
    wc = seg(w_in, 8, 10).astype(BF16)
    wo = w_out.astype(BF16)
    wxq = (w_xq * ((d // XATT_HEADS) ** -0.5)).astype(BF16)
    wxkv = w_xkv.astype(BF16)
    wxo = w_xo.astype(BF16)
    n_exp = N_GROUPS * EXPERTS_PER_GROUP
    wr = jnp.concatenate([w_group, w_router], axis=-1)
    wr_hi = wr.astype(BF16)
    wr_lo = (wr - wr_hi.astype(F32)).astype(BF16)
    pad_rows = lambda a: jnp.pad(jnp.swapaxes(a, 1, 2), ((0, 0), (0, ROUTER_ROWS - a.shape[2]), (0, 0)))
    pad_cols = lambda a: jnp.pad(a, ((0, 0), (0, 0), (0, LANES - a.shape[2])))
    br = jnp.concatenate([b_group, b_router], axis=-1)
    br_col = jnp.pad(br, ((0, 0), (0, ROUTER_ROWS - br.shape[1])))[:, :, None]
    br_row = jnp.pad(br, ((0, 0), (0, LANES - br.shape[1])))[:, None, :]
    ff = w_gate.shape[-1]
    to_cols = lambda w: jnp.transpose(w, (0, 1, 3, 2, 4)).reshape(depth, N_GROUPS, d, EXPERTS_PER_GROUP * ff).astype(BF16)
    wg, wu = to_cols(w_gate), to_cols(w_up)
    wd = w_down.reshape(depth, N_GROUPS, EXPERTS_PER_GROUP * ff, d).astype(BF16)
    conv_w_pad = jnp.pad(conv_w, ((0, 0), (0, CONV_HALO - CONV_WIDTH), (0, 0)))
    wrh_rows, wrl_rows, wrh_cols, wrl_cols = pad_rows(wr_hi), pad_rows(wr_lo), pad_cols(wr_hi), pad_cols(wr_lo)

    kv_all = _mem_kv(mem.reshape(-1, d), _row(mem_norm_g), wxkv).reshape(depth, batch, mem.shape[1], 2 * d)

    xs = x.reshape(n, d)
    for l in range(depth):
        qkv, hy, cy, lf, qn, kn = _mix_in(xs, _row(mix_norm_g[l]), wqkv[l], wh[l], wc[l], wff[l],
                                          fox_f_bias[l].reshape(-1, 1), batch, seq, tm)
        c, keep = _fox_plan(lf, qn, kn, tm)
        fo = _fox_attention(keep.reshape(-1), qkv, c, _row(fox_norm_g[l]), batch, seq, tm)
        ho = _hgrn(hy, hgrn_lb_logits, _row(hgrn_norm_g[l]), l, batch, seq, tm)
        co = _conv(cy, conv_w_pad[l], _row(conv_b[l]), _row(conv_norm_g[l]), _row(conv_norm_b[l]), batch, seq, tm)
        x1, xq = _mix_out(fo, ho, co, xs, wo[l], _row(xatt_norm_g[l]), wxq[l], tm)
        x2, route, cnt = _xattn(xq, kv_all[l], x1, wxo[l], _row(ffn_norm_g[l]),
                                wrh_rows[l], wrl_rows[l], br_col[l], batch, seq, tm)
        tos, tg = _route_plan(route[0], route[1], cnt[:N_GROUPS, 0], n, tmm)
        xs = _moe(tos, tg, x2, _row(ffn_norm_g[l]), wrh_cols[l], wrl_cols[l], br_row[l],
                  wg[l], wu[l], wd[l], _row(final_norm_g), tmm, final=(l == depth - 1))
    return xs[:n].reshape(batch, seq, d)
```

```python
import functools

import jax
import jax.numpy as jnp
from jax import lax
from jax.experimental import pallas as pl
from jax.experimental.pallas import tpu as pltpu

F32 = jnp.float32
BF16 = jnp.bfloat16
I32 = jnp.int32

EPS = 1e-6
LANES = 128
FOX_HEADS = 8
FOX_HEAD_DIM = 64
LOG2E = 1.4426950408889634
FOX_BLOCK = 512
FOX_SKIP_EXPONENT = -45.0
FOX_NORM_SLACK = 1.02
HGRN_HEADS = 4
HGRN_DIM = 64
HGRN_CHUNK = 16
HGRN_BLOCK = 128
CONV_WIDTH = 31
CONV_HALO = 32
CONV_GROUP = 64
XATT_HEADS = 4
N_GROUPS = 4
EXPERTS_PER_GROUP = 4
ROUTER_ROWS = 32
DMA_UNROLL = 8
VMEM_LIMIT_BYTES = 56 * 1024 * 1024

NT_DIMS = (((1,), (1,)), ((), ()))


def _cparams(*sem):
    return pltpu.CompilerParams(dimension_semantics=sem, vmem_limit_bytes=VMEM_LIMIT_BYTES)


def _dot(a, b):
    return jnp.dot(a, b, preferred_element_type=F32)


def _dot_nt(a, b):
    return lax.dot_general(a, b, NT_DIMS, preferred_element_type=F32)


def _rms(x, g):
    ms = jnp.mean(x * x, axis=-1, keepdims=True)
    return x * lax.rsqrt(ms + EPS) * g


def _sigmoid(x):
    return 1.0 / (1.0 + jnp.exp(-x))


def _split2(x):
    hi = x.astype(BF16)
    lo = (x - hi.astype(F32)).astype(BF16)
    return hi, lo


def _split3(x):
    hi = x.astype(BF16)
    r = x - hi.astype(F32)
    mid = r.astype(BF16)
    lo = (r - mid.astype(F32)).astype(BF16)
    return hi, mid, lo


def _halves_select(lane_lo_mask, a, b):
    return jnp.where(lane_lo_mask, a, b)


def _load_rows(ref, rows):
    if ref.shape[0] == rows:
        return ref[...]
    sub = ref.shape[0] // rows
    return jnp.concatenate([ref[pl.ds(c, rows, stride=sub), :] for c in range(sub)], axis=1)


def _store_rows(ref, val):
    rows = val.shape[0]
    if ref.shape[0] == rows:
        ref[...] = val
    else:
        sub = ref.shape[0] // rows
        for c in range(sub):
            ref[pl.ds(c, rows, stride=sub), :] = val[:, c * LANES:(c + 1) * LANES]


def _pair_rms(o, lo_mask, width):
    o2 = o * o
    s_all = jnp.sum(o2, axis=-1, keepdims=True)
    s_lo = jnp.sum(jnp.where(lo_mask, o2, 0.0), axis=-1, keepdims=True)
    ms = jnp.where(lo_mask, s_lo, s_all - s_lo) * (1.0 / width)
    return o * lax.rsqrt(ms + EPS)


def _mem_kv_kernel(mem_ref, g_ref, w_ref, o_ref):
    t = _rms(mem_ref[...], g_ref[...]).astype(BF16)
    o_ref[...] = _dot(t, w_ref[...]).astype(BF16)


def _mem_kv(mem2d, g, w_xkv):
    depth, d, d2 = w_xkv.shape
    rows = mem2d.shape[0]
    return pl.pallas_call(
        _mem_kv_kernel,
        grid=(depth,),
        in_specs=[pl.BlockSpec((rows, d), lambda l: (0, 0)),
                  pl.BlockSpec((1, d), lambda l: (0, 0)),
                  pl.BlockSpec((None, d, d2), lambda l: (l, 0, 0))],
        out_specs=pl.BlockSpec((None, rows, d2), lambda l: (l, 0, 0)),
        out_shape=jax.ShapeDtypeStruct((depth, rows, d2), BF16),
        compiler_params=_cparams("arbitrary"),
        name="mem_kv",
    )(mem2d, g, w_xkv)


def _head_norm_max(y, first_col):
    lane = lax.broadcasted_iota(I32, (1, LANES), 1)
    lo_mask = lane < FOX_HEAD_DIM
    head_row = lax.broadcasted_iota(I32, (FOX_HEADS, LANES), 0)
    out = jnp.zeros((FOX_HEADS, LANES), F32)
    for p in range(FOX_HEADS // 2):
        blk = y[:, first_col + p * LANES:first_col + (p + 1) * LANES]
        sq = blk * blk
        for h, m in ((2 * p, lo_mask), (2 * p + 1, jnp.logical_not(lo_mask))):
            n2 = jnp.sum(jnp.where(m, sq, 0.0), axis=-1, keepdims=True)
            out = jnp.where(head_row == h, jnp.sqrt(jnp.max(n2, axis=0, keepdims=True)), out)
    return out


def _mix_in_kernel(x_ref, g_ref, wqkv_ref, wh_ref, wc_ref, wff_ref, fb_ref,
                   qkv_ref, hy_ref, cy_ref, lf_ref, qn_ref, kn_ref, kn_scr, *, nb):
    t = _rms(_load_rows(x_ref, qkv_ref.shape[0]), g_ref[...]).astype(BF16)
    y = _dot(t, wqkv_ref[...])
    qkv_ref[...] = y.astype(BF16)
    hy_ref[...] = _dot(t, wh_ref[...]).astype(BF16)
    cy_ref[...] = _dot(t, wc_ref[...]).astype(BF16)
    z = _dot_nt(wff_ref[...], t) + fb_ref[...]
    lf_ref[...] = (jnp.minimum(z, 0.0) - jnp.log1p(jnp.exp(-jnp.abs(z)))) * LOG2E
    width = FOX_HEADS * FOX_HEAD_DIM
    qn_ref[...] = _head_norm_max(y, 0)

    @pl.when(pl.program_id(0) % nb == 0)
    def _():
        kn_scr[...] = jnp.zeros_like(kn_scr)

    kn_scr[...] = jnp.maximum(kn_scr[...], _head_norm_max(y, width))
    kn_ref[...] = kn_scr[...]


def _row_block(a, tm, d):
    sub = 1 if a.shape[1] == d else d // a.shape[1]
    return pl.BlockSpec((tm * sub, a.shape[1]), lambda r: (r, 0))


def _mix_in(x2d, g, wqkv, wh, wc, wff, fb, batch, seq, tm):
    n = batch * seq
    nb = seq // tm
    full = lambda a: pl.BlockSpec(a.shape, lambda r: (0,) * a.ndim)
    return pl.pallas_call(
        functools.partial(_mix_in_kernel, nb=nb),
        grid=(n // tm,),
        in_specs=[_row_block(x2d, tm, wqkv.shape[0]), full(g), full(wqkv), full(wh), full(wc),
                  full(wff), full(fb)],
        out_specs=[pl.BlockSpec((tm, wqkv.shape[1]), lambda r: (r, 0)),
                   pl.BlockSpec((tm, wh.shape[1]), lambda r: (r, 0)),
                   pl.BlockSpec((tm, wc.shape[1]), lambda r: (r, 0)),
                   pl.BlockSpec((None, FOX_HEADS, tm), lambda r: (r // nb, 0, r % nb)),
                   pl.BlockSpec((None, FOX_HEADS, LANES), lambda r: (r, 0, 0)),
                   pl.BlockSpec((None, FOX_HEADS, LANES), lambda r: (r // nb, 0, 0))],
        out_shape=[jax.ShapeDtypeStruct((n, wqkv.shape[1]), BF16),
                   jax.ShapeDtypeStruct((n, wh.shape[1]), BF16),
                   jax.ShapeDtypeStruct((n, wc.shape[1]), BF16),
                   jax.ShapeDtypeStruct((batch, FOX_HEADS, seq), F32),
                   jax.ShapeDtypeStruct((n // tm, FOX_HEADS, LANES), F32),
                   jax.ShapeDtypeStruct((batch, FOX_HEADS, LANES), F32)],
        scratch_shapes=[pltpu.VMEM((FOX_HEADS, LANES), F32)],
        compiler_params=_cparams("arbitrary"),
        name="mix_in",
    )(x2d, g, wqkv, wh, wc, wff, fb)


def _fox_plan_kernel(lf_ref, qn_ref, kn_ref, c_ref, keep_ref, *, blk, stat_rows):
    seq = lf_ref.shape[-1]
    nblk = seq // blk
    r = lax.broadcasted_iota(I32, (blk, blk), 0)
    c = lax.broadcasted_iota(I32, (blk, blk), 1)
    upper = jnp.where(r <= c, 1.0, 0.0).astype(BF16)
    lane = lax.broadcasted_iota(I32, (FOX_HEADS, LANES), 1)
    carry = jnp.zeros((FOX_HEADS, 1), F32)
    first = jnp.zeros((FOX_HEADS, LANES), F32)
    last = jnp.zeros((FOX_HEADS, LANES), F32)
    qn = jnp.zeros((FOX_HEADS, LANES), F32)
    for b in range(nblk):
        x = lf_ref[:, b * blk:(b + 1) * blk]
        hi, mid, lo = _split3(x)
        cb = _dot(hi, upper) + _dot(mid, upper) + _dot(lo, upper) + carry
        for h in range(FOX_HEADS):
            c_ref[h, :, b * blk:(b + 1) * blk] = cb[h:h + 1, :]
        carry = cb[:, blk - 1:blk]
        first = jnp.where(lane == b, cb[:, 0:1], first)
        last = jnp.where(lane == b, carry, last)
        qn = jnp.where(lane == b, qn_ref[b * blk // stat_rows], qn)
    bound = FOX_NORM_SLACK * 2.0 * qn * kn_ref[...] + first
    keep = jnp.zeros((FOX_HEADS, LANES), I32)
    for i in range(nblk):
        live = (bound[:, i:i + 1] - last >= FOX_SKIP_EXPONENT * LOG2E) & (lane < i)
        keep = jnp.where(lane == i, jnp.sum(live.astype(I32), axis=1, keepdims=True), keep)
    keep_ref[...] = keep


def _fox_plan(lf, qn, kn, blk, stat_rows):
    batch, heads, seq = lf.shape
    assert seq // blk <= LANES
    return pl.pallas_call(
        functools.partial(_fox_plan_kernel, blk=blk, stat_rows=stat_rows),
        grid=(batch,),
        in_specs=[pl.BlockSpec((None, heads, seq), lambda b: (b, 0, 0)),
                  pl.BlockSpec((seq // stat_rows, heads, LANES), lambda b: (b, 0, 0)),
                  pl.BlockSpec((None, heads, LANES), lambda b: (b, 0, 0))],
        out_specs=[pl.BlockSpec((None, heads, 1, seq), lambda b: (b, 0, 0, 0)),
                   pl.BlockSpec((None, heads, LANES), lambda b: (b, 0, 0))],
        out_shape=[jax.ShapeDtypeStruct((batch, heads, 1, seq), F32),
                   jax.ShapeDtypeStruct((batch, heads, LANES), I32)],
        compiler_params=_cparams("arbitrary"),
        name="fox_plan",
    )(lf, qn, kn)


def _fox_kernel(keep_ref, q_ref, k_ref, v_ref, c_ref, g_ref, o_ref, *, tq, qpb):
    head0 = (pl.program_id(0) * FOX_HEADS + 2 * pl.program_id(1)) * LANES
    lane = lax.broadcasted_iota(I32, (1, LANES), 1)
    lo_mask = lane < FOX_HEAD_DIM
    row = lax.broadcasted_iota(I32, (tq, tq), 0)
    col = lax.broadcasted_iota(I32, (tq, tq), 1)
    causal = row >= col

    for qb in range(qpb):
        i = pl.program_id(2) * qpb + qb
        n_prev = jnp.maximum(keep_ref[head0 + i], keep_ref[head0 + LANES + i])
        q = q_ref[qb * tq:(qb + 1) * tq, :]
        zero = jnp.zeros_like(q)
        qh = (jnp.where(lo_mask, q, zero), jnp.where(lo_mask, zero, q))
        q0 = pl.multiple_of(i * tq, tq)
        cq0 = tuple(c_ref[h, :, pl.ds(q0, LANES)][:, 0:1] for h in range(2))

        def block(j, carry, masked, qh=qh, cq0=cq0):
            k0 = pl.multiple_of(j * tq, tq)
            kb = k_ref[pl.ds(k0, tq), :]
            vb = v_ref[pl.ds(k0, tq), :]
            s = [_dot_nt(qh[h], kb) + (cq0[h] - c_ref[h, :, pl.ds(k0, tq)]) for h in range(2)]
            out = []
            for h in range(2):
                m, l, acc = carry[h]
                sh = jnp.where(causal, s[h], -jnp.inf) if masked else s[h]
                m_new = jnp.maximum(m, jnp.max(sh, axis=-1, keepdims=True))
                alpha = jnp.exp2(m - m_new)
                p = jnp.exp2(sh - m_new)
                l = alpha * l + jnp.sum(p, axis=-1, keepdims=True)
                acc = alpha * acc + _dot(p.astype(BF16), vb)
                out.append((m_new, l, acc))
            return tuple(out)

        init = tuple((jnp.full((tq, 1), -jnp.inf, F32), jnp.zeros((tq, 1), F32), jnp.zeros((tq, LANES), F32))
                     for _ in range(2))
        carry = block(i, init, True)
        carry = lax.fori_loop(i - n_prev, i, lambda j, c, block=block: block(j, c, False), carry)
        (_, l0, a0), (_, l1, a1) = carry
        o = jnp.where(lo_mask, a0 * (1.0 / l0), a1 * (1.0 / l1))
        o_ref[qb * tq:(qb + 1) * tq, :] = (_pair_rms(o, lo_mask, FOX_HEAD_DIM) * g_ref[...]).astype(o_ref.dtype)


def _fox_attention(keep, qkv, c, gain, batch, seq, tq, qpb):
    n = qkv.shape[0]
    pairs = FOX_HEADS // 2
    rows = tq * qpb
    nq = seq // rows
    grid_spec = pltpu.PrefetchScalarGridSpec(
        num_scalar_prefetch=1,
        grid=(batch, pairs, nq),
        in_specs=[pl.BlockSpec((rows, LANES), lambda b, p, i, keep: (b * nq + i, p)),
                  pl.BlockSpec((seq, LANES), lambda b, p, i, keep: (b, pairs + p)),
                  pl.BlockSpec((seq, LANES), lambda b, p, i, keep: (b, 2 * pairs + p)),
                  pl.BlockSpec((None, 2, 1, seq), lambda b, p, i, keep: (b, p, 0, 0)),
                  pl.BlockSpec((1, LANES), lambda b, p, i, keep: (0, p))],
        out_specs=pl.BlockSpec((rows, LANES), lambda b, p, i, keep: (b * nq + i, p)),
    )
    return pl.pallas_call(
        functools.partial(_fox_kernel, tq=tq, qpb=qpb),
        grid_spec=grid_spec,
        out_shape=jax.ShapeDtypeStruct((n, pairs * LANES), BF16),
        compiler_params=_cparams("parallel", "parallel", "arbitrary"),
        name="fox_attention",
    )(keep, qkv, qkv, qkv, c, gain)


def _hgrn_kernel(q_ref, f_ref, v_ref, gate_ref, lbz_ref, gain_ref, o_ref, st_ref, u_scr, prev_scr, *, layer, n_sub):
    T, C = HGRN_BLOCK, HGRN_CHUNK
    nchunk = T // C
    i = pl.program_id(2)

    @pl.when(i == 0)
    def _():
        st_ref[...] = jnp.zeros_like(st_ref)

    z = lbz_ref[...]
    e = jnp.exp(z - jnp.max(z, axis=0, keepdims=True))
    pz = e / jnp.sum(e, axis=0, keepdims=True)
    lb = jnp.zeros((1, LANES), F32)
    for j in range(1, layer + 1):
        lb = lb + pz[j:j + 1, :]

    lane = lax.broadcasted_iota(I32, (1, LANES), 1)
    lo_mask = lane < HGRN_DIM
    r = lax.broadcasted_iota(I32, (T, T), 0)
    c = lax.broadcasted_iota(I32, (T, T), 1)
    same_chunk = (r // C) == (c // C)
    one = lambda m: jnp.where(m, 1.0, 0.0).astype(BF16)
    scan_mat = jnp.concatenate([one(same_chunk & (c <= r)),
                                one(same_chunk & ((c % C) <= C // 2)),
                                one(same_chunk)], axis=0)
    intra_mask = same_chunk & (c <= r)
    chunk_of_row = lax.broadcasted_iota(I32, (T, LANES), 0) // C
    vr = lax.broadcasted_iota(I32, (LANES, nchunk * LANES), 0)
    kc = lax.broadcasted_iota(I32, (LANES, nchunk * LANES), 1)
    same_head = (vr < HGRN_DIM) == ((kc % LANES) < HGRN_DIM)

    subs = range(n_sub)
    rows = [pl.ds(sb * T, T) for sb in subs]
    q = [q_ref[r_, :].astype(F32) for r_ in rows]
    f = [lb + (1.0 - lb) * _sigmoid(f_ref[r_, :].astype(F32)) for r_ in rows]
    kk = [1.0 - f_ for f_ in f]
    parts = []
    for f_ in f:
        parts.extend(_split2(jnp.log(f_)))
    sc = _dot(scan_mat, jnp.concatenate(parts, axis=1))
    sc = [sc[:, (2 * sb) * LANES:(2 * sb + 1) * LANES] + sc[:, (2 * sb + 1) * LANES:(2 * sb + 2) * LANES]
          for sb in subs]
    b = [s_[:T] for s_ in sc]
    b_mid = [s_[T:2 * T] for s_ in sc]
    b_last = [s_[2 * T:] for s_ in sc]
    v = [v_ref[r_, :] for r_ in rows]
    for sb in subs:
        k_out = kk[sb] * jnp.exp(b_last[sb] - b[sb])
        k_exp = jnp.concatenate([jnp.where(chunk_of_row == j, k_out, 0.0) for j in range(nchunk)],
                                axis=1).astype(BF16)
        v_t = v[sb].astype(F32).T.astype(BF16)
        u_scr[sb] = jnp.where(same_head, _dot(v_t, k_exp), 0.0)
    att = []
    for sb in subs:
        q_in = (q[sb] * jnp.exp(b[sb] - b_mid[sb])).astype(BF16)
        k_in = (kk[sb] * jnp.exp(b_mid[sb] - b[sb])).astype(BF16)
        zq = jnp.zeros_like(q_in)
        att.append([jnp.where(intra_mask, _dot_nt(qm, k_in), 0.0).astype(BF16)
                    for qm in (jnp.where(lo_mask, q_in, zq), jnp.where(lo_mask, zq, q_in))])
    o = [jnp.where(lo_mask, _dot(att[sb][0], v[sb]), _dot(att[sb][1], v[sb])) for sb in subs]
    state = st_ref[...]
    for sb in subs:
        decay = jnp.exp(b_last[sb])
        for j in range(nchunk):
            prev_scr[sb, j * LANES:(j + 1) * LANES, :] = state.astype(BF16)
            state = state * decay[j * C:j * C + 1, :] + u_scr[sb, :, j * LANES:(j + 1) * LANES]
    st_ref[...] = state
    for sb in subs:
        q_out = (q[sb] * jnp.exp(b[sb])).astype(BF16)
        o_all = _dot_nt(q_out, prev_scr[sb])
        acc = o[sb]
        for j in range(nchunk):
            acc = acc + jnp.where(chunk_of_row == j, o_all[:, j * LANES:(j + 1) * LANES], 0.0)
        gate = gate_ref[rows[sb], :].astype(F32)
        y = _pair_rms(acc, lo_mask, HGRN_DIM) * gain_ref[...] * (gate * _sigmoid(gate))
        o_ref[rows[sb], :] = y.astype(o_ref.dtype)


def _hgrn(hy, lb_logits, gain, layer, batch, seq, tg):
    n = hy.shape[0]
    pairs = HGRN_HEADS // 2
    nb = seq // tg
    depth = lb_logits.shape[0]
    col = lambda k: pl.BlockSpec((tg, LANES), lambda b, p, i, k=k: (b * nb + i, k * pairs + p))
    n_sub = tg // HGRN_BLOCK
    states = (HGRN_BLOCK // HGRN_CHUNK) * LANES
    return pl.pallas_call(
        functools.partial(_hgrn_kernel, layer=layer, n_sub=n_sub),
        grid=(batch, pairs, nb),
        in_specs=[col(0), col(1), col(2), col(3),
                  pl.BlockSpec((depth, LANES), lambda b, p, i: (0, p)),
                  pl.BlockSpec((1, LANES), lambda b, p, i: (0, p))],
        out_specs=pl.BlockSpec((tg, LANES), lambda b, p, i: (b * nb + i, p)),
        out_shape=jax.ShapeDtypeStruct((n, pairs * LANES), BF16),
        scratch_shapes=[pltpu.VMEM((LANES, LANES), F32),
                        pltpu.VMEM((n_sub, LANES, states), F32),
                        pltpu.VMEM((n_sub, states, LANES), BF16)],
        compiler_params=_cparams("parallel", "parallel", "arbitrary"),
        name="hgrn",
    )(hy, hy, hy, hy, lb_logits, gain)


def _conv_kernel(cu_ref, cg_ref, w_ref, b_ref, ng_ref, nb_ref, o_ref, a_scr, *, tm):
    i = pl.program_id(1)

    @pl.when(i == 0)
    def _():
        a_scr[0:CONV_HALO, :] = jnp.zeros((CONV_HALO, a_scr.shape[1]), F32)

    @pl.when(i > 0)
    def _():
        a_scr[0:CONV_HALO, :] = a_scr[tm:tm + CONV_HALO, :]

    a_scr[CONV_HALO:CONV_HALO + tm, :] = cu_ref[...].astype(F32) * _sigmoid(cg_ref[...].astype(F32))
    acc = jnp.zeros((tm, a_scr.shape[1]), F32) + b_ref[...]
    first = CONV_HALO - (CONV_WIDTH - 1)
    for w in range(CONV_WIDTH):
        acc = acc + a_scr[first + w:first + w + tm, :] * w_ref[w:w + 1, :]
    lane = lax.broadcasted_iota(I32, (1, LANES), 1)
    lo_mask = lane < CONV_GROUP
    halves = []
    for hh in range(acc.shape[1] // LANES):
        xh = acc[:, hh * LANES:(hh + 1) * LANES]
        s_all = jnp.sum(xh, axis=-1, keepdims=True)
        s_lo = jnp.sum(jnp.where(lo_mask, xh, 0.0), axis=-1, keepdims=True)
        d = xh - jnp.where(lo_mask, s_lo, s_all - s_lo) * (1.0 / CONV_GROUP)
        halves.append(_pair_rms(d, lo_mask, CONV_GROUP))
    y = jnp.concatenate(halves, axis=1) * ng_ref[...] + nb_ref[...]
    o_ref[...] = (y * _sigmoid(y)).astype(o_ref.dtype)


def _conv(cy, w, b, ng, nb_, batch, seq, tm):
    n = cy.shape[0]
    ch = cy.shape[1] // 2
    nb = seq // tm
    full = lambda a: pl.BlockSpec(a.shape, lambda bb, i: (0,) * a.ndim)
    return pl.pallas_call(
        functools.partial(_conv_kernel, tm=tm),
        grid=(batch, nb),
        in_specs=[pl.BlockSpec((tm, ch), lambda bb, i: (bb * nb + i, 0)),
                  pl.BlockSpec((tm, ch), lambda bb, i: (bb * nb + i, 1)),
                  full(w), full(b), full(ng), full(nb_)],
        out_specs=pl.BlockSpec((tm, ch), lambda bb, i: (bb * nb + i, 0)),
        out_shape=jax.ShapeDtypeStruct((n, ch), BF16),
        scratch_shapes=[pltpu.VMEM((CONV_HALO + tm, ch), F32)],
        compiler_params=_cparams("parallel", "arbitrary"),
        name="conv",
    )(cy, cy, w, b, ng, nb_)


def _mix_out_kernel(fo_ref, ho_ref, co_ref, x_ref, wo_ref, g_ref, wq_ref, x1_ref, xq_ref):
    nf, nh = fo_ref.shape[1], ho_ref.shape[1]
    y = (_dot(fo_ref[...], wo_ref[0:nf, :]) + _dot(ho_ref[...], wo_ref[nf:nf + nh, :])
         + _dot(co_ref[...], wo_ref[nf + nh:, :]))
    x1 = _load_rows(x_ref, y.shape[0]) + y
    x1_ref[...] = x1
    xq_ref[...] = _dot(_rms(x1, g_ref[...]).astype(BF16), wq_ref[...]).astype(BF16)


def _mix_out(fo, ho, co, x2d, wo, g, wq, tm):
    n, d = fo.shape[0], wo.shape[1]
    full = lambda a: pl.BlockSpec(a.shape, lambda r: (0,) * a.ndim)
    rowblk = lambda a: _row_block(a, tm, d if a is x2d else a.shape[1])
    return pl.pallas_call(
        _mix_out_kernel,
        grid=(n // tm,),
        in_specs=[rowblk(fo), rowblk(ho), rowblk(co), rowblk(x2d), full(wo), full(g), full(wq)],
        out_specs=[pl.BlockSpec((tm, d), lambda r: (r, 0)), pl.BlockSpec((tm, d), lambda r: (r, 0))],
        out_shape=[jax.ShapeDtypeStruct((n, d), F32), jax.ShapeDtypeStruct((n, d), BF16)],
        compiler_params=_cparams("arbitrary"),
        name="mix_out",
    )(fo, ho, co, x2d, wo, g, wq)


def _router_logits_t(t, wr_hi, wr_lo, br):
    th, tl = _split2(t)
    return _dot_nt(wr_hi, th) + _dot_nt(wr_lo, th) + _dot_nt(wr_hi, tl) + br


def _xattn_kernel(xq_ref, kv_ref, x1_ref, wxo_ref, g_ref, wrh_ref, wrl_ref, br_ref,
                  x2_ref, route_ref, cnt_ref, cnt_scr, *, tm):
    step = pl.program_id(0)

    @pl.when(step == 0)
    def _():
        cnt_scr[...] = jnp.zeros_like(cnt_scr)

    d = xq_ref.shape[1]
    hd = d // XATT_HEADS
    heads = []
    for h in range(XATT_HEADS):
        qh = xq_ref[:, h * hd:(h + 1) * hd]
        kh = kv_ref[:, h * hd:(h + 1) * hd]
        vh = kv_ref[:, d + h * hd:d + (h + 1) * hd]
        s = _dot_nt(qh, kh)
        p = jnp.exp(s - jnp.max(s, axis=-1, keepdims=True))
        l = jnp.sum(p, axis=-1, keepdims=True)
        heads.append((_dot(p.astype(BF16), vh) * (1.0 / l)).astype(BF16))
    x2 = x1_ref[...] + _dot(jnp.concatenate(heads, axis=1), wxo_ref[...])
    _store_rows(x2_ref, x2)
    logits = _router_logits_t(_rms(x2, g_ref[...]), wrh_ref[...], wrl_ref[...], br_ref[...])
    gl = [logits[g:g + 1, :] for g in range(N_GROUPS)]
    gmax = jnp.maximum(jnp.maximum(gl[0], gl[1]), jnp.maximum(gl[2], gl[3]))
    gidx = jnp.where(gl[0] >= gmax, 0, jnp.where(gl[1] >= gmax, 1, jnp.where(gl[2] >= gmax, 2, 3)))
    rows8 = lax.broadcasted_iota(I32, (8, tm), 0)
    onehot = jnp.where(rows8 == gidx, 1.0, 0.0)
    r = lax.broadcasted_iota(I32, (tm, tm), 0)
    c = lax.broadcasted_iota(I32, (tm, tm), 1)
    incl = _dot(onehot.astype(BF16), jnp.where(r <= c, 1.0, 0.0).astype(BF16))
    before = cnt_scr[:, 0:1]
    rank = jnp.sum(onehot * (incl - 1.0 + before), axis=0, keepdims=True)
    route_ref[...] = jnp.where(rows8 == 0, gidx, jnp.where(rows8 == 1, rank.astype(I32), 0))
    cnt_scr[...] = cnt_scr[...] + incl[:, tm - 1:tm]
    cnt_ref[...] = cnt_scr[...].astype(I32)


def _xattn(xq, kv, x1, wxo, g, wrh, wrl, br, batch, seq, tm):
    n, d = x1.shape
    nb = seq // tm
    mem = kv.shape[1]
    full = lambda a: pl.BlockSpec(a.shape, lambda r: (0,) * a.ndim)
    return pl.pallas_call(
        functools.partial(_xattn_kernel, tm=tm),
        grid=(n // tm,),
        in_specs=[pl.BlockSpec((tm, d), lambda r: (r, 0)),
                  pl.BlockSpec((None, mem, 2 * d), lambda r: (r // nb, 0, 0)),
                  pl.BlockSpec((tm, d), lambda r: (r, 0)),
                  full(wxo), full(g), full(wrh), full(wrl), full(br)],
        out_specs=[pl.BlockSpec((tm * (d // LANES), LANES), lambda r: (r, 0)),
                   pl.BlockSpec((8, tm), lambda r: (0, r)),
                   pl.BlockSpec((8, LANES), lambda r: (0, 0))],
        out_shape=[jax.ShapeDtypeStruct((n * (d // LANES), LANES), F32),
                   jax.ShapeDtypeStruct((8, n), I32),
                   jax.ShapeDtypeStruct((8, LANES), I32)],
        scratch_shapes=[pltpu.VMEM((8, LANES), F32)],
        compiler_params=_cparams("arbitrary"),
        name="xattn",
    )(xq, kv, x1, wxo, g, wrh, wrl, br)


def _plan_kernel(gidx_ref, rank_ref, cnt_ref, tos_ref, tg_ref, off_scr, *, n_tok, tmm):
    n_slots = tos_ref.shape[0]
    n_tiles = tg_ref.shape[0]
    off = 0
    last_group = 0
    ends = []
    for g in range(N_GROUPS):
        off_scr[g] = off
        cnt = cnt_ref[g]
        off = off + ((cnt + (tmm - 1)) // tmm) * tmm
        ends.append(off)
        last_group = jnp.where(cnt > 0, g, last_group)

    def clear(s, _):
        tos_ref[s] = -1
        return 0

    lax.fori_loop(0, n_slots, clear, 0, unroll=8)

    def place(t, _):
        tos_ref[off_scr[gidx_ref[t]] + rank_ref[t]] = t
        return 0

    lax.fori_loop(0, n_tok, place, 0, unroll=8)

    def tile(k, _):
        start = k * tmm
        g = ((start >= ends[0]).astype(I32) + (start >= ends[1]).astype(I32) + (start >= ends[2]).astype(I32))
        tg_ref[k] = jnp.minimum(g, last_group)
        return 0

    lax.fori_loop(0, n_tiles, tile, 0)


def _route_plan(gidx, rank, cnt, n_tok, tmm):
    n_tiles = n_tok // tmm + N_GROUPS
    smem = pl.BlockSpec(memory_space=pltpu.SMEM)
    return pl.pallas_call(
        functools.partial(_plan_kernel, n_tok=n_tok, tmm=tmm),
        in_specs=[smem, smem, smem],
        out_specs=[smem, smem],
        out_shape=[jax.ShapeDtypeStruct((n_tiles * tmm,), I32), jax.ShapeDtypeStruct((n_tiles,), I32)],
        scratch_shapes=[pltpu.SMEM((N_GROUPS,), I32)],
        name="route_plan",
    )(gidx, rank, cnt)


def _moe_kernel(tos_ref, tg_ref, x2_hbm, g_ref, wrh_ref, wrl_ref, br_ref, wg_ref, wu_ref, wd_ref, fg_ref,
                x3_hbm, xb, ob, gsem, ssem, *, tmm, n_tok, final):
    k = pl.program_id(0)
    nk = pl.num_programs(0)
    slot = k % 2

    sub = xb.shape[1] // tmm

    def tile_rows(t):
        return pl.ds(pl.multiple_of(t * sub, sub), sub)

    def gather_copy(tile, sl, r):
        tok = jnp.maximum(tos_ref[tile * tmm + r], 0)
        return pltpu.make_async_copy(x2_hbm.at[tile_rows(tok)], xb.at[sl, tile_rows(r)], gsem.at[sl])

    def scatter_copy(tile, sl, r):
        tok = tos_ref[tile * tmm + r]
        dst = jnp.where(tok >= 0, tok, n_tok + sl * tmm + r)
        return pltpu.make_async_copy(ob.at[sl, tile_rows(r)], x3_hbm.at[tile_rows(dst)], ssem.at[sl])

    def for_rows(fn):
        def body(r8, _):
            for u in range(DMA_UNROLL):
                fn(r8 * DMA_UNROLL + u, u % 2)
            return 0
        lax.fori_loop(0, tmm // DMA_UNROLL, body, 0)

    @pl.when(k == 0)
    def _():
        for_rows(lambda r, pri: gather_copy(0, 0, r).start(priority=pri))

    for_rows(lambda r, pri: gather_copy(k, slot, r).wait())

    @pl.when(k + 1 < nk)
    def _():
        for_rows(lambda r, pri: gather_copy(k + 1, 1 - slot, r).start(priority=pri))

    @pl.when(k >= 2)
    def _():
        for_rows(lambda r, pri: scatter_copy(k - 2, slot, r).wait())

    x = _load_rows(xb.at[slot], tmm)
    t = _rms(x, g_ref[...])
    tb = t.astype(BF16)
    th, tl = _split2(t)
    logits = _dot(th, wrh_ref[...]) + _dot(th, wrl_ref[...]) + _dot(tl, wrh_ref[...]) + br_ref[...]
    grp = tg_ref[k]
    lane = lax.broadcasted_iota(I32, (1, LANES), 1)
    lanef = lane.astype(F32)
    rmax = lambda a: jnp.max(a, axis=-1, keepdims=True)
    rsum = lambda a: jnp.sum(a, axis=-1, keepdims=True)
    gmask = lane < N_GROUPS
    gmax = rmax(jnp.where(gmask, logits, -jnp.inf))
    zg = rsum(jnp.where(gmask, jnp.exp(logits - gmax), 0.0))
    p_group = jnp.exp(rsum(jnp.where(lane == grp, logits, 0.0)) - gmax) / zg
    e_lo = N_GROUPS + EXPERTS_PER_GROUP * grp
    emask = (lane >= e_lo) & (lane < e_lo + EXPERTS_PER_GROUP)
    em = jnp.where(emask, logits, -jnp.inf)
    e1 = rmax(em)
    i1 = jnp.min(jnp.where(em == e1, lanef, 1e9), axis=-1, keepdims=True)
    em2 = jnp.where(lanef == i1, -jnp.inf, em)
    e2 = rmax(em2)
    i2 = jnp.min(jnp.where(em2 == e2, lanef, 1e9), axis=-1, keepdims=True)
    ze = rsum(jnp.where(emask, jnp.exp(logits - e1), 0.0))
    p1 = 1.0 / ze
    p2 = jnp.exp(e2 - e1) / ze
    w1 = p_group * (p1 / (p1 + p2))
    w2 = p_group * (p2 / (p1 + p2))
    hid = _dot(tb, wg_ref[...])
    hid = hid * _sigmoid(hid) * _dot(tb, wu_ref[...])
    ff = hid.shape[1] // EXPERTS_PER_GROUP
    cw_lanes = jnp.where(lanef == i1, w1, 0.0) + jnp.where(lanef == i2, w2, 0.0)
    parts = []
    for e in range(EXPERTS_PER_GROUP):
        cw = rsum(jnp.where(lane == e_lo + e, cw_lanes, 0.0))
        parts.append((hid[:, e * ff:(e + 1) * ff] * cw).astype(BF16))
    y = x + _dot(jnp.concatenate(parts, axis=1), wd_ref[...])
    if final:
        y = _rms(y, fg_ref[...])
    _store_rows(ob.at[slot], y)

    for_rows(lambda r, pri: scatter_copy(k, slot, r).start(priority=pri))

    @pl.when(k == nk - 1)
    def _():
        @pl.when(k >= 1)
        def _():
            for_rows(lambda r, pri: scatter_copy(k - 1, 1 - slot, r).wait())
        for_rows(lambda r, pri: scatter_copy(k, slot, r).wait())


def _moe(tos, tg, x2, g, wrh_t, wrl_t, br_row, wg, wu, wd, fg, tmm, final):
    sub = g.shape[1] // LANES
    n_tok = x2.shape[0] // sub
    n_tiles = tg.shape[0]
    full = lambda a: pl.BlockSpec(a.shape, lambda k, tos, tg: (0,) * a.ndim)
    grp = lambda a: pl.BlockSpec((None,) + a.shape[1:], lambda k, tos, tg: (tg[k], 0, 0))
    grid_spec = pltpu.PrefetchScalarGridSpec(
        num_scalar_prefetch=2,
        grid=(n_tiles,),
        in_specs=[pl.BlockSpec(memory_space=pl.ANY), full(g), full(wrh_t), full(wrl_t), full(br_row),
                  grp(wg), grp(wu), grp(wd), full(fg)],
        out_specs=pl.BlockSpec(memory_space=pl.ANY),
        scratch_shapes=[pltpu.VMEM((2, tmm * sub, LANES), F32), pltpu.VMEM((2, tmm * sub, LANES), F32),
                        pltpu.SemaphoreType.DMA((2,)), pltpu.SemaphoreType.DMA((2,))],
    )
    return pl.pallas_call(
        functools.partial(_moe_kernel, tmm=tmm, n_tok=n_tok, final=final),
        grid_spec=grid_spec,
        out_shape=jax.ShapeDtypeStruct(((n_tok + 2 * tmm) * sub, LANES), F32),
        compiler_params=_cparams("arbitrary"),
        name="moe",
    )(tos, tg, x2, g, wrh_t, wrl_t, br_row, wg, wu, wd, fg)


def _row(v):
    return v.reshape(1, -1)


def kernel(x, mem, mix_norm_g, w_in, fox_f_bias, fox_norm_g, hgrn_lb_logits, hgrn_norm_g, conv_w, conv_b,
           conv_norm_g, conv_norm_b, w_out, xatt_norm_g, mem_norm_g, w_xq, w_xkv, w_xo, ffn_norm_g,
           w_group, b_group, w_router, b_router, w_gate, w_up, w_down, final_norm_g):
    batch, seq, d = x.shape
    depth = w_in.shape[0]
    n = batch * seq
    tm = min(512, seq)
    tmm = min(512, seq)
    fblk = min(FOX_BLOCK, tm)
    fw = FOX_HEADS * FOX_HEAD_DIM
    hw = HGRN_HEADS * HGRN_DIM
    cw = conv_w.shape[-1]
    assert seq % tm == 0 and seq % HGRN_BLOCK == 0 and d == fw + hw + cw

    offs = [0]
    for width in (fw, fw, fw, FOX_HEADS, hw, hw, hw, hw, cw, cw):
        offs.append(offs[-1] + width)
    seg = lambda a, i, j: a[..., offs[i]:offs[j]]
    wqkv = jnp.concatenate([seg(w_in, 0, 1) * (FOX_HEAD_DIM ** -0.5 * LOG2E), seg(w_in, 1, 3)], axis=-1).astype(BF16)
    wff = jnp.swapaxes(seg(w_in, 3, 4), 1, 2).astype(BF16)
    wh = jnp.concatenate([seg(w_in, 4, 5), seg(w_in, 5, 6), seg(w_in, 6, 7), seg(w_in, 7, 8)], axis=-1).astype(BF16)
    wc = seg(w_in, 8, 10).astype(BF16)
    wo = w_out.astype(BF16)
    wxq = (w_xq * ((d // XATT_HEADS) ** -0.5)).astype(BF16)
    wxkv = w_xkv.astype(BF16)
    wxo = w_xo.astype(BF16)
    n_exp = N_GROUPS * EXPERTS_PER_GROUP
    wr = jnp.concatenate([w_group, w_router], axis=-1)
    wr_hi = wr.astype(BF16)
    wr_lo = (wr - wr_hi.astype(F32)).astype(BF16)
    pad_rows = lambda a: jnp.pad(jnp.swapaxes(a, 1, 2), ((0, 0), (0, ROUTER_ROWS - a.shape[2]), (0, 0)))
    pad_cols = lambda a: jnp.pad(a, ((0, 0), (0, 0), (0, LANES - a.shape[2])))
    br = jnp.concatenate([b_group, b_router], axis=-1)
    br_col = jnp.pad(br, ((0, 0), (0, ROUTER_ROWS - br.shape[1])))[:, :, None]
    br_row = jnp.pad(br, ((0, 0), (0, LANES - br.shape[1])))[:, None, :]
    ff = w_gate.shape[-1]
    to_cols = lambda w: jnp.transpose(w, (0, 1, 3, 2, 4)).reshape(depth, N_GROUPS, d, EXPERTS_PER_GROUP * ff).astype(BF16)
    wg, wu = to_cols(w_gate), to_cols(w_up)
    wd = w_down.reshape(depth, N_GROUPS, EXPERTS_PER_GROUP * ff, d).astype(BF16)
    conv_w_pad = jnp.pad(conv_w, ((0, 0), (0, CONV_HALO - CONV_WIDTH), (0, 0)))
    wrh_rows, wrl_rows, wrh_cols, wrl_cols = pad_rows(wr_hi), pad_rows(wr_lo), pad_cols(wr_hi), pad_cols(wr_lo)

    kv_all = _mem_kv(mem.reshape(-1, d), _row(mem_norm_g), wxkv).reshape(depth, batch, mem.shape[1], 2 * d)

    xs = x.reshape(n, d)
    for l in range(depth):
        qkv, hy, cy, lf, qn, kn = _mix_in(xs, _row(mix_norm_g[l]), wqkv[l], wh[l], wc[l], wff[l],
                                          fox_f_bias[l].reshape(-1, 1), batch, seq, tm)
        c, keep = _fox_plan(lf, qn, kn, fblk, tm)
        fo = _fox_attention(keep.reshape(-1), qkv, c, _row(fox_norm_g[l]), batch, seq, fblk, tm // fblk)
        ho = _hgrn(hy, hgrn_lb_logits, _row(hgrn_norm_g[l]), l, batch, seq, min(1024, seq))
        co = _conv(cy, conv_w_pad[l], _row(conv_b[l]), _row(conv_norm_g[l]), _row(conv_norm_b[l]), batch, seq, tm)
        x1, xq = _mix_out(fo, ho, co, xs, wo[l], _row(xatt_norm_g[l]), wxq[l], tm)
        x2, route, cnt = _xattn(xq, kv_all[l], x1, wxo[l], _row(ffn_norm_g[l]),
                                wrh_rows[l], wrl_rows[l], br_col[l], batch, seq, tm)
        tos, tg = _route_plan(route[0], route[1], cnt[:N_GROUPS, 0], n, tmm)
        xs = _moe(tos, tg, x2, _row(ffn_norm_g[l]), wrh_cols[l], wrl_cols[l], br_row[l],
                  wg[l], wu[l], wd[l], _row(final_norm_g), tmm, final=(l == depth - 1))
    return xs[:n * (d // LANES)].reshape(batch, seq, d)
```

```python
import functools

import jax
import jax.numpy as jnp
from jax import lax
from jax.experimental import pallas as pl
from jax.experimental.pallas import tpu as pltpu

F32 = jnp.float32
BF16 = jnp.bfloat16
I32 = jnp.int32

EPS = 1e-6
LANES = 128
SUBLANES = 8
FOX_HEADS = 8
FOX_HEAD_DIM = 64
LOG2E = 1.4426950408889634
FOX_BLOCK = 512
FOX_SKIP_EXPONENT = -45.0
FOX_NORM_SLACK = 1.02
HGRN_HEADS = 4
HGRN_DIM = 64
HGRN_CHUNK = 16
HGRN_BLOCK = 128
CONV_WIDTH = 31
CONV_HALO = 32
CONV_GROUP = 64
XATT_HEADS = 4
N_GROUPS = 4
EXPERTS_PER_GROUP = 4
ROUTER_ROWS = 32
DMA_UNROLL = 8
VMEM_LIMIT_BYTES = 56 * 1024 * 1024

NT_DIMS = (((1,), (1,)), ((), ()))


def _cparams(*sem):
    return pltpu.CompilerParams(dimension_semantics=sem, vmem_limit_bytes=VMEM_LIMIT_BYTES)


def _dot(a, b):
    return jnp.dot(a, b, preferred_element_type=F32)


def _dot_nt(a, b):
    return lax.dot_general(a, b, NT_DIMS, preferred_element_type=F32)


def _rms(x, g):
    ms = jnp.mean(x * x, axis=-1, keepdims=True)
    return x * lax.rsqrt(ms + EPS) * g


def _sigmoid(x):
    return 1.0 / (1.0 + jnp.exp(-x))


def _split2(x):
    hi = x.astype(BF16)
    lo = (x - hi.astype(F32)).astype(BF16)
    return hi, lo


def _split3(x):
    hi = x.astype(BF16)
    r = x - hi.astype(F32)
    mid = r.astype(BF16)
    lo = (r - mid.astype(F32)).astype(BF16)
    return hi, mid, lo


def _halves_select(lane_lo_mask, a, b):
    return jnp.where(lane_lo_mask, a, b)


def _load_rows(ref, rows):
    if ref.shape[0] == rows:
        return ref[...]
    sub = ref.shape[0] // rows
    return jnp.concatenate([ref[pl.ds(c, rows, stride=sub), :] for c in range(sub)], axis=1)


def _store_rows(ref, val):
    rows = val.shape[0]
    if ref.shape[0] == rows:
        ref[...] = val
    else:
        sub = ref.shape[0] // rows
        for c in range(sub):
            ref[pl.ds(c, rows, stride=sub), :] = val[:, c * LANES:(c + 1) * LANES]


def _pair_rms(o, lo_mask, width):
    o2 = o * o
    s_all = jnp.sum(o2, axis=-1, keepdims=True)
    s_lo = jnp.sum(jnp.where(lo_mask, o2, 0.0), axis=-1, keepdims=True)
    ms = jnp.where(lo_mask, s_lo, s_all - s_lo) * (1.0 / width)
    return o * lax.rsqrt(ms + EPS)


def _mem_kv_kernel(mem_ref, g_ref, w_ref, o_ref):
    t = _rms(mem_ref[...], g_ref[...]).astype(BF16)
    o_ref[...] = _dot(t, w_ref[...]).astype(BF16)


def _mem_kv(mem2d, g, w_xkv):
    depth, d, d2 = w_xkv.shape
    rows = mem2d.shape[0]
    return pl.pallas_call(
        _mem_kv_kernel,
        grid=(depth,),
        in_specs=[pl.BlockSpec((rows, d), lambda l: (0, 0)),
                  pl.BlockSpec((1, d), lambda l: (0, 0)),
                  pl.BlockSpec((None, d, d2), lambda l: (l, 0, 0))],
        out_specs=pl.BlockSpec((None, rows, d2), lambda l: (l, 0, 0)),
        out_shape=jax.ShapeDtypeStruct((depth, rows, d2), BF16),
        compiler_params=_cparams("arbitrary"),
        name="mem_kv",
    )(mem2d, g, w_xkv)


def _head_norm_max(y, ind):
    n2 = _dot((y[:, :ind.shape[0]] * y[:, :ind.shape[0]]).astype(BF16), ind)
    top = jnp.broadcast_to(jnp.max(n2, axis=0, keepdims=True), (FOX_HEADS, LANES))
    row = lax.broadcasted_iota(I32, (FOX_HEADS, LANES), 0)
    lane = lax.broadcasted_iota(I32, (FOX_HEADS, LANES), 1)
    pick = lambda first: jnp.broadcast_to(
        jnp.sqrt(jnp.sum(jnp.where(lane == row + first, top, 0.0), axis=1, keepdims=True)), (FOX_HEADS, LANES))
    return pick(0), pick(FOX_HEADS)


def _mix_in_kernel(x_ref, g_ref, wqkv_ref, wh_ref, wc_ref, wff_ref, fb_ref, ind_ref,
                   qkv_ref, hy_ref, cy_ref, lf_ref, qn_ref, kn_ref, kn_scr, *, nb):
    t = _rms(_load_rows(x_ref, qkv_ref.shape[0]), g_ref[...]).astype(BF16)
    y = _dot(t, wqkv_ref[...])
    qkv_ref[...] = y.astype(BF16)
    hy_ref[...] = _dot(t, wh_ref[...]).astype(BF16)
    cy_ref[...] = _dot(t, wc_ref[...]).astype(BF16)
    z = _dot_nt(wff_ref[...], t) + fb_ref[...]
    lf_ref[...] = (jnp.minimum(z, 0.0) - jnp.log1p(jnp.exp(-jnp.abs(z)))) * LOG2E
    q_norm, k_norm = _head_norm_max(y, ind_ref[...])
    qn_ref[...] = q_norm

    @pl.when(pl.program_id(0) % nb == 0)
    def _():
        kn_scr[...] = jnp.zeros_like(kn_scr)

    kn_scr[...] = jnp.maximum(kn_scr[...], k_norm)
    kn_ref[...] = kn_scr[...]


def _row_block(a, tm, d):
    sub = 1 if a.shape[1] == d else d // a.shape[1]
    return pl.BlockSpec((tm * sub, a.shape[1]), lambda r: (r, 0))


def _mix_in(x2d, g, wqkv, wh, wc, wff, fb, batch, seq, tm):
    n = batch * seq
    nb = seq // tm
    full = lambda a: pl.BlockSpec(a.shape, lambda r: (0,) * a.ndim)
    qk_cols = 2 * FOX_HEADS * FOX_HEAD_DIM
    ind = (jnp.arange(qk_cols)[:, None] // FOX_HEAD_DIM == jnp.arange(LANES)[None, :]).astype(BF16)
    return pl.pallas_call(
        functools.partial(_mix_in_kernel, nb=nb),
        grid=(n // tm,),
        in_specs=[_row_block(x2d, tm, wqkv.shape[0]), full(g), full(wqkv), full(wh), full(wc),
                  full(wff), full(fb), full(ind)],
        out_specs=[pl.BlockSpec((tm, wqkv.shape[1]), lambda r: (r, 0)),
                   pl.BlockSpec((tm, wh.shape[1]), lambda r: (r, 0)),
                   pl.BlockSpec((tm, wc.shape[1]), lambda r: (r, 0)),
                   pl.BlockSpec((None, FOX_HEADS, tm), lambda r: (r // nb, 0, r % nb)),
                   pl.BlockSpec((None, FOX_HEADS, LANES), lambda r: (r, 0, 0)),
                   pl.BlockSpec((None, FOX_HEADS, LANES), lambda r: (r // nb, 0, 0))],
        out_shape=[jax.ShapeDtypeStruct((n, wqkv.shape[1]), BF16),
                   jax.ShapeDtypeStruct((n, wh.shape[1]), BF16),
                   jax.ShapeDtypeStruct((n, wc.shape[1]), BF16),
                   jax.ShapeDtypeStruct((batch, FOX_HEADS, seq), F32),
                   jax.ShapeDtypeStruct((n // tm, FOX_HEADS, LANES), F32),
                   jax.ShapeDtypeStruct((batch, FOX_HEADS, LANES), F32)],
        scratch_shapes=[pltpu.VMEM((FOX_HEADS, LANES), F32)],
        compiler_params=_cparams("arbitrary"),
        name="mix_in",
    )(x2d, g, wqkv, wh, wc, wff, fb, ind)


def _fox_plan_kernel(lf_ref, qn_ref, kn_ref, c_ref, keep_ref, *, blk, stat_rows):
    seq = lf_ref.shape[-1]
    nblk = seq // blk
    r = lax.broadcasted_iota(I32, (blk, blk), 0)
    c = lax.broadcasted_iota(I32, (blk, blk), 1)
    upper = jnp.where(r <= c, 1.0, 0.0).astype(BF16)
    lane = lax.broadcasted_iota(I32, (FOX_HEADS, LANES), 1)
    carry = jnp.zeros((FOX_HEADS, 1), F32)
    first = jnp.zeros((FOX_HEADS, LANES), F32)
    last = jnp.zeros((FOX_HEADS, LANES), F32)
    qn = jnp.zeros((FOX_HEADS, LANES), F32)
    for b in range(nblk):
        x = lf_ref[:, b * blk:(b + 1) * blk]
        hi, mid, lo = _split3(x)
        cb = _dot(hi, upper) + _dot(mid, upper) + _dot(lo, upper) + carry
        for h in range(FOX_HEADS):
            c_ref[h, :, b * blk:(b + 1) * blk] = cb[h:h + 1, :]
        carry = cb[:, blk - 1:blk]
        first = jnp.where(lane == b, cb[:, 0:1], first)
        last = jnp.where(lane == b, carry, last)
        qn = jnp.where(lane == b, qn_ref[b * blk // stat_rows], qn)
    bound = FOX_NORM_SLACK * 2.0 * qn * kn_ref[...] + first
    keep = jnp.zeros((FOX_HEADS, LANES), I32)
    for i in range(nblk):
        live = (bound[:, i:i + 1] - last >= FOX_SKIP_EXPONENT * LOG2E) & (lane < i)
        keep = jnp.where(lane == i, jnp.sum(live.astype(I32), axis=1, keepdims=True), keep)
    keep_ref[...] = keep


def _fox_plan(lf, qn, kn, blk, stat_rows):
    batch, heads, seq = lf.shape
    assert seq // blk <= LANES
    return pl.pallas_call(
        functools.partial(_fox_plan_kernel, blk=blk, stat_rows=stat_rows),
        grid=(batch,),
        in_specs=[pl.BlockSpec((None, heads, seq), lambda b: (b, 0, 0)),
                  pl.BlockSpec((seq // stat_rows, heads, LANES), lambda b: (b, 0, 0)),
                  pl.BlockSpec((None, heads, LANES), lambda b: (b, 0, 0))],
        out_specs=[pl.BlockSpec((None, heads, 1, seq), lambda b: (b, 0, 0, 0)),
                   pl.BlockSpec((None, heads, LANES), lambda b: (b, 0, 0))],
        out_shape=[jax.ShapeDtypeStruct((batch, heads, 1, seq), F32),
                   jax.ShapeDtypeStruct((batch, heads, LANES), I32)],
        compiler_params=_cparams("arbitrary"),
        name="fox_plan",
    )(lf, qn, kn)


def _fox_kernel(keep_ref, q_ref, k_ref, v_ref, c_ref, g_ref, o_ref, *, tq, qpb):
    head0 = (pl.program_id(0) * FOX_HEADS + 2 * pl.program_id(1)) * LANES
    lane = lax.broadcasted_iota(I32, (1, LANES), 1)
    lo_mask = lane < FOX_HEAD_DIM
    row = lax.broadcasted_iota(I32, (tq, tq), 0)
    col = lax.broadcasted_iota(I32, (tq, tq), 1)
    causal = row >= col

    for qb in range(qpb):
        i = pl.program_id(2) * qpb + qb
        n_prev = jnp.maximum(keep_ref[head0 + i], keep_ref[head0 + LANES + i])
        q = q_ref[qb * tq:(qb + 1) * tq, :]
        zero = jnp.zeros_like(q)
        qh = (jnp.where(lo_mask, q, zero), jnp.where(lo_mask, zero, q))
        q0 = pl.multiple_of(i * tq, tq)
        cq0 = tuple(c_ref[h, :, pl.ds(q0, LANES)][:, 0:1] for h in range(2))

        def block(j, carry, masked, qh=qh, cq0=cq0):
            k0 = pl.multiple_of(j * tq, tq)
            kb = k_ref[pl.ds(k0, tq), :]
            vb = v_ref[pl.ds(k0, tq), :]
            s = [_dot_nt(qh[h], kb) for h in range(2)]
            bias = [cq0[h] - c_ref[h, :, pl.ds(k0, tq)] for h in range(2)]
            out = []
            out = []
            for h in range(2):
                m, l, acc = carry[h]
                sh = s[h] + bias[h]
                if masked:
                    sh = jnp.where(causal, sh, -jnp.inf)
                m_new = jnp.maximum(m, jnp.max(sh, axis=-1, keepdims=True))
                alpha = jnp.exp2(m - m_new)
                p = jnp.exp2(sh - m_new)
                l = alpha * l + jnp.sum(p, axis=-1, keepdims=True)
                acc = alpha * acc + _dot(p.astype(BF16), vb)
                out.append((m_new, l, acc))
            return tuple(out)

        init = tuple((jnp.full((tq, 1), -jnp.inf, F32), jnp.zeros((tq, 1), F32), jnp.zeros((tq, LANES), F32))
                     for _ in range(2))
        carry = block(i, init, True)
        carry = lax.fori_loop(i - n_prev, i, lambda j, c, block=block: block(j, c, False), carry)
        (_, l0, a0), (_, l1, a1) = carry
        o = jnp.where(lo_mask, a0 * (1.0 / l0), a1 * (1.0 / l1))
        o_ref[qb * tq:(qb + 1) * tq, :] = (_pair_rms(o, lo_mask, FOX_HEAD_DIM) * g_ref[...]).astype(o_ref.dtype)


def _fox_attention(keep, qkv, c, gain, batch, seq, tq, qpb):
    n = qkv.shape[0]
    pairs = FOX_HEADS // 2
    rows = tq * qpb
    nq = seq // rows
    grid_spec = pltpu.PrefetchScalarGridSpec(
        num_scalar_prefetch=1,
        grid=(batch, pairs, nq),
        in_specs=[pl.BlockSpec((rows, LANES), lambda b, p, i, keep: (b * nq + i, p)),
                  pl.BlockSpec((seq, LANES), lambda b, p, i, keep: (b, pairs + p)),
                  pl.BlockSpec((seq, LANES), lambda b, p, i, keep: (b, 2 * pairs + p)),
                  pl.BlockSpec((None, 2, 1, seq), lambda b, p, i, keep: (b, p, 0, 0)),
                  pl.BlockSpec((1, LANES), lambda b, p, i, keep: (0, p))],
        out_specs=pl.BlockSpec((rows, LANES), lambda b, p, i, keep: (b * nq + i, p)),
    )
    return pl.pallas_call(
        functools.partial(_fox_kernel, tq=tq, qpb=qpb),
        grid_spec=grid_spec,
        out_shape=jax.ShapeDtypeStruct((n, pairs * LANES), BF16),
        compiler_params=_cparams("parallel", "parallel", "arbitrary"),
        name="fox_attention",
    )(keep, qkv, qkv, qkv, c, gain)


def _hgrn_kernel(q_ref, f_ref, v_ref, gate_ref, lbz_ref, gain_ref, o_ref, st_ref, u_scr, prev_scr, *, layer, n_sub):
    T, C = HGRN_BLOCK, HGRN_CHUNK
    nchunk = T // C
    i = pl.program_id(2)

    @pl.when(i == 0)
    def _():
        st_ref[...] = jnp.zeros_like(st_ref)

    z = lbz_ref[...]
    e = jnp.exp(z - jnp.max(z, axis=0, keepdims=True))
    pz = e / jnp.sum(e, axis=0, keepdims=True)
    lb = jnp.zeros((1, LANES), F32)
    for j in range(1, layer + 1):
        lb = lb + pz[j:j + 1, :]

    lane = lax.broadcasted_iota(I32, (1, LANES), 1)
    lo_mask = lane < HGRN_DIM
    r = lax.broadcasted_iota(I32, (T, T), 0)
    c = lax.broadcasted_iota(I32, (T, T), 1)
    same_chunk = (r // C) == (c // C)
    one = lambda m: jnp.where(m, 1.0, 0.0).astype(BF16)
    scan_mat = jnp.concatenate([one(same_chunk & (c <= r)),
                                one(same_chunk & ((c % C) <= C // 2)),
                                one(same_chunk)], axis=0)
    intra_mask = same_chunk & (c <= r)
    chunk_of_row = lax.broadcasted_iota(I32, (T, LANES), 0) // C
    vr = lax.broadcasted_iota(I32, (LANES, nchunk * LANES), 0)
    kc = lax.broadcasted_iota(I32, (LANES, nchunk * LANES), 1)
    same_head = (vr < HGRN_DIM) == ((kc % LANES) < HGRN_DIM)

    subs = range(n_sub)
    rows = [pl.ds(sb * T, T) for sb in subs]
    q = [q_ref[r_, :].astype(F32) for r_ in rows]
    f = [lb + (1.0 - lb) * _sigmoid(f_ref[r_, :].astype(F32)) for r_ in rows]
    kk = [1.0 - f_ for f_ in f]
    parts = []
    for f_ in f:
        parts.extend(_split2(jnp.log(f_)))
    sc = _dot(scan_mat, jnp.concatenate(parts, axis=1))
    sc = [sc[:, (2 * sb) * LANES:(2 * sb + 1) * LANES] + sc[:, (2 * sb + 1) * LANES:(2 * sb + 2) * LANES]
          for sb in subs]
    b = [s_[:T] for s_ in sc]
    b_mid = [s_[T:2 * T] for s_ in sc]
    b_last = [s_[2 * T:] for s_ in sc]
    v = [v_ref[r_, :] for r_ in rows]
    for sb in subs:
        k_out = kk[sb] * jnp.exp(b_last[sb] - b[sb])
        k_exp = jnp.concatenate([jnp.where(chunk_of_row == j, k_out, 0.0) for j in range(nchunk)],
                                axis=1).astype(BF16)
        v_t = v[sb].astype(F32).T.astype(BF16)
        u_scr[sb] = jnp.where(same_head, _dot(v_t, k_exp), 0.0)
    att = []
    for sb in subs:
        q_in = (q[sb] * jnp.exp(b[sb] - b_mid[sb])).astype(BF16)
        k_in = (kk[sb] * jnp.exp(b_mid[sb] - b[sb])).astype(BF16)
        zq = jnp.zeros_like(q_in)
        att.append([jnp.where(intra_mask, _dot_nt(qm, k_in), 0.0).astype(BF16)
                    for qm in (jnp.where(lo_mask, q_in, zq), jnp.where(lo_mask, zq, q_in))])
    o = [jnp.where(lo_mask, _dot(att[sb][0], v[sb]), _dot(att[sb][1], v[sb])) for sb in subs]
    state = st_ref[...]
    for sb in subs:
        decay = jnp.exp(b_last[sb])
        for j in range(nchunk):
            prev_scr[sb, j * LANES:(j + 1) * LANES, :] = state.astype(BF16)
            state = state * decay[j * C:j * C + 1, :] + u_scr[sb, :, j * LANES:(j + 1) * LANES]
    st_ref[...] = state
    for sb in subs:
        q_out = (q[sb] * jnp.exp(b[sb])).astype(BF16)
        o_all = _dot_nt(q_out, prev_scr[sb])
        acc = o[sb]
        for j in range(nchunk):
            acc = acc + jnp.where(chunk_of_row == j, o_all[:, j * LANES:(j + 1) * LANES], 0.0)
        gate = gate_ref[rows[sb], :].astype(F32)
        y = _pair_rms(acc, lo_mask, HGRN_DIM) * gain_ref[...] * (gate * _sigmoid(gate))
        o_ref[rows[sb], :] = y.astype(o_ref.dtype)


def _hgrn(hy, lb_logits, gain, layer, batch, seq, tg):
    n = hy.shape[0]
    pairs = HGRN_HEADS // 2
    nb = seq // tg
    depth = lb_logits.shape[0]
    col = lambda k: pl.BlockSpec((tg, LANES), lambda b, p, i, k=k: (b * nb + i, k * pairs + p))
    n_sub = tg // HGRN_BLOCK
    states = (HGRN_BLOCK // HGRN_CHUNK) * LANES
    return pl.pallas_call(
        functools.partial(_hgrn_kernel, layer=layer, n_sub=n_sub),
        grid=(batch, pairs, nb),
        in_specs=[col(0), col(1), col(2), col(3),
                  pl.BlockSpec((depth, LANES), lambda b, p, i: (0, p)),
                  pl.BlockSpec((1, LANES), lambda b, p, i: (0, p))],
        out_specs=pl.BlockSpec((tg, LANES), lambda b, p, i: (b * nb + i, p)),
        out_shape=jax.ShapeDtypeStruct((n, pairs * LANES), BF16),
        scratch_shapes=[pltpu.VMEM((LANES, LANES), F32),
                        pltpu.VMEM((n_sub, LANES, states), F32),
                        pltpu.VMEM((n_sub, states, LANES), BF16)],
        compiler_params=_cparams("parallel", "parallel", "arbitrary"),
        name="hgrn",
    )(hy, hy, hy, hy, lb_logits, gain)


def _conv_kernel(cu_ref, cg_ref, w_ref, b_ref, ng_ref, nb_ref, o_ref, a_scr, sh_scr, *, tm):
    i = pl.program_id(1)
    sub = SUBLANES

    @pl.when(i == 0)
    def _():
        a_scr[0:CONV_HALO, :] = jnp.zeros((CONV_HALO, a_scr.shape[1]), F32)

    @pl.when(i > 0)
    def _():
        a_scr[0:CONV_HALO, :] = a_scr[tm:tm + CONV_HALO, :]

    a_scr[CONV_HALO:CONV_HALO + tm, :] = cu_ref[...].astype(F32) * _sigmoid(cg_ref[...].astype(F32))
    span = CONV_HALO + tm - sub
    for k in range(1, sub):
        sh_scr[k - 1, 0:span, :] = a_scr[k:k + span, :]
    acc = jnp.zeros((tm, a_scr.shape[1]), F32) + b_ref[...]
    first = CONV_HALO - (CONV_WIDTH - 1)
    for w in range(CONV_WIDTH):
        base, k = divmod(first + w, sub)
        src = a_scr if k == 0 else sh_scr.at[k - 1]
        acc = acc + src[base * sub:base * sub + tm, :] * w_ref[w:w + 1, :]
    lane = lax.broadcasted_iota(I32, (1, LANES), 1)
    lo_mask = lane < CONV_GROUP
    halves = []
    for hh in range(acc.shape[1] // LANES):
        xh = acc[:, hh * LANES:(hh + 1) * LANES]
        s_all = jnp.sum(xh, axis=-1, keepdims=True)
        s_lo = jnp.sum(jnp.where(lo_mask, xh, 0.0), axis=-1, keepdims=True)
        d = xh - jnp.where(lo_mask, s_lo, s_all - s_lo) * (1.0 / CONV_GROUP)
        halves.append(_pair_rms(d, lo_mask, CONV_GROUP))
    y = jnp.concatenate(halves, axis=1) * ng_ref[...] + nb_ref[...]
    o_ref[...] = (y * _sigmoid(y)).astype(o_ref.dtype)


def _conv(cy, w, b, ng, nb_, batch, seq, tm):
    n = cy.shape[0]
    ch = cy.shape[1] // 2
    nb = seq // tm
    full = lambda a: pl.BlockSpec(a.shape, lambda bb, i: (0,) * a.ndim)
    return pl.pallas_call(
        functools.partial(_conv_kernel, tm=tm),
        grid=(batch, nb),
        in_specs=[pl.BlockSpec((tm, ch), lambda bb, i: (bb * nb + i, 0)),
                  pl.BlockSpec((tm, ch), lambda bb, i: (bb * nb + i, 1)),
                  full(w), full(b), full(ng), full(nb_)],
        out_specs=pl.BlockSpec((tm, ch), lambda bb, i: (bb * nb + i, 0)),
        out_shape=jax.ShapeDtypeStruct((n, ch), BF16),
        scratch_shapes=[pltpu.VMEM((CONV_HALO + tm, ch), F32),
                        pltpu.VMEM((SUBLANES - 1, CONV_HALO + tm, ch), F32)],
        compiler_params=_cparams("parallel", "arbitrary"),
        name="conv",
    )(cy, cy, w, b, ng, nb_)


def _mix_out_kernel(fo_ref, ho_ref, co_ref, x_ref, wo_ref, g_ref, wq_ref, x1_ref, xq_ref):
    nf, nh = fo_ref.shape[1], ho_ref.shape[1]
    y = (_dot(fo_ref[...], wo_ref[0:nf, :]) + _dot(ho_ref[...], wo_ref[nf:nf + nh, :])
         + _dot(co_ref[...], wo_ref[nf + nh:, :]))
    x1 = _load_rows(x_ref, y.shape[0]) + y
    x1_ref[...] = x1
    xq_ref[...] = _dot(_rms(x1, g_ref[...]).astype(BF16), wq_ref[...]).astype(BF16)


def _mix_out(fo, ho, co, x2d, wo, g, wq, tm):
    n, d = fo.shape[0], wo.shape[1]
    full = lambda a: pl.BlockSpec(a.shape, lambda r: (0,) * a.ndim)
    rowblk = lambda a: _row_block(a, tm, d if a is x2d else a.shape[1])
    return pl.pallas_call(
        _mix_out_kernel,
        grid=(n // tm,),
        in_specs=[rowblk(fo), rowblk(ho), rowblk(co), rowblk(x2d), full(wo), full(g), full(wq)],
        out_specs=[pl.BlockSpec((tm, d), lambda r: (r, 0)), pl.BlockSpec((tm, d), lambda r: (r, 0))],
        out_shape=[jax.ShapeDtypeStruct((n, d), F32), jax.ShapeDtypeStruct((n, d), BF16)],
        compiler_params=_cparams("arbitrary"),
        name="mix_out",
    )(fo, ho, co, x2d, wo, g, wq)


def _router_logits_t(t, wr_hi, wr_lo, br):
    th, tl = _split2(t)
    return _dot_nt(wr_hi, th) + _dot_nt(wr_lo, th) + _dot_nt(wr_hi, tl) + br


def _xattn_kernel(xq_ref, kv_ref, x1_ref, wxo_ref, g_ref, wrh_ref, wrl_ref, br_ref,
                  x2_ref, route_ref, cnt_ref, cnt_scr, *, tm):
    step = pl.program_id(0)

    @pl.when(step == 0)
    def _():
        cnt_scr[...] = jnp.zeros_like(cnt_scr)

    d = xq_ref.shape[1]
    hd = d // XATT_HEADS
    scores = [_dot_nt(xq_ref[:, h * hd:(h + 1) * hd], kv_ref[:, h * hd:(h + 1) * hd]) for h in range(XATT_HEADS)]
    probs = [jnp.exp(s - jnp.max(s, axis=-1, keepdims=True)) for s in scores]
    heads = []
    for h, p in enumerate(probs):
        vh = kv_ref[:, d + h * hd:d + (h + 1) * hd]
        l = jnp.sum(p, axis=-1, keepdims=True)
        heads.append((_dot(p.astype(BF16), vh) * (1.0 / l)).astype(BF16))
    x2 = x1_ref[...] + _dot(jnp.concatenate(heads, axis=1), wxo_ref[...])
    _store_rows(x2_ref, x2)
    logits = _router_logits_t(_rms(x2, g_ref[...]), wrh_ref[...], wrl_ref[...], br_ref[...])
    gl = [logits[g:g + 1, :] for g in range(N_GROUPS)]
    gmax = jnp.maximum(jnp.maximum(gl[0], gl[1]), jnp.maximum(gl[2], gl[3]))
    gidx = jnp.where(gl[0] >= gmax, 0, jnp.where(gl[1] >= gmax, 1, jnp.where(gl[2] >= gmax, 2, 3)))
    rows8 = lax.broadcasted_iota(I32, (8, tm), 0)
    onehot = jnp.where(rows8 == gidx, 1.0, 0.0)
    r = lax.broadcasted_iota(I32, (tm, tm), 0)
    c = lax.broadcasted_iota(I32, (tm, tm), 1)
    incl = _dot(onehot.astype(BF16), jnp.where(r <= c, 1.0, 0.0).astype(BF16))
    before = cnt_scr[:, 0:1]
    rank = jnp.sum(onehot * (incl - 1.0 + before), axis=0, keepdims=True)
    route_ref[...] = jnp.where(rows8 == 0, gidx, jnp.where(rows8 == 1, rank.astype(I32), 0))
    cnt_scr[...] = cnt_scr[...] + incl[:, tm - 1:tm]
    cnt_ref[...] = cnt_scr[...].astype(I32)


def _xattn(xq, kv, x1, wxo, g, wrh, wrl, br, batch, seq, tm):
    n, d = x1.shape
    nb = seq // tm
    mem = kv.shape[1]
    full = lambda a: pl.BlockSpec(a.shape, lambda r: (0,) * a.ndim)
    return pl.pallas_call(
        functools.partial(_xattn_kernel, tm=tm),
        grid=(n // tm,),
        in_specs=[pl.BlockSpec((tm, d), lambda r: (r, 0)),
                  pl.BlockSpec((None, mem, 2 * d), lambda r: (r // nb, 0, 0)),
                  pl.BlockSpec((tm, d), lambda r: (r, 0)),
                  full(wxo), full(g), full(wrh), full(wrl), full(br)],
        out_specs=[pl.BlockSpec((tm * (d // LANES), LANES), lambda r: (r, 0)),
                   pl.BlockSpec((8, tm), lambda r: (0, r)),
                   pl.BlockSpec((8, LANES), lambda r: (0, 0))],
        out_shape=[jax.ShapeDtypeStruct((n * (d // LANES), LANES), F32),
                   jax.ShapeDtypeStruct((8, n), I32),
                   jax.ShapeDtypeStruct((8, LANES), I32)],
        scratch_shapes=[pltpu.VMEM((8, LANES), F32)],
        compiler_params=_cparams("arbitrary"),
        name="xattn",
    )(xq, kv, x1, wxo, g, wrh, wrl, br)


def _plan_kernel(cnt_ref, route_ref, tos_ref, tg_ref, dest_vmem, dest_smem, sem, *, n_tok, tmm):
    n_slots = tos_ref.shape[0]
    n_tiles = tg_ref.shape[0]
    shift = tmm.bit_length() - 1
    assert tmm == 1 << shift

    def clear(s, _):
        tos_ref[s] = -1
        return 0

    off = jnp.int32(0)
    last_group = jnp.int32(0)
    starts, ends = [], []
    for g in range(N_GROUPS):
        cnt = cnt_ref[g]
        padded = lax.shift_left(lax.shift_right_logical(cnt + (tmm - 1), shift), shift)
        lax.fori_loop(off + cnt, off + padded, clear, 0)
        starts.append(off)
        off = off + padded
        ends.append(off)
        last_group = jnp.where(cnt > 0, g, last_group)
    lax.fori_loop(off, n_slots, clear, 0)

    gidx = route_ref[0:1, :]
    rank = route_ref[1:2, :]
    start = jnp.where(gidx == 0, starts[0], jnp.where(gidx == 1, starts[1],
                                                      jnp.where(gidx == 2, starts[2], starts[3])))
    dest_vmem[...] = start + rank
    to_smem = pltpu.make_async_copy(dest_vmem, dest_smem, sem)
    to_smem.start()
    to_smem.wait()

    def place(t, _):
        tos_ref[dest_smem[0, t]] = t
        return 0

    lax.fori_loop(0, n_tok, place, 0, unroll=8)

    def tile(k, _):
        start = k * tmm
        g = ((start >= ends[0]).astype(I32) + (start >= ends[1]).astype(I32) + (start >= ends[2]).astype(I32))
        tg_ref[k] = jnp.minimum(g, last_group)
        return 0

    lax.fori_loop(0, n_tiles, tile, 0)


def _route_plan(cnt, route, n_tok, tmm):
    n_tiles = n_tok // tmm + N_GROUPS
    smem = pl.BlockSpec(memory_space=pltpu.SMEM)
    return pl.pallas_call(
        functools.partial(_plan_kernel, n_tok=n_tok, tmm=tmm),
        in_specs=[smem, pl.BlockSpec(memory_space=pltpu.VMEM)],
        out_specs=[smem, smem],
        out_shape=[jax.ShapeDtypeStruct((n_tiles * tmm,), I32), jax.ShapeDtypeStruct((n_tiles,), I32)],
        scratch_shapes=[pltpu.VMEM((1, n_tok), I32), pltpu.SMEM((1, n_tok), I32), pltpu.SemaphoreType.DMA(())],
        name="route_plan",
    )(cnt, route)


def _moe_kernel(tos_ref, tg_ref, x2_hbm, g_ref, wrh_ref, wrl_ref, br_ref, wg_ref, wu_ref, wd_ref,
                x3_hbm, xb, ob, gsem, ssem, *, tmm, n_tok):
    k = pl.program_id(0)
    nk = pl.num_programs(0)
    slot = k % 2

    sub = xb.shape[1] // tmm

    def tile_rows(t):
        return pl.ds(pl.multiple_of(t * sub, sub), sub)

    def gather_copy(tile, sl, r):
        tok = jnp.maximum(tos_ref[tile * tmm + r], 0)
        return pltpu.make_async_copy(x2_hbm.at[tile_rows(tok)], xb.at[sl, tile_rows(r)], gsem.at[sl])

    def scatter_copy(tile, sl, r):
        tok = tos_ref[tile * tmm + r]
        dst = jnp.where(tok >= 0, tok, n_tok + sl * tmm + r)
        return pltpu.make_async_copy(ob.at[sl, tile_rows(r)], x3_hbm.at[tile_rows(dst)], ssem.at[sl])

    def for_rows(fn):
        def body(r8, _):
            for u in range(DMA_UNROLL):
                fn(r8 * DMA_UNROLL + u, u % 2)
            return 0
        lax.fori_loop(0, tmm // DMA_UNROLL, body, 0)

    @pl.when(k == 0)
    def _():
        for_rows(lambda r, pri: gather_copy(0, 0, r).start(priority=pri))

    for_rows(lambda r, pri: gather_copy(k, slot, r).wait())

    @pl.when(k + 1 < nk)
    def _():
        for_rows(lambda r, pri: gather_copy(k + 1, 1 - slot, r).start(priority=pri))

    @pl.when(k >= 2)
    def _():
        for_rows(lambda r, pri: scatter_copy(k - 2, slot, r).wait())

    x = _load_rows(xb.at[slot], tmm)
    t = _rms(x, g_ref[...])
    tb = t.astype(BF16)
    th, tl = _split2(t)
    logits = _dot(th, wrh_ref[...]) + _dot(th, wrl_ref[...]) + _dot(tl, wrh_ref[...]) + br_ref[...]
    grp = tg_ref[k]
    lane = lax.broadcasted_iota(I32, (1, LANES), 1)
    lanef = lane.astype(F32)
    rmax = lambda a: jnp.max(a, axis=-1, keepdims=True)
    rsum = lambda a: jnp.sum(a, axis=-1, keepdims=True)
    gmask = lane < N_GROUPS
    gmax = rmax(jnp.where(gmask, logits, -jnp.inf))
    zg = rsum(jnp.where(gmask, jnp.exp(logits - gmax), 0.0))
    p_group = jnp.exp(rsum(jnp.where(lane == grp, logits, 0.0)) - gmax) / zg
    e_lo = N_GROUPS + EXPERTS_PER_GROUP * grp
    emask = (lane >= e_lo) & (lane < e_lo + EXPERTS_PER_GROUP)
    em = jnp.where(emask, logits, -jnp.inf)
    e1 = rmax(em)
    i1 = jnp.min(jnp.where(em == e1, lanef, 1e9), axis=-1, keepdims=True)
    em2 = jnp.where(lanef == i1, -jnp.inf, em)
    e2 = rmax(em2)
    i2 = jnp.min(jnp.where(em2 == e2, lanef, 1e9), axis=-1, keepdims=True)
    ze = rsum(jnp.where(emask, jnp.exp(logits - e1), 0.0))
    p1 = 1.0 / ze
    p2 = jnp.exp(e2 - e1) / ze
    w1 = p_group * (p1 / (p1 + p2))
    w2 = p_group * (p2 / (p1 + p2))
    hid = _dot(tb, wg_ref[...])
    hid = hid * _sigmoid(hid) * _dot(tb, wu_ref[...])
    ff = hid.shape[1] // EXPERTS_PER_GROUP
    cw_lanes = jnp.where(lanef == i1, w1, 0.0) + jnp.where(lanef == i2, w2, 0.0)
    parts = []
    for e in range(EXPERTS_PER_GROUP):
        cw = rsum(jnp.where(lane == e_lo + e, cw_lanes, 0.0))
        parts.append((hid[:, e * ff:(e + 1) * ff] * cw).astype(BF16))
    y = x + _dot(jnp.concatenate(parts, axis=1), wd_ref[...])
    _store_rows(ob.at[slot], y)

    for_rows(lambda r, pri: scatter_copy(k, slot, r).start(priority=pri))

    @pl.when(k == nk - 1)
    def _():
        @pl.when(k >= 1)
        def _():
            for_rows(lambda r, pri: scatter_copy(k - 1, 1 - slot, r).wait())
        for_rows(lambda r, pri: scatter_copy(k, slot, r).wait())


def _moe(tos, tg, x2, g, wrh_t, wrl_t, br_row, wg, wu, wd, tmm):
    sub = g.shape[1] // LANES
    n_tok = x2.shape[0] // sub
    n_tiles = tg.shape[0]
    full = lambda a: pl.BlockSpec(a.shape, lambda k, tos, tg: (0,) * a.ndim)
    grp = lambda a: pl.BlockSpec((None,) + a.shape[1:], lambda k, tos, tg: (tg[k], 0, 0))
    grid_spec = pltpu.PrefetchScalarGridSpec(
        num_scalar_prefetch=2,
        grid=(n_tiles,),
        in_specs=[pl.BlockSpec(memory_space=pl.ANY), full(g), full(wrh_t), full(wrl_t), full(br_row),
                  grp(wg), grp(wu), grp(wd)],
        out_specs=pl.BlockSpec(memory_space=pl.ANY),
        scratch_shapes=[pltpu.VMEM((2, tmm * sub, LANES), F32), pltpu.VMEM((2, tmm * sub, LANES), F32),
                        pltpu.SemaphoreType.DMA((2,)), pltpu.SemaphoreType.DMA((2,))],
    )
    return pl.pallas_call(
        functools.partial(_moe_kernel, tmm=tmm, n_tok=n_tok),
        grid_spec=grid_spec,
        out_shape=jax.ShapeDtypeStruct(((n_tok + 2 * tmm) * sub, LANES), F32),
        compiler_params=_cparams("arbitrary"),
        name="moe",
    )(tos, tg, x2, g, wrh_t, wrl_t, br_row, wg, wu, wd)


def _final_norm_kernel(x_ref, g_ref, o_ref):
    o_ref[...] = _rms(_load_rows(x_ref, o_ref.shape[0]), g_ref[...])


def _final_norm(xs, g, n, tm):
    d = g.shape[1]
    return pl.pallas_call(
        _final_norm_kernel,
        grid=(n // tm,),
        in_specs=[_row_block(xs, tm, d), pl.BlockSpec((1, d), lambda r: (0, 0))],
        out_specs=pl.BlockSpec((tm, d), lambda r: (r, 0)),
        out_shape=jax.ShapeDtypeStruct((n, d), F32),
        compiler_params=_cparams("arbitrary"),
        name="final_norm",
    )(xs, g)


def _row(v):
    return v.reshape(1, -1)


def kernel(x, mem, mix_norm_g, w_in, fox_f_bias, fox_norm_g, hgrn_lb_logits, hgrn_norm_g, conv_w, conv_b,
           conv_norm_g, conv_norm_b, w_out, xatt_norm_g, mem_norm_g, w_xq, w_xkv, w_xo, ffn_norm_g,
           w_group, b_group, w_router, b_router, w_gate, w_up, w_down, final_norm_g):
    batch, seq, d = x.shape
    depth = w_in.shape[0]
    n = batch * seq
    tm = min(512, seq)
    tmm = min(512, seq)
    fblk = min(FOX_BLOCK, tm)
    fw = FOX_HEADS * FOX_HEAD_DIM
    hw = HGRN_HEADS * HGRN_DIM
    cw = conv_w.shape[-1]
    assert seq % tm == 0 and seq % HGRN_BLOCK == 0 and d == fw + hw + cw

    offs = [0]
    for width in (fw, fw, fw, FOX_HEADS, hw, hw, hw, hw, cw, cw):
        offs.append(offs[-1] + width)
    seg = lambda a, i, j: a[..., offs[i]:offs[j]]
    wqkv = jnp.concatenate([seg(w_in, 0, 1) * (FOX_HEAD_DIM ** -0.5 * LOG2E), seg(w_in, 1, 3)], axis=-1).astype(BF16)
    wff = jnp.swapaxes(seg(w_in, 3, 4), 1, 2).astype(BF16)
    wh = jnp.concatenate([seg(w_in, 4, 5), seg(w_in, 5, 6), seg(w_in, 6, 7), seg(w_in, 7, 8)], axis=-1).astype(BF16)
    wc = seg(w_in, 8, 10).astype(BF16)
    wo = w_out.astype(BF16)
    wxq = (w_xq * ((d // XATT_HEADS) ** -0.5)).astype(BF16)
    wxkv = w_xkv.astype(BF16)
    wxo = w_xo.astype(BF16)
    n_exp = N_GROUPS * EXPERTS_PER_GROUP
    wr = jnp.concatenate([w_group, w_router], axis=-1)
    wr_hi = wr.astype(BF16)
    wr_lo = (wr - wr_hi.astype(F32)).astype(BF16)
    pad_rows = lambda a: jnp.pad(jnp.swapaxes(a, 1, 2), ((0, 0), (0, ROUTER_ROWS - a.shape[2]), (0, 0)))
    pad_cols = lambda a: jnp.pad(a, ((0, 0), (0, 0), (0, LANES - a.shape[2])))
    br = jnp.concatenate([b_group, b_router], axis=-1)
    br_col = jnp.pad(br, ((0, 0), (0, ROUTER_ROWS - br.shape[1])))[:, :, None]
    br_row = jnp.pad(br, ((0, 0), (0, LANES - br.shape[1])))[:, None, :]
    ff = w_gate.shape[-1]
    to_cols = lambda w: jnp.transpose(w, (0, 1, 3, 2, 4)).reshape(depth, N_GROUPS, d, EXPERTS_PER_GROUP * ff).astype(BF16)
    wg, wu = to_cols(w_gate), to_cols(w_up)
    wd = w_down.reshape(depth, N_GROUPS, EXPERTS_PER_GROUP * ff, d).astype(BF16)
    conv_w_pad = jnp.pad(conv_w, ((0, 0), (0, CONV_HALO - CONV_WIDTH), (0, 0)))
    wrh_rows, wrl_rows, wrh_cols, wrl_cols = pad_rows(wr_hi), pad_rows(wr_lo), pad_cols(wr_hi), pad_cols(wr_lo)

    kv_all = _mem_kv(mem.reshape(-1, d), _row(mem_norm_g), wxkv).reshape(depth, batch, mem.shape[1], 2 * d)

    xs = x.reshape(n, d)
    for l in range(depth):
        qkv, hy, cy, lf, qn, kn = _mix_in(xs, _row(mix_norm_g[l]), wqkv[l], wh[l], wc[l], wff[l],
                                          fox_f_bias[l].reshape(-1, 1), batch, seq, tm)
        c, keep = _fox_plan(lf, qn, kn, fblk, tm)
        fo = _fox_attention(keep.reshape(-1), qkv, c, _row(fox_norm_g[l]), batch, seq, fblk, tm // fblk)
        ho = _hgrn(hy, hgrn_lb_logits, _row(hgrn_norm_g[l]), l, batch, seq, min(1024, seq))
        co = _conv(cy, conv_w_pad[l], _row(conv_b[l]), _row(conv_norm_g[l]), _row(conv_norm_b[l]), batch, seq, tm)
        x1, xq = _mix_out(fo, ho, co, xs, wo[l], _row(xatt_norm_g[l]), wxq[l], tm)
        x2, route, cnt = _xattn(xq, kv_all[l], x1, wxo[l], _row(ffn_norm_g[l]),
                                wrh_rows[l], wrl_rows[l], br_col[l], batch, seq, tm)
        tos, tg = _route_plan(cnt[:N_GROUPS, 0], route, n, tmm)
        xs = _moe(tos, tg, x2, _row(ffn_norm_g[l]), wrh_cols[l], wrl_cols[l], br_row[l],
                  wg[l], wu[l], wd[l], tmm)
    return _final_norm(xs, _row(final_norm_g), n, tm).reshape(batch, seq, d)
```

```python
import functools

import jax
import jax.numpy as jnp
from jax import lax
from jax.experimental import pallas as pl
from jax.experimental.pallas import tpu as pltpu

F32 = jnp.float32
BF16 = jnp.bfloat16
I32 = jnp.int32

EPS = 1e-6
LANES = 128
SUBLANES = 8
FOX_HEADS = 8
FOX_HEAD_DIM = 64
LOG2E = 1.4426950408889634
FOX_BLOCK = 512
FOX_SKIP_EXPONENT = -45.0
FOX_NORM_SLACK = 1.02
HGRN_HEADS = 4
HGRN_DIM = 64
HGRN_CHUNK = 16
HGRN_BLOCK = 128
CONV_WIDTH = 31
CONV_HALO = 32
CONV_GROUP = 64
XATT_HEADS = 4
N_GROUPS = 4
EXPERTS_PER_GROUP = 4
ROUTER_ROWS = 32
DMA_UNROLL = 8
VMEM_LIMIT_BYTES = 56 * 1024 * 1024

NT_DIMS = (((1,), (1,)), ((), ()))


def _cparams(*sem):
    return pltpu.CompilerParams(dimension_semantics=sem, vmem_limit_bytes=VMEM_LIMIT_BYTES)


def _dot(a, b):
    return jnp.dot(a, b, preferred_element_type=F32)


def _dot_nt(a, b):
    return lax.dot_general(a, b, NT_DIMS, preferred_element_type=F32)


def _rms(x, g):
    ms = jnp.mean(x * x, axis=-1, keepdims=True)
    return x * lax.rsqrt(ms + EPS) * g


def _sigmoid(x):
    return 1.0 / (1.0 + jnp.exp(-x))


def _split2(x):
    hi = x.astype(BF16)
    lo = (x - hi.astype(F32)).astype(BF16)
    return hi, lo


def _split3(x):
    hi = x.astype(BF16)
    r = x - hi.astype(F32)
    mid = r.astype(BF16)
    lo = (r - mid.astype(F32)).astype(BF16)
    return hi, mid, lo


def _halves_select(lane_lo_mask, a, b):
    return jnp.where(lane_lo_mask, a, b)


def _load_rows(ref, rows):
    if ref.shape[0] == rows:
        return ref[...]
    sub = ref.shape[0] // rows
    return jnp.concatenate([ref[pl.ds(c, rows, stride=sub), :] for c in range(sub)], axis=1)


def _store_rows(ref, val):
    rows = val.shape[0]
    if ref.shape[0] == rows:
        ref[...] = val
    else:
        sub = ref.shape[0] // rows
        for c in range(sub):
            ref[pl.ds(c, rows, stride=sub), :] = val[:, c * LANES:(c + 1) * LANES]


def _pair_rms(o, lo_mask, width):
    o2 = o * o
    s_all = jnp.sum(o2, axis=-1, keepdims=True)
    s_lo = jnp.sum(jnp.where(lo_mask, o2, 0.0), axis=-1, keepdims=True)
    ms = jnp.where(lo_mask, s_lo, s_all - s_lo) * (1.0 / width)
    return o * lax.rsqrt(ms + EPS)


def _mem_kv_kernel(mem_ref, g_ref, w_ref, o_ref):
    t = _rms(mem_ref[...], g_ref[...]).astype(BF16)
    o_ref[...] = _dot(t, w_ref[...]).astype(BF16)


def _mem_kv(mem2d, g, w_xkv):
    depth, d, d2 = w_xkv.shape
    rows = mem2d.shape[0]
    return pl.pallas_call(
        _mem_kv_kernel,
        grid=(depth,),
        in_specs=[pl.BlockSpec((rows, d), lambda l: (0, 0)),
                  pl.BlockSpec((1, d), lambda l: (0, 0)),
                  pl.BlockSpec((None, d, d2), lambda l: (l, 0, 0))],
        out_specs=pl.BlockSpec((None, rows, d2), lambda l: (l, 0, 0)),
        out_shape=jax.ShapeDtypeStruct((depth, rows, d2), BF16),
        compiler_params=_cparams("arbitrary"),
        name="mem_kv",
    )(mem2d, g, w_xkv)


def _head_norm_max(y, ind):
    n2 = _dot((y[:, :ind.shape[0]] * y[:, :ind.shape[0]]).astype(BF16), ind)
    top = jnp.broadcast_to(jnp.max(n2, axis=0, keepdims=True), (FOX_HEADS, LANES))
    row = lax.broadcasted_iota(I32, (FOX_HEADS, LANES), 0)
    lane = lax.broadcasted_iota(I32, (FOX_HEADS, LANES), 1)
    pick = lambda first: jnp.broadcast_to(
        jnp.sqrt(jnp.sum(jnp.where(lane == row + first, top, 0.0), axis=1, keepdims=True)), (FOX_HEADS, LANES))
    return pick(0), pick(FOX_HEADS)


def _mix_in_kernel(x_ref, g_ref, wqkv_ref, wh_ref, wc_ref, wff_ref, fb_ref, ind_ref,
                   qkv_ref, hy_ref, cy_ref, lf_ref, qn_ref, kn_ref, kn_scr, *, nb):
    t = _rms(_load_rows(x_ref, qkv_ref.shape[0]), g_ref[...]).astype(BF16)
    y = _dot(t, wqkv_ref[...])
    qkv_ref[...] = y.astype(BF16)
    hy_ref[...] = _dot(t, wh_ref[...]).astype(BF16)
    cy_ref[...] = _dot(t, wc_ref[...]).astype(BF16)
    z = _dot_nt(wff_ref[...], t) + fb_ref[...]
    lf_ref[...] = (jnp.minimum(z, 0.0) - jnp.log1p(jnp.exp(-jnp.abs(z)))) * LOG2E
    q_norm, k_norm = _head_norm_max(y, ind_ref[...])
    qn_ref[...] = q_norm

    @pl.when(pl.program_id(0) % nb == 0)
    def _():
        kn_scr[...] = jnp.zeros_like(kn_scr)

    kn_scr[...] = jnp.maximum(kn_scr[...], k_norm)
    kn_ref[...] = kn_scr[...]


def _row_block(a, tm, d):
    sub = 1 if a.shape[1] == d else d // a.shape[1]
    return pl.BlockSpec((tm * sub, a.shape[1]), lambda r: (r, 0))


def _mix_in(x2d, g, wqkv, wh, wc, wff, fb, batch, seq, tm):
    n = batch * seq
    nb = seq // tm
    full = lambda a: pl.BlockSpec(a.shape, lambda r: (0,) * a.ndim)
    qk_cols = 2 * FOX_HEADS * FOX_HEAD_DIM
    ind = (jnp.arange(qk_cols)[:, None] // FOX_HEAD_DIM == jnp.arange(LANES)[None, :]).astype(BF16)
    return pl.pallas_call(
        functools.partial(_mix_in_kernel, nb=nb),
        grid=(n // tm,),
        in_specs=[_row_block(x2d, tm, wqkv.shape[0]), full(g), full(wqkv), full(wh), full(wc),
                  full(wff), full(fb), full(ind)],
        out_specs=[pl.BlockSpec((tm, wqkv.shape[1]), lambda r: (r, 0)),
                   pl.BlockSpec((tm, wh.shape[1]), lambda r: (r, 0)),
                   pl.BlockSpec((tm, wc.shape[1]), lambda r: (r, 0)),
                   pl.BlockSpec((None, FOX_HEADS, tm), lambda r: (r // nb, 0, r % nb)),
                   pl.BlockSpec((None, FOX_HEADS, LANES), lambda r: (r, 0, 0)),
                   pl.BlockSpec((None, FOX_HEADS, LANES), lambda r: (r // nb, 0, 0))],
        out_shape=[jax.ShapeDtypeStruct((n, wqkv.shape[1]), BF16),
                   jax.ShapeDtypeStruct((n, wh.shape[1]), BF16),
                   jax.ShapeDtypeStruct((n, wc.shape[1]), BF16),
                   jax.ShapeDtypeStruct((batch, FOX_HEADS, seq), F32),
                   jax.ShapeDtypeStruct((n // tm, FOX_HEADS, LANES), F32),
                   jax.ShapeDtypeStruct((batch, FOX_HEADS, LANES), F32)],
        scratch_shapes=[pltpu.VMEM((FOX_HEADS, LANES), F32)],
        compiler_params=_cparams("arbitrary"),
        name="mix_in",
    )(x2d, g, wqkv, wh, wc, wff, fb, ind)


def _fox_plan_kernel(lf_ref, qn_ref, kn_ref, c_ref, keep_ref, *, blk, stat_rows):
    seq = lf_ref.shape[-1]
    nblk = seq // blk
    r = lax.broadcasted_iota(I32, (blk, blk), 0)
    c = lax.broadcasted_iota(I32, (blk, blk), 1)
    upper = jnp.where(r <= c, 1.0, 0.0).astype(BF16)
    lane = lax.broadcasted_iota(I32, (FOX_HEADS, LANES), 1)
    carry = jnp.zeros((FOX_HEADS, 1), F32)
    first = jnp.zeros((FOX_HEADS, LANES), F32)
    last = jnp.zeros((FOX_HEADS, LANES), F32)
    qn = jnp.zeros((FOX_HEADS, LANES), F32)
    for b in range(nblk):
        x = lf_ref[:, b * blk:(b + 1) * blk]
        hi, mid, lo = _split3(x)
        cb = _dot(hi, upper) + _dot(mid, upper) + _dot(lo, upper) + carry
        for h in range(FOX_HEADS):
            c_ref[h, :, b * blk:(b + 1) * blk] = cb[h:h + 1, :]
        carry = cb[:, blk - 1:blk]
        first = jnp.where(lane == b, cb[:, 0:1], first)
        last = jnp.where(lane == b, carry, last)
        qn = jnp.where(lane == b, qn_ref[b * blk // stat_rows], qn)
    bound = FOX_NORM_SLACK * 2.0 * qn * kn_ref[...] + first
    keep = jnp.zeros((FOX_HEADS, LANES), I32)
    for i in range(nblk):
        live = (bound[:, i:i + 1] - last >= FOX_SKIP_EXPONENT * LOG2E) & (lane < i)
        keep = jnp.where(lane == i, jnp.sum(live.astype(I32), axis=1, keepdims=True), keep)
    keep_ref[...] = keep


def _fox_plan(lf, qn, kn, blk, stat_rows):
    batch, heads, seq = lf.shape
    assert seq // blk <= LANES
    return pl.pallas_call(
        functools.partial(_fox_plan_kernel, blk=blk, stat_rows=stat_rows),
        grid=(batch,),
        in_specs=[pl.BlockSpec((None, heads, seq), lambda b: (b, 0, 0)),
                  pl.BlockSpec((seq // stat_rows, heads, LANES), lambda b: (b, 0, 0)),
                  pl.BlockSpec((None, heads, LANES), lambda b: (b, 0, 0))],
        out_specs=[pl.BlockSpec((None, heads, 1, seq), lambda b: (b, 0, 0, 0)),
                   pl.BlockSpec((None, heads, LANES), lambda b: (b, 0, 0))],
        out_shape=[jax.ShapeDtypeStruct((batch, heads, 1, seq), F32),
                   jax.ShapeDtypeStruct((batch, heads, LANES), I32)],
        compiler_params=_cparams("arbitrary"),
        name="fox_plan",
    )(lf, qn, kn)


def _fox_kernel(keep_ref, q_ref, k_ref, v_ref, c_ref, g_ref, o_ref, *, tq, qpb):
    head0 = (pl.program_id(0) * FOX_HEADS + 2 * pl.program_id(1)) * LANES
    lane = lax.broadcasted_iota(I32, (1, LANES), 1)
    lo_mask = lane < FOX_HEAD_DIM
    row = lax.broadcasted_iota(I32, (tq, tq), 0)
    col = lax.broadcasted_iota(I32, (tq, tq), 1)
    causal = row >= col

    for qb in range(qpb):
        i = pl.program_id(2) * qpb + qb
        n_prev = jnp.maximum(keep_ref[head0 + i], keep_ref[head0 + LANES + i])
        q = q_ref[qb * tq:(qb + 1) * tq, :]
        zero = jnp.zeros_like(q)
        qh = (jnp.where(lo_mask, q, zero), jnp.where(lo_mask, zero, q))
        q0 = pl.multiple_of(i * tq, tq)
        cq0 = tuple(c_ref[h, :, pl.ds(q0, LANES)][:, 0:1] for h in range(2))

        def block(j, carry, masked, qh=qh, cq0=cq0):
            k0 = pl.multiple_of(j * tq, tq)
            kb = k_ref[pl.ds(k0, tq), :]
            vb = v_ref[pl.ds(k0, tq), :]
            s = [_dot_nt(qh[h], kb) + (cq0[h] - c_ref[h, :, pl.ds(k0, tq)]) for h in range(2)]
            out = []
            for h in range(2):
                m, l, acc = carry[h]
                sh = jnp.where(causal, s[h], -jnp.inf) if masked else s[h]
                m_new = jnp.maximum(m, jnp.max(sh, axis=-1, keepdims=True))
                alpha = jnp.exp2(m - m_new)
                p = jnp.exp2(sh - m_new)
                l = alpha * l + jnp.sum(p, axis=-1, keepdims=True)
                acc = alpha * acc + _dot(p.astype(BF16), vb)
                out.append((m_new, l, acc))
            return tuple(out)

        init = tuple((jnp.full((tq, 1), -jnp.inf, F32), jnp.zeros((tq, 1), F32), jnp.zeros((tq, LANES), F32))
                     for _ in range(2))
        carry = block(i, init, True)
        carry = lax.fori_loop(i - n_prev, i, lambda j, c, block=block: block(j, c, False), carry)
        (_, l0, a0), (_, l1, a1) = carry
        o = jnp.where(lo_mask, a0 * (1.0 / l0), a1 * (1.0 / l1))
        o_ref[qb * tq:(qb + 1) * tq, :] = (_pair_rms(o, lo_mask, FOX_HEAD_DIM) * g_ref[...]).astype(o_ref.dtype)


def _fox_attention(keep, qkv, c, gain, batch, seq, tq, qpb):
    n = qkv.shape[0]
    pairs = FOX_HEADS // 2
    rows = tq * qpb
    nq = seq // rows
    grid_spec = pltpu.PrefetchScalarGridSpec(
        num_scalar_prefetch=1,
        grid=(batch, pairs, nq),
        in_specs=[pl.BlockSpec((rows, LANES), lambda b, p, i, keep: (b * nq + i, p)),
                  pl.BlockSpec((seq, LANES), lambda b, p, i, keep: (b, pairs + p)),
                  pl.BlockSpec((seq, LANES), lambda b, p, i, keep: (b, 2 * pairs + p)),
                  pl.BlockSpec((None, 2, 1, seq), lambda b, p, i, keep: (b, p, 0, 0)),
                  pl.BlockSpec((1, LANES), lambda b, p, i, keep: (0, p))],
        out_specs=pl.BlockSpec((rows, LANES), lambda b, p, i, keep: (b * nq + i, p)),
    )
    return pl.pallas_call(
        functools.partial(_fox_kernel, tq=tq, qpb=qpb),
        grid_spec=grid_spec,
        out_shape=jax.ShapeDtypeStruct((n, pairs * LANES), BF16),
        compiler_params=_cparams("parallel", "parallel", "arbitrary"),
        name="fox_attention",
    )(keep, qkv, qkv, qkv, c, gain)


def _hgrn_kernel(q_ref, f_ref, v_ref, gate_ref, lbz_ref, gain_ref, o_ref, st_ref, u_scr, prev_scr, *, layer, n_sub):
    T, C = HGRN_BLOCK, HGRN_CHUNK
    nchunk = T // C
    i = pl.program_id(2)

    @pl.when(i == 0)
    def _():
        st_ref[...] = jnp.zeros_like(st_ref)

    z = lbz_ref[...]
    e = jnp.exp(z - jnp.max(z, axis=0, keepdims=True))
    pz = e / jnp.sum(e, axis=0, keepdims=True)
    lb = jnp.zeros((1, LANES), F32)
    for j in range(1, layer + 1):
        lb = lb + pz[j:j + 1, :]

    lane = lax.broadcasted_iota(I32, (1, LANES), 1)
    lo_mask = lane < HGRN_DIM
    r = lax.broadcasted_iota(I32, (T, T), 0)
    c = lax.broadcasted_iota(I32, (T, T), 1)
    same_chunk = (r // C) == (c // C)
    one = lambda m: jnp.where(m, 1.0, 0.0).astype(BF16)
    scan_mat = jnp.concatenate([one(same_chunk & (c <= r)),
                                one(same_chunk & ((c % C) <= C // 2)),
                                one(same_chunk)], axis=0)
    intra_mask = same_chunk & (c <= r)
    chunk_of_row = lax.broadcasted_iota(I32, (T, LANES), 0) // C
    vr = lax.broadcasted_iota(I32, (LANES, nchunk * LANES), 0)
    kc = lax.broadcasted_iota(I32, (LANES, nchunk * LANES), 1)
    same_head = (vr < HGRN_DIM) == ((kc % LANES) < HGRN_DIM)

    subs = range(n_sub)
    rows = [pl.ds(sb * T, T) for sb in subs]
    q = [q_ref[r_, :].astype(F32) for r_ in rows]
    f = [lb + (1.0 - lb) * _sigmoid(f_ref[r_, :].astype(F32)) for r_ in rows]
    kk = [1.0 - f_ for f_ in f]
    parts = []
    for f_ in f:
        parts.extend(_split2(jnp.log(f_)))
    sc = _dot(scan_mat, jnp.concatenate(parts, axis=1))
    sc = [sc[:, (2 * sb) * LANES:(2 * sb + 1) * LANES] + sc[:, (2 * sb + 1) * LANES:(2 * sb + 2) * LANES]
          for sb in subs]
    b = [s_[:T] for s_ in sc]
    b_mid = [s_[T:2 * T] for s_ in sc]
    b_last = [s_[2 * T:] for s_ in sc]
    v = [v_ref[r_, :] for r_ in rows]
    for sb in subs:
        k_out = kk[sb] * jnp.exp(b_last[sb] - b[sb])
        k_exp = jnp.concatenate([jnp.where(chunk_of_row == j, k_out, 0.0) for j in range(nchunk)],
                                axis=1).astype(BF16)
        v_t = v[sb].astype(F32).T.astype(BF16)
        u_scr[sb] = jnp.where(same_head, _dot(v_t, k_exp), 0.0)
    att = []
    for sb in subs:
        q_in = (q[sb] * jnp.exp(b[sb] - b_mid[sb])).astype(BF16)
        k_in = (kk[sb] * jnp.exp(b_mid[sb] - b[sb])).astype(BF16)
        zq = jnp.zeros_like(q_in)
        att.append([jnp.where(intra_mask, _dot_nt(qm, k_in), 0.0).astype(BF16)
                    for qm in (jnp.where(lo_mask, q_in, zq), jnp.where(lo_mask, zq, q_in))])
    o = [jnp.where(lo_mask, _dot(att[sb][0], v[sb]), _dot(att[sb][1], v[sb])) for sb in subs]
    state = st_ref[...]
    for sb in subs:
        decay = jnp.exp(b_last[sb])
        for j in range(nchunk):
            prev_scr[sb, j * LANES:(j + 1) * LANES, :] = state.astype(BF16)
            state = state * decay[j * C:j * C + 1, :] + u_scr[sb, :, j * LANES:(j + 1) * LANES]
    st_ref[...] = state
    for sb in subs:
        q_out = (q[sb] * jnp.exp(b[sb])).astype(BF16)
        o_all = _dot_nt(q_out, prev_scr[sb])
        acc = o[sb]
        for j in range(nchunk):
            acc = acc + jnp.where(chunk_of_row == j, o_all[:, j * LANES:(j + 1) * LANES], 0.0)
        gate = gate_ref[rows[sb], :].astype(F32)
        y = _pair_rms(acc, lo_mask, HGRN_DIM) * gain_ref[...] * (gate * _sigmoid(gate))
        o_ref[rows[sb], :] = y.astype(o_ref.dtype)


def _hgrn(hy, lb_logits, gain, layer, batch, seq, tg):
    n = hy.shape[0]
    pairs = HGRN_HEADS // 2
    nb = seq // tg
    depth = lb_logits.shape[0]
    col = lambda k: pl.BlockSpec((tg, LANES), lambda b, p, i, k=k: (b * nb + i, k * pairs + p))
    n_sub = tg // HGRN_BLOCK
    states = (HGRN_BLOCK // HGRN_CHUNK) * LANES
    return pl.pallas_call(
        functools.partial(_hgrn_kernel, layer=layer, n_sub=n_sub),
        grid=(batch, pairs, nb),
        in_specs=[col(0), col(1), col(2), col(3),
                  pl.BlockSpec((depth, LANES), lambda b, p, i: (0, p)),
                  pl.BlockSpec((1, LANES), lambda b, p, i: (0, p))],
        out_specs=pl.BlockSpec((tg, LANES), lambda b, p, i: (b * nb + i, p)),
        out_shape=jax.ShapeDtypeStruct((n, pairs * LANES), BF16),
        scratch_shapes=[pltpu.VMEM((LANES, LANES), F32),
                        pltpu.VMEM((n_sub, LANES, states), F32),
                        pltpu.VMEM((n_sub, states, LANES), BF16)],
        compiler_params=_cparams("parallel", "parallel", "arbitrary"),
        name="hgrn",
    )(hy, hy, hy, hy, lb_logits, gain)


def _conv_kernel(cu_ref, cg_ref, w_ref, b_ref, ng_ref, nb_ref, o_ref, a_scr, sh_scr, *, tm):
    i = pl.program_id(1)
    sub = SUBLANES

    @pl.when(i == 0)
    def _():
        a_scr[0:CONV_HALO, :] = jnp.zeros((CONV_HALO, a_scr.shape[1]), F32)

    @pl.when(i > 0)
    def _():
        a_scr[0:CONV_HALO, :] = a_scr[tm:tm + CONV_HALO, :]

    a_scr[CONV_HALO:CONV_HALO + tm, :] = cu_ref[...].astype(F32) * _sigmoid(cg_ref[...].astype(F32))
    span = CONV_HALO + tm - sub
    for k in range(1, sub):
        sh_scr[k - 1, 0:span, :] = a_scr[k:k + span, :]
    acc = jnp.zeros((tm, a_scr.shape[1]), F32) + b_ref[...]
    first = CONV_HALO - (CONV_WIDTH - 1)
    for w in range(CONV_WIDTH):
        base, k = divmod(first + w, sub)
        src = a_scr if k == 0 else sh_scr.at[k - 1]
        acc = acc + src[base * sub:base * sub + tm, :] * w_ref[w:w + 1, :]
    lane = lax.broadcasted_iota(I32, (1, LANES), 1)
    lo_mask = lane < CONV_GROUP
    halves = []
    for hh in range(acc.shape[1] // LANES):
        xh = acc[:, hh * LANES:(hh + 1) * LANES]
        s_all = jnp.sum(xh, axis=-1, keepdims=True)
        s_lo = jnp.sum(jnp.where(lo_mask, xh, 0.0), axis=-1, keepdims=True)
        d = xh - jnp.where(lo_mask, s_lo, s_all - s_lo) * (1.0 / CONV_GROUP)
        halves.append(_pair_rms(d, lo_mask, CONV_GROUP))
    y = jnp.concatenate(halves, axis=1) * ng_ref[...] + nb_ref[...]
    o_ref[...] = (y * _sigmoid(y)).astype(o_ref.dtype)


def _conv(cy, w, b, ng, nb_, batch, seq, tm):
    n = cy.shape[0]
    ch = cy.shape[1] // 2
    nb = seq // tm
    full = lambda a: pl.BlockSpec(a.shape, lambda bb, i: (0,) * a.ndim)
    return pl.pallas_call(
        functools.partial(_conv_kernel, tm=tm),
        grid=(batch, nb),
        in_specs=[pl.BlockSpec((tm, ch), lambda bb, i: (bb * nb + i, 0)),
                  pl.BlockSpec((tm, ch), lambda bb, i: (bb * nb + i, 1)),
                  full(w), full(b), full(ng), full(nb_)],
        out_specs=pl.BlockSpec((tm, ch), lambda bb, i: (bb * nb + i, 0)),
        out_shape=jax.ShapeDtypeStruct((n, ch), BF16),
        scratch_shapes=[pltpu.VMEM((CONV_HALO + tm, ch), F32),
                        pltpu.VMEM((SUBLANES - 1, CONV_HALO + tm, ch), F32)],
        compiler_params=_cparams("parallel", "arbitrary"),
        name="conv",
    )(cy, cy, w, b, ng, nb_)


def _router_logits_t(t, wr_hi, wr_lo, br):
    th, tl = _split2(t)
    return _dot_nt(wr_hi, th) + _dot_nt(wr_lo, th) + _dot_nt(wr_hi, tl) + br


def _xattn_kernel(fo_ref, ho_ref, co_ref, x_ref, wo_ref, gx_ref, wq_ref, kv_ref, wxo_ref, g_ref,
                  wrh_ref, wrl_ref, br_ref, x2_ref, route_ref, cnt_ref, cnt_scr, *, tm):
    step = pl.program_id(0)

    @pl.when(step == 0)
    def _():
        cnt_scr[...] = jnp.zeros_like(cnt_scr)

    nf, nh = fo_ref.shape[1], ho_ref.shape[1]
    x1 = (_load_rows(x_ref, tm) + _dot(fo_ref[...], wo_ref[0:nf, :]) + _dot(ho_ref[...], wo_ref[nf:nf + nh, :])
          + _dot(co_ref[...], wo_ref[nf + nh:, :]))
    xq = _dot(_rms(x1, gx_ref[...]).astype(BF16), wq_ref[...]).astype(BF16)
    d = xq.shape[1]
    hd = d // XATT_HEADS
    scores = [_dot_nt(xq[:, h * hd:(h + 1) * hd], kv_ref[:, h * hd:(h + 1) * hd]) for h in range(XATT_HEADS)]
    probs = [jnp.exp(s - jnp.max(s, axis=-1, keepdims=True)) for s in scores]
    heads = []
    for h, p in enumerate(probs):
        vh = kv_ref[:, d + h * hd:d + (h + 1) * hd]
        l = jnp.sum(p, axis=-1, keepdims=True)
        heads.append((_dot(p.astype(BF16), vh) * (1.0 / l)).astype(BF16))
    x2 = x1 + _dot(jnp.concatenate(heads, axis=1), wxo_ref[...])
    _store_rows(x2_ref, x2)
    logits = _router_logits_t(_rms(x2, g_ref[...]), wrh_ref[...], wrl_ref[...], br_ref[...])
    gl = [logits[g:g + 1, :] for g in range(N_GROUPS)]
    gmax = jnp.maximum(jnp.maximum(gl[0], gl[1]), jnp.maximum(gl[2], gl[3]))
    gidx = jnp.where(gl[0] >= gmax, 0, jnp.where(gl[1] >= gmax, 1, jnp.where(gl[2] >= gmax, 2, 3)))
    rows8 = lax.broadcasted_iota(I32, (8, tm), 0)
    onehot = jnp.where(rows8 == gidx, 1.0, 0.0)
    r = lax.broadcasted_iota(I32, (tm, tm), 0)
    c = lax.broadcasted_iota(I32, (tm, tm), 1)
    incl = _dot(onehot.astype(BF16), jnp.where(r <= c, 1.0, 0.0).astype(BF16))
    before = cnt_scr[:, 0:1]
    rank = jnp.sum(onehot * (incl - 1.0 + before), axis=0, keepdims=True)
    route_ref[...] = jnp.where(rows8 == 0, gidx, jnp.where(rows8 == 1, rank.astype(I32), 0))
    cnt_scr[...] = cnt_scr[...] + incl[:, tm - 1:tm]
    cnt_ref[...] = cnt_scr[...].astype(I32)


def _xattn(fo, ho, co, x2d, wo, gx, wq, kv, wxo, g, wrh, wrl, br, batch, seq, tm):
    n, d = fo.shape[0], wo.shape[1]
    nb = seq // tm
    mem = kv.shape[1]
    full = lambda a: pl.BlockSpec(a.shape, lambda r: (0,) * a.ndim)
    rowblk = lambda a: _row_block(a, tm, d if a is x2d else a.shape[1])
    return pl.pallas_call(
        functools.partial(_xattn_kernel, tm=tm),
        grid=(n // tm,),
        in_specs=[rowblk(fo), rowblk(ho), rowblk(co), rowblk(x2d), full(wo), full(gx), full(wq),
                  pl.BlockSpec((None, mem, 2 * d), lambda r: (r // nb, 0, 0)),
                  full(wxo), full(g), full(wrh), full(wrl), full(br)],
        out_specs=[pl.BlockSpec((tm * (d // LANES), LANES), lambda r: (r, 0)),
                   pl.BlockSpec((8, tm), lambda r: (0, r)),
                   pl.BlockSpec((8, LANES), lambda r: (0, 0))],
        out_shape=[jax.ShapeDtypeStruct((n * (d // LANES), LANES), F32),
                   jax.ShapeDtypeStruct((8, n), I32),
                   jax.ShapeDtypeStruct((8, LANES), I32)],
        scratch_shapes=[pltpu.VMEM((8, LANES), F32)],
        compiler_params=_cparams("arbitrary"),
        name="xattn",
    )(fo, ho, co, x2d, wo, gx, wq, kv, wxo, g, wrh, wrl, br)


def _plan_kernel(cnt_ref, route_ref, src_ref, dst_ref, tg_ref, dest_vmem, dest_smem, sem, *, n_tok, tmm, sub):
    n_slots = src_ref.shape[0]
    n_tiles = tg_ref.shape[0] - 1
    shift = tmm.bit_length() - 1
    assert tmm == 1 << shift

    def clear(s, _):
        src_ref[s] = 0
        parity = lax.shift_right_logical(s, shift) & 1
        dst_ref[s] = (n_tok + parity * tmm + (s & (tmm - 1))) * sub
        return 0

    off = jnp.int32(0)
    last_group = jnp.int32(0)
    starts, ends = [], []
    for g in range(N_GROUPS):
        cnt = cnt_ref[g]
        padded = lax.shift_left(lax.shift_right_logical(cnt + (tmm - 1), shift), shift)
        lax.fori_loop(off + cnt, off + padded, clear, 0)
        starts.append(off)
        off = off + padded
        ends.append(off)
        last_group = jnp.where(cnt > 0, g, last_group)
    lax.fori_loop(off, n_slots, clear, 0)

    gidx = route_ref[0:1, :]
    rank = route_ref[1:2, :]
    start = jnp.where(gidx == 0, starts[0], jnp.where(gidx == 1, starts[1],
                                                      jnp.where(gidx == 2, starts[2], starts[3])))
    dest_vmem[...] = start + rank
    to_smem = pltpu.make_async_copy(dest_vmem, dest_smem, sem)
    to_smem.start()
    to_smem.wait()

    def place(t, _):
        slot = dest_smem[0, t]
        src_ref[slot] = t * sub
        dst_ref[slot] = t * sub
        return 0

    lax.fori_loop(0, n_tok, place, 0, unroll=8)

    def tile(k, _):
        start = k * tmm
        g = ((start >= ends[0]).astype(I32) + (start >= ends[1]).astype(I32) + (start >= ends[2]).astype(I32))
        tg_ref[k] = jnp.minimum(g, last_group)
        return 0

    lax.fori_loop(0, n_tiles, tile, 0)
    tg_ref[n_tiles] = lax.shift_right_logical(off, shift)


def _route_plan(cnt, route, n_tok, tmm, sub):
    n_tiles = n_tok // tmm + N_GROUPS
    smem = pl.BlockSpec(memory_space=pltpu.SMEM)
    slots = jax.ShapeDtypeStruct((n_tiles * tmm,), I32)
    return pl.pallas_call(
        functools.partial(_plan_kernel, n_tok=n_tok, tmm=tmm, sub=sub),
        in_specs=[smem, pl.BlockSpec(memory_space=pltpu.VMEM)],
        out_specs=[smem, smem, smem],
        out_shape=[slots, slots, jax.ShapeDtypeStruct((n_tiles + 1,), I32)],
        scratch_shapes=[pltpu.VMEM((1, n_tok), I32), pltpu.SMEM((1, n_tok), I32), pltpu.SemaphoreType.DMA(())],
        name="route_plan",
    )(cnt, route)


def _moe_kernel(src_ref, dst_ref, tg_ref, x2_hbm, g_ref, wrh_ref, wrl_ref, br_ref, wg_ref, wu_ref, wd_ref,
                x3_hbm, xb, ob, gsem, ssem, zsem, *, tmm, n_tok):
    k = pl.program_id(0)
    n_active = tg_ref[pl.num_programs(0)]
    slot = k % 2

    sub = xb.shape[1] // tmm

    def rows_at(first):
        return pl.ds(pl.multiple_of(first, sub), sub)

    def gather_copy(tile, sl, r):
        return pltpu.make_async_copy(x2_hbm.at[rows_at(src_ref[tile * tmm + r])], xb.at[sl, rows_at(r * sub)],
                                     gsem.at[sl])

    def scatter_copy(tile, sl, r):
        return pltpu.make_async_copy(ob.at[sl, rows_at(r * sub)], x3_hbm.at[rows_at(dst_ref[tile * tmm + r])],
                                     ssem.at[sl])

    def for_rows(fn):
        def body(r8, _):
            for u in range(DMA_UNROLL):
                fn(r8 * DMA_UNROLL + u, u % 2)
            return 0
        lax.fori_loop(0, tmm // DMA_UNROLL, body, 0)

    def dump_fill(half):
        return pltpu.make_async_copy(ob.at[1], x3_hbm.at[pl.ds((n_tok + half * tmm) * sub, tmm * sub)], zsem)

    @pl.when(k == 0)
    def _():
        ob[1] = jnp.zeros(ob.shape[1:], F32)
        dump_fill(0).start()
        dump_fill(1).start()
        for_rows(lambda r, pri: gather_copy(0, 0, r).start(priority=pri))

    @pl.when(k < n_active)
    def _():
        for_rows(lambda r, pri: gather_copy(k, slot, r).wait())

        @pl.when(k + 1 < n_active)
        def _():
            for_rows(lambda r, pri: gather_copy(k + 1, 1 - slot, r).start(priority=pri))

        @pl.when(k >= 2)
        def _():
            for_rows(lambda r, pri: scatter_copy(k - 2, slot, r).wait())

        y = _moe_tile(_load_rows(xb.at[slot], tmm), tg_ref[k], g_ref, wrh_ref, wrl_ref, br_ref,
                      wg_ref, wu_ref, wd_ref)
        _store_rows(ob.at[slot], y)

        @pl.when(k == 0)
        def _():
            dump_fill(0).wait()
            dump_fill(1).wait()

        for_rows(lambda r, pri: scatter_copy(k, slot, r).start(priority=pri))

        @pl.when(k == n_active - 1)
        def _():
            @pl.when(k >= 1)
            def _():
                for_rows(lambda r, pri: scatter_copy(k - 1, 1 - slot, r).wait())
            for_rows(lambda r, pri: scatter_copy(k, slot, r).wait())


def _moe_tile(x, grp, g_ref, wrh_ref, wrl_ref, br_ref, wg_ref, wu_ref, wd_ref):
    t = _rms(x, g_ref[...])
    tb = t.astype(BF16)
    th, tl = _split2(t)
    logits = _dot(th, wrh_ref[...]) + _dot(th, wrl_ref[...]) + _dot(tl, wrh_ref[...]) + br_ref[...]
    lane = lax.broadcasted_iota(I32, (1, LANES), 1)
    lanef = lane.astype(F32)
    rmax = lambda a: jnp.max(a, axis=-1, keepdims=True)
    rsum = lambda a: jnp.sum(a, axis=-1, keepdims=True)
    gmask = lane < N_GROUPS
    gmax = rmax(jnp.where(gmask, logits, -jnp.inf))
    zg = rsum(jnp.where(gmask, jnp.exp(logits - gmax), 0.0))
    p_group = jnp.exp(rsum(jnp.where(lane == grp, logits, 0.0)) - gmax) / zg
    e_lo = N_GROUPS + EXPERTS_PER_GROUP * grp
    emask = (lane >= e_lo) & (lane < e_lo + EXPERTS_PER_GROUP)
    em = jnp.where(emask, logits, -jnp.inf)
    e1 = rmax(em)
    i1 = jnp.min(jnp.where(em == e1, lanef, 1e9), axis=-1, keepdims=True)
    em2 = jnp.where(lanef == i1, -jnp.inf, em)
    e2 = rmax(em2)
    i2 = jnp.min(jnp.where(em2 == e2, lanef, 1e9), axis=-1, keepdims=True)
    ze = rsum(jnp.where(emask, jnp.exp(logits - e1), 0.0))
    p1 = 1.0 / ze
    p2 = jnp.exp(e2 - e1) / ze
    w1 = p_group * (p1 / (p1 + p2))
    w2 = p_group * (p2 / (p1 + p2))
    hid = _dot(tb, wg_ref[...])
    hid = hid * _sigmoid(hid) * _dot(tb, wu_ref[...])
    ff = hid.shape[1] // EXPERTS_PER_GROUP
    cw_lanes = jnp.where(lanef == i1, w1, 0.0) + jnp.where(lanef == i2, w2, 0.0)
    parts = []
    for e in range(EXPERTS_PER_GROUP):
        cw = rsum(jnp.where(lane == e_lo + e, cw_lanes, 0.0))
        parts.append((hid[:, e * ff:(e + 1) * ff] * cw).astype(BF16))
    return x + _dot(jnp.concatenate(parts, axis=1), wd_ref[...])


def _moe(src, dst, tg, x2, g, wrh_t, wrl_t, br_row, wg, wu, wd, tmm):
    sub = g.shape[1] // LANES
    n_tok = x2.shape[0] // sub
    n_tiles = tg.shape[0] - 1
    full = lambda a: pl.BlockSpec(a.shape, lambda k, src, dst, tg: (0,) * a.ndim)
    grp = lambda a: pl.BlockSpec((None,) + a.shape[1:], lambda k, src, dst, tg: (tg[k], 0, 0))
    grid_spec = pltpu.PrefetchScalarGridSpec(
        num_scalar_prefetch=3,
        grid=(n_tiles,),
        in_specs=[pl.BlockSpec(memory_space=pl.ANY), full(g), full(wrh_t), full(wrl_t), full(br_row),
                  grp(wg), grp(wu), grp(wd)],
        out_specs=pl.BlockSpec(memory_space=pl.ANY),
        scratch_shapes=[pltpu.VMEM((2, tmm * sub, LANES), F32), pltpu.VMEM((2, tmm * sub, LANES), F32),
                        pltpu.SemaphoreType.DMA((2,)), pltpu.SemaphoreType.DMA((2,)), pltpu.SemaphoreType.DMA(())],
    )
    return pl.pallas_call(
        functools.partial(_moe_kernel, tmm=tmm, n_tok=n_tok),
        grid_spec=grid_spec,
        out_shape=jax.ShapeDtypeStruct(((n_tok + 2 * tmm) * sub, LANES), F32),
        compiler_params=_cparams("arbitrary"),
        name="moe",
    )(src, dst, tg, x2, g, wrh_t, wrl_t, br_row, wg, wu, wd)


def _final_norm_kernel(x_ref, g_ref, o_ref):
    o_ref[...] = _rms(_load_rows(x_ref, o_ref.shape[0]), g_ref[...])


def _final_norm(xs, g, n, tm):
    d = g.shape[1]
    return pl.pallas_call(
        _final_norm_kernel,
        grid=(n // tm,),
        in_specs=[_row_block(xs, tm, d), pl.BlockSpec((1, d), lambda r: (0, 0))],
        out_specs=pl.BlockSpec((tm, d), lambda r: (r, 0)),
        out_shape=jax.ShapeDtypeStruct((n, d), F32),
        compiler_params=_cparams("arbitrary"),
        name="final_norm",
    )(xs, g)


def _row(v):
    return v.reshape(1, -1)


def kernel(x, mem, mix_norm_g, w_in, fox_f_bias, fox_norm_g, hgrn_lb_logits, hgrn_norm_g, conv_w, conv_b,
           conv_norm_g, conv_norm_b, w_out, xatt_norm_g, mem_norm_g, w_xq, w_xkv, w_xo, ffn_norm_g,
           w_group, b_group, w_router, b_router, w_gate, w_up, w_down, final_norm_g):
    batch, seq, d = x.shape
    depth = w_in.shape[0]
    n = batch * seq
    tm = min(512, seq)
    tmm = min(512, seq)
    fblk = min(FOX_BLOCK, tm)
    fw = FOX_HEADS * FOX_HEAD_DIM
    hw = HGRN_HEADS * HGRN_DIM
    cw = conv_w.shape[-1]
    assert seq % tm == 0 and seq % HGRN_BLOCK == 0 and d == fw + hw + cw

    offs = [0]
    for width in (fw, fw, fw, FOX_HEADS, hw, hw, hw, hw, cw, cw):
        offs.append(offs[-1] + width)
    seg = lambda a, i, j: a[..., offs[i]:offs[j]]
    wqkv = jnp.concatenate([seg(w_in, 0, 1) * (FOX_HEAD_DIM ** -0.5 * LOG2E), seg(w_in, 1, 3)], axis=-1).astype(BF16)
    wff = jnp.swapaxes(seg(w_in, 3, 4), 1, 2).astype(BF16)
    wh = jnp.concatenate([seg(w_in, 4, 5), seg(w_in, 5, 6), seg(w_in, 6, 7), seg(w_in, 7, 8)], axis=-1).astype(BF16)
    wc = seg(w_in, 8, 10).astype(BF16)
    wo = w_out.astype(BF16)
    wxq = (w_xq * ((d // XATT_HEADS) ** -0.5)).astype(BF16)
    wxkv = w_xkv.astype(BF16)
    wxo = w_xo.astype(BF16)
    n_exp = N_GROUPS * EXPERTS_PER_GROUP
    wr = jnp.concatenate([w_group, w_router], axis=-1)
    wr_hi = wr.astype(BF16)
    wr_lo = (wr - wr_hi.astype(F32)).astype(BF16)
    pad_rows = lambda a: jnp.pad(jnp.swapaxes(a, 1, 2), ((0, 0), (0, ROUTER_ROWS - a.shape[2]), (0, 0)))
    pad_cols = lambda a: jnp.pad(a, ((0, 0), (0, 0), (0, LANES - a.shape[2])))
    br = jnp.concatenate([b_group, b_router], axis=-1)
    br_col = jnp.pad(br, ((0, 0), (0, ROUTER_ROWS - br.shape[1])))[:, :, None]
    br_row = jnp.pad(br, ((0, 0), (0, LANES - br.shape[1])))[:, None, :]
    ff = w_gate.shape[-1]
    to_cols = lambda w: jnp.transpose(w, (0, 1, 3, 2, 4)).reshape(depth, N_GROUPS, d, EXPERTS_PER_GROUP * ff).astype(BF16)
    wg, wu = to_cols(w_gate), to_cols(w_up)
    wd = w_down.reshape(depth, N_GROUPS, EXPERTS_PER_GROUP * ff, d).astype(BF16)
    conv_w_pad = jnp.pad(conv_w, ((0, 0), (0, CONV_HALO - CONV_WIDTH), (0, 0)))
    wrh_rows, wrl_rows, wrh_cols, wrl_cols = pad_rows(wr_hi), pad_rows(wr_lo), pad_cols(wr_hi), pad_cols(wr_lo)

    kv_all = _mem_kv(mem.reshape(-1, d), _row(mem_norm_g), wxkv).reshape(depth, batch, mem.shape[1], 2 * d)

    xs = x.reshape(n, d)
    for l in range(depth):
        qkv, hy, cy, lf, qn, kn = _mix_in(xs, _row(mix_norm_g[l]), wqkv[l], wh[l], wc[l], wff[l],
                                          fox_f_bias[l].reshape(-1, 1), batch, seq, tm)
        c, keep = _fox_plan(lf, qn, kn, fblk, tm)
        fo = _fox_attention(keep.reshape(-1), qkv, c, _row(fox_norm_g[l]), batch, seq, fblk, tm // fblk)
        ho = _hgrn(hy, hgrn_lb_logits, _row(hgrn_norm_g[l]), l, batch, seq, min(1024, seq))
        co = _conv(cy, conv_w_pad[l], _row(conv_b[l]), _row(conv_norm_g[l]), _row(conv_norm_b[l]), batch, seq, tm)
        x2, route, cnt = _xattn(fo, ho, co, xs, wo[l], _row(xatt_norm_g[l]), wxq[l], kv_all[l], wxo[l],
                                _row(ffn_norm_g[l]), wrh_rows[l], wrl_rows[l], br_col[l], batch, seq, tm)
        src, dst, tg = _route_plan(cnt[:N_GROUPS, 0], route, n, tmm, d // LANES)
        xs = _moe(src, dst, tg, x2, _row(ffn_norm_g[l]), wrh_cols[l], wrl_cols[l], br_row[l],
                  wg[l], wu[l], wd[l], tmm)
    return _final_norm(xs, _row(final_norm_g), n, tm).reshape(batch, seq, d)
```

```python
import functools

import jax
import jax.numpy as jnp
from jax import lax
from jax.experimental import pallas as pl
from jax.experimental.pallas import tpu as pltpu

F32 = jnp.float32
BF16 = jnp.bfloat16
I32 = jnp.int32

EPS = 1e-6
LANES = 128
SUBLANES = 8
FOX_HEADS = 8
FOX_HEAD_DIM = 64
LOG2E = 1.4426950408889634
FOX_BLOCK = 512
FOX_SKIP_EXPONENT = -34.0
FOX_NORM_SLACK = 1.02
HGRN_HEADS = 4
HGRN_DIM = 64
HGRN_CHUNK = 16
HGRN_BLOCK = 128
CONV_WIDTH = 31
CONV_HALO = 32
CONV_GROUP = 64
XATT_HEADS = 4
N_GROUPS = 4
EXPERTS_PER_GROUP = 4
ROUTER_ROWS = 32
DMA_UNROLL = 8
MOE_ISSUE_BATCHES = 2 * EXPERTS_PER_GROUP
VMEM_LIMIT_BYTES = 56 * 1024 * 1024

NT_DIMS = (((1,), (1,)), ((), ()))


def _cparams(*sem):
    return pltpu.CompilerParams(dimension_semantics=sem, vmem_limit_bytes=VMEM_LIMIT_BYTES)


def _dot(a, b):
    return jnp.dot(a, b, preferred_element_type=F32)


def _dot_nt(a, b):
    return lax.dot_general(a, b, NT_DIMS, preferred_element_type=F32)


def _rms(x, g):
    ms = jnp.mean(x * x, axis=-1, keepdims=True)
    return x * lax.rsqrt(ms + EPS) * g


def _sigmoid(x):
    return 1.0 / (1.0 + jnp.exp(-x))


def _split2(x):
    hi = x.astype(BF16)
    lo = (x - hi.astype(F32)).astype(BF16)
    return hi, lo


def _split3(x):
    hi = x.astype(BF16)
    r = x - hi.astype(F32)
    mid = r.astype(BF16)
    lo = (r - mid.astype(F32)).astype(BF16)
    return hi, mid, lo


def _halves_select(lane_lo_mask, a, b):
    return jnp.where(lane_lo_mask, a, b)


def _load_rows(ref, rows):
    if ref.shape[0] == rows:
        return ref[...]
    sub = ref.shape[0] // rows
    return jnp.concatenate([ref[pl.ds(c, rows, stride=sub), :] for c in range(sub)], axis=1)


def _store_rows(ref, val):
    rows = val.shape[0]
    if ref.shape[0] == rows:
        ref[...] = val
    else:
        sub = ref.shape[0] // rows
        for c in range(sub):
            ref[pl.ds(c, rows, stride=sub), :] = val[:, c * LANES:(c + 1) * LANES]


def _pair_rms(o, lo_mask, width):
    o2 = o * o
    s_all = jnp.sum(o2, axis=-1, keepdims=True)
    s_lo = jnp.sum(jnp.where(lo_mask, o2, 0.0), axis=-1, keepdims=True)
    ms = jnp.where(lo_mask, s_lo, s_all - s_lo) * (1.0 / width)
    return o * lax.rsqrt(ms + EPS)


def _mem_kv_kernel(mem_ref, g_ref, w_ref, o_ref):
    t = _rms(mem_ref[...], g_ref[...]).astype(BF16)
    o_ref[...] = _dot(t, w_ref[...]).astype(BF16)


def _mem_kv(mem2d, g, w_xkv):
    depth, d, d2 = w_xkv.shape
    rows = mem2d.shape[0]
    return pl.pallas_call(
        _mem_kv_kernel,
        grid=(depth,),
        in_specs=[pl.BlockSpec((rows, d), lambda l: (0, 0)),
                  pl.BlockSpec((1, d), lambda l: (0, 0)),
                  pl.BlockSpec((None, d, d2), lambda l: (l, 0, 0))],
        out_specs=pl.BlockSpec((None, rows, d2), lambda l: (l, 0, 0)),
        out_shape=jax.ShapeDtypeStruct((depth, rows, d2), BF16),
        compiler_params=_cparams("arbitrary"),
        name="mem_kv",
    )(mem2d, g, w_xkv)


def _head_norm_max(y, ind):
    n2 = _dot((y[:, :ind.shape[0]] * y[:, :ind.shape[0]]).astype(BF16), ind)
    top = jnp.broadcast_to(jnp.max(n2, axis=0, keepdims=True), (FOX_HEADS, LANES))
    row = lax.broadcasted_iota(I32, (FOX_HEADS, LANES), 0)
    lane = lax.broadcasted_iota(I32, (FOX_HEADS, LANES), 1)
    pick = lambda first: jnp.broadcast_to(
        jnp.sqrt(jnp.sum(jnp.where(lane == row + first, top, 0.0), axis=1, keepdims=True)), (FOX_HEADS, LANES))
    return pick(0), pick(FOX_HEADS)


def _mix_in_kernel(x_ref, g_ref, wqkv_ref, wh_ref, wc_ref, wff_ref, fb_ref, ind_ref,
                   qkv_ref, hy_ref, cy_ref, lf_ref, qn_ref, kn_ref, kn_scr, *, nb):
    t = _rms(_load_rows(x_ref, qkv_ref.shape[0]), g_ref[...]).astype(BF16)
    y = _dot(t, wqkv_ref[...])
    qkv_ref[...] = y.astype(BF16)
    hy_ref[...] = _dot(t, wh_ref[...]).astype(BF16)
    cy_ref[...] = _dot(t, wc_ref[...]).astype(BF16)
    z = _dot_nt(wff_ref[...], t) + fb_ref[...]
    lf_ref[...] = (jnp.minimum(z, 0.0) - jnp.log1p(jnp.exp(-jnp.abs(z)))) * LOG2E
    q_norm, k_norm = _head_norm_max(y, ind_ref[...])
    qn_ref[...] = q_norm

    @pl.when(pl.program_id(0) % nb == 0)
    def _():
        kn_scr[...] = jnp.zeros_like(kn_scr)

    kn_scr[...] = jnp.maximum(kn_scr[...], k_norm)
    kn_ref[...] = kn_scr[...]


def _row_block(a, tm, d):
    sub = 1 if a.shape[1] == d else d // a.shape[1]
    return pl.BlockSpec((tm * sub, a.shape[1]), lambda r: (r, 0))


def _layer_block(a, layer):
    return pl.BlockSpec((None,) + a.shape[1:], lambda r: (layer,) + (0,) * (a.ndim - 1))


def _mix_in(x2d, g, wqkv, wh, wc, wff, fb, layer, batch, seq, tm):
    n = batch * seq
    nb = seq // tm
    full = lambda a: pl.BlockSpec(a.shape, lambda r: (0,) * a.ndim)
    stacked = lambda a: _layer_block(a, layer)
    qk_cols = 2 * FOX_HEADS * FOX_HEAD_DIM
    ind = (jnp.arange(qk_cols)[:, None] // FOX_HEAD_DIM == jnp.arange(LANES)[None, :]).astype(BF16)
    return pl.pallas_call(
        functools.partial(_mix_in_kernel, nb=nb),
        grid=(n // tm,),
        in_specs=[_row_block(x2d, tm, wqkv.shape[1]), full(g), stacked(wqkv), stacked(wh), stacked(wc),
                  stacked(wff), full(fb), full(ind)],
        out_specs=[pl.BlockSpec((tm, wqkv.shape[2]), lambda r: (r, 0)),
                   pl.BlockSpec((tm, wh.shape[2]), lambda r: (r, 0)),
                   pl.BlockSpec((tm, wc.shape[2]), lambda r: (r, 0)),
                   pl.BlockSpec((None, FOX_HEADS, tm), lambda r: (r // nb, 0, r % nb)),
                   pl.BlockSpec((None, FOX_HEADS, LANES), lambda r: (r, 0, 0)),
                   pl.BlockSpec((None, FOX_HEADS, LANES), lambda r: (r // nb, 0, 0))],
        out_shape=[jax.ShapeDtypeStruct((n, wqkv.shape[2]), BF16),
                   jax.ShapeDtypeStruct((n, wh.shape[2]), BF16),
                   jax.ShapeDtypeStruct((n, wc.shape[2]), BF16),
                   jax.ShapeDtypeStruct((batch, FOX_HEADS, seq), F32),
                   jax.ShapeDtypeStruct((n // tm, FOX_HEADS, LANES), F32),
                   jax.ShapeDtypeStruct((batch, FOX_HEADS, LANES), F32)],
        scratch_shapes=[pltpu.VMEM((FOX_HEADS, LANES), F32)],
        compiler_params=_cparams("arbitrary"),
        name="mix_in",
    )(x2d, g, wqkv, wh, wc, wff, fb, ind)


def _fox_plan_kernel(lf_ref, qn_ref, kn_ref, c_ref, keep_ref, *, blk, stat_rows):
    seq = lf_ref.shape[-1]
    nblk = seq // blk
    r = lax.broadcasted_iota(I32, (blk, blk), 0)
    c = lax.broadcasted_iota(I32, (blk, blk), 1)
    upper = jnp.where(r <= c, 1.0, 0.0).astype(BF16)
    lane = lax.broadcasted_iota(I32, (FOX_HEADS, LANES), 1)
    carry = jnp.zeros((FOX_HEADS, 1), F32)
    first = jnp.zeros((FOX_HEADS, LANES), F32)
    last = jnp.zeros((FOX_HEADS, LANES), F32)
    qn = jnp.zeros((FOX_HEADS, LANES), F32)
    for b in range(nblk):
        x = lf_ref[:, b * blk:(b + 1) * blk]
        hi, mid, lo = _split3(x)
        cb = _dot(hi, upper) + _dot(mid, upper) + _dot(lo, upper) + carry
        for h in range(FOX_HEADS):
            c_ref[h, :, b * blk:(b + 1) * blk] = cb[h:h + 1, :]
        carry = cb[:, blk - 1:blk]
        first = jnp.where(lane == b, cb[:, 0:1], first)
        last = jnp.where(lane == b, carry, last)
        qn = jnp.where(lane == b, qn_ref[b * blk // stat_rows], qn)
    bound = FOX_NORM_SLACK * 2.0 * qn * kn_ref[...] + first
    keep = jnp.zeros((FOX_HEADS, LANES), I32)
    for i in range(nblk):
        live = (bound[:, i:i + 1] - last >= FOX_SKIP_EXPONENT * LOG2E) & (lane < i)
        keep = jnp.where(lane == i, jnp.sum(live.astype(I32), axis=1, keepdims=True), keep)
    keep_ref[...] = keep


def _fox_plan(lf, qn, kn, blk, stat_rows):
    batch, heads, seq = lf.shape
    assert seq // blk <= LANES
    return pl.pallas_call(
        functools.partial(_fox_plan_kernel, blk=blk, stat_rows=stat_rows),
        grid=(batch,),
        in_specs=[pl.BlockSpec((None, heads, seq), lambda b: (b, 0, 0)),
                  pl.BlockSpec((seq // stat_rows, heads, LANES), lambda b: (b, 0, 0)),
                  pl.BlockSpec((None, heads, LANES), lambda b: (b, 0, 0))],
        out_specs=[pl.BlockSpec((None, heads, 1, seq), lambda b: (b, 0, 0, 0)),
                   pl.BlockSpec((None, heads, LANES), lambda b: (b, 0, 0))],
        out_shape=[jax.ShapeDtypeStruct((batch, heads, 1, seq), F32),
                   jax.ShapeDtypeStruct((batch, heads, LANES), I32)],
        compiler_params=_cparams("arbitrary"),
        name="fox_plan",
    )(lf, qn, kn)


def _fox_kernel(keep_ref, q_ref, k_ref, v_ref, c_ref, g_ref, o_ref, *, tq):
    head0 = (pl.program_id(0) * FOX_HEADS + 2 * pl.program_id(1)) * LANES
    lane = lax.broadcasted_iota(I32, (1, LANES), 1)
    lo_mask = lane < FOX_HEAD_DIM
    causal = lax.broadcasted_iota(I32, (tq, tq), 0) >= lax.broadcasted_iota(I32, (tq, tq), 1)

    i = pl.program_id(2)
    n_prev = jnp.maximum(keep_ref[head0 + i], keep_ref[head0 + LANES + i])
    q = q_ref[...]
    zero = jnp.zeros_like(q)
    qh = (jnp.where(lo_mask, q, zero), jnp.where(lo_mask, zero, q))
    q0 = pl.multiple_of(i * tq, tq)
    cq0 = tuple(c_ref[h, :, pl.ds(q0, LANES)][:, 0:1] for h in range(2))

    def block(j, carry, masked):
        k0 = pl.multiple_of(j * tq, tq)
        kb = k_ref[pl.ds(k0, tq), :]
        vb = v_ref[pl.ds(k0, tq), :]
        s = [_dot_nt(qh[h], kb) + (cq0[h] - c_ref[h, :, pl.ds(k0, tq)]) for h in range(2)]
        out = []
        for h in range(2):
            m, l, acc = carry[h]
            sh = jnp.where(causal, s[h], -jnp.inf) if masked else s[h]
            m_new = jnp.maximum(m, jnp.max(sh, axis=-1, keepdims=True))
            alpha = jnp.exp2(m - m_new)
            p = jnp.exp2(sh - m_new)
            l = alpha * l + jnp.sum(p, axis=-1, keepdims=True)
            acc = alpha * acc + _dot(p.astype(BF16), vb)
            out.append((m_new, l, acc))
        return tuple(out)

    init = tuple((jnp.full((tq, 1), -jnp.inf, F32), jnp.zeros((tq, 1), F32), jnp.zeros((tq, LANES), F32))
                 for _ in range(2))
    carry = block(i, init, True)
    carry = lax.fori_loop(i - n_prev, i, lambda j, c: block(j, c, False), carry)
    (_, l0, a0), (_, l1, a1) = carry
    o = jnp.where(lo_mask, a0 * (1.0 / l0), a1 * (1.0 / l1))
    o_ref[...] = (_pair_rms(o, lo_mask, FOX_HEAD_DIM) * g_ref[...]).astype(o_ref.dtype)


def _fox_attention(keep, qkv, c, gain, batch, seq, tq):
    n = qkv.shape[0]
    pairs = FOX_HEADS // 2
    rows = tq
    nq = seq // rows
    grid_spec = pltpu.PrefetchScalarGridSpec(
        num_scalar_prefetch=1,
        grid=(batch, pairs, nq),
        in_specs=[pl.BlockSpec((rows, LANES), lambda b, p, i, keep: (b * nq + i, p)),
                  pl.BlockSpec((seq, LANES), lambda b, p, i, keep: (b, pairs + p)),
                  pl.BlockSpec((seq, LANES), lambda b, p, i, keep: (b, 2 * pairs + p)),
                  pl.BlockSpec((None, 2, 1, seq), lambda b, p, i, keep: (b, p, 0, 0)),
                  pl.BlockSpec((1, LANES), lambda b, p, i, keep: (0, p))],
        out_specs=pl.BlockSpec((rows, LANES), lambda b, p, i, keep: (b * nq + i, p)),
    )
    return pl.pallas_call(
        functools.partial(_fox_kernel, tq=tq),
        grid_spec=grid_spec,
        out_shape=jax.ShapeDtypeStruct((n, pairs * LANES), BF16),
        compiler_params=_cparams("parallel", "parallel", "arbitrary"),
        name="fox_attention",
    )(keep, qkv, qkv, qkv, c, gain)


def _hgrn_kernel(q_ref, f_ref, v_ref, gate_ref, lbz_ref, gain_ref, o_ref, st_ref, u_scr, prev_scr, *, layer, n_sub):
    T, C = HGRN_BLOCK, HGRN_CHUNK
    nchunk = T // C
    i = pl.program_id(2)

    @pl.when(i == 0)
    def _():
        st_ref[...] = jnp.zeros_like(st_ref)

    z = lbz_ref[...]
    e = jnp.exp(z - jnp.max(z, axis=0, keepdims=True))
    pz = e / jnp.sum(e, axis=0, keepdims=True)
    lb = jnp.zeros((1, LANES), F32)
    for j in range(1, layer + 1):
        lb = lb + pz[j:j + 1, :]

    lane = lax.broadcasted_iota(I32, (1, LANES), 1)
    lo_mask = lane < HGRN_DIM
    r = lax.broadcasted_iota(I32, (T, T), 0)
    c = lax.broadcasted_iota(I32, (T, T), 1)
    same_chunk = (r // C) == (c // C)
    one = lambda m: jnp.where(m, 1.0, 0.0).astype(BF16)
    scan_mat = jnp.concatenate([one(same_chunk & (c <= r)),
                                one(same_chunk & ((c % C) <= C // 2)),
                                one(same_chunk)], axis=0)
    intra_mask = same_chunk & (c <= r)
    chunk_of_row = lax.broadcasted_iota(I32, (T, LANES), 0) // C
    vr = lax.broadcasted_iota(I32, (LANES, nchunk * LANES), 0)
    kc = lax.broadcasted_iota(I32, (LANES, nchunk * LANES), 1)
    same_head = (vr < HGRN_DIM) == ((kc % LANES) < HGRN_DIM)

    subs = range(n_sub)
    rows = [pl.ds(sb * T, T) for sb in subs]
    q = [q_ref[r_, :].astype(F32) for r_ in rows]
    f = [lb + (1.0 - lb) * _sigmoid(f_ref[r_, :].astype(F32)) for r_ in rows]
    kk = [1.0 - f_ for f_ in f]
    parts = []
    for f_ in f:
        parts.extend(_split2(jnp.log(f_)))
    sc = _dot(scan_mat, jnp.concatenate(parts, axis=1))
    sc = [sc[:, (2 * sb) * LANES:(2 * sb + 1) * LANES] + sc[:, (2 * sb + 1) * LANES:(2 * sb + 2) * LANES]
          for sb in subs]
    b = [s_[:T] for s_ in sc]
    b_mid = [s_[T:2 * T] for s_ in sc]
    b_last = [s_[2 * T:] for s_ in sc]
    v = [v_ref[r_, :] for r_ in rows]
    for sb in subs:
        k_out = kk[sb] * jnp.exp(b_last[sb] - b[sb])
        k_exp = jnp.concatenate([jnp.where(chunk_of_row == j, k_out, 0.0) for j in range(nchunk)],
                                axis=1).astype(BF16)
        v_t = v[sb].astype(F32).T.astype(BF16)
        u_scr[sb] = jnp.where(same_head, _dot(v_t, k_exp), 0.0)
    att = []
    for sb in subs:
        q_in = (q[sb] * jnp.exp(b[sb] - b_mid[sb])).astype(BF16)
        k_in = (kk[sb] * jnp.exp(b_mid[sb] - b[sb])).astype(BF16)
        zq = jnp.zeros_like(q_in)
        att.append([jnp.where(intra_mask, _dot_nt(qm, k_in), 0.0).astype(BF16)
                    for qm in (jnp.where(lo_mask, q_in, zq), jnp.where(lo_mask, zq, q_in))])
    o = [jnp.where(lo_mask, _dot(att[sb][0], v[sb]), _dot(att[sb][1], v[sb])) for sb in subs]
    state = st_ref[...]
    for sb in subs:
        decay = jnp.exp(b_last[sb])
        for j in range(nchunk):
            prev_scr[sb, j * LANES:(j + 1) * LANES, :] = state.astype(BF16)
            state = state * decay[j * C:j * C + 1, :] + u_scr[sb, :, j * LANES:(j + 1) * LANES]
    st_ref[...] = state
    for sb in subs:
        q_out = (q[sb] * jnp.exp(b[sb])).astype(BF16)
        o_all = _dot_nt(q_out, prev_scr[sb])
        acc = o[sb]
        for j in range(nchunk):
            acc = acc + jnp.where(chunk_of_row == j, o_all[:, j * LANES:(j + 1) * LANES], 0.0)
        gate = gate_ref[rows[sb], :].astype(F32)
        y = _pair_rms(acc, lo_mask, HGRN_DIM) * gain_ref[...] * (gate * _sigmoid(gate))
        o_ref[rows[sb], :] = y.astype(o_ref.dtype)


def _hgrn(hy, lb_logits, gain, layer, batch, seq, tg):
    n = hy.shape[0]
    pairs = HGRN_HEADS // 2
    nb = seq // tg
    depth = lb_logits.shape[0]
    col = lambda k: pl.BlockSpec((tg, LANES), lambda b, p, i, k=k: (b * nb + i, k * pairs + p))
    n_sub = tg // HGRN_BLOCK
    states = (HGRN_BLOCK // HGRN_CHUNK) * LANES
    return pl.pallas_call(
        functools.partial(_hgrn_kernel, layer=layer, n_sub=n_sub),
        grid=(batch, pairs, nb),
        in_specs=[col(0), col(1), col(2), col(3),
                  pl.BlockSpec((depth, LANES), lambda b, p, i: (0, p)),
                  pl.BlockSpec((1, LANES), lambda b, p, i: (0, p))],
        out_specs=pl.BlockSpec((tg, LANES), lambda b, p, i: (b * nb + i, p)),
        out_shape=jax.ShapeDtypeStruct((n, pairs * LANES), BF16),
        scratch_shapes=[pltpu.VMEM((LANES, LANES), F32),
                        pltpu.VMEM((n_sub, LANES, states), F32),
                        pltpu.VMEM((n_sub, states, LANES), BF16)],
        compiler_params=_cparams("parallel", "parallel", "arbitrary"),
        name="hgrn",
    )(hy, hy, hy, hy, lb_logits, gain)


def _conv_kernel(cu_ref, cg_ref, w_ref, b_ref, ng_ref, nb_ref, o_ref, a_scr, sh_scr, *, tm):
    i = pl.program_id(1)
    sub = SUBLANES

    @pl.when(i == 0)
    def _():
        a_scr[0:CONV_HALO, :] = jnp.zeros((CONV_HALO, a_scr.shape[1]), F32)

    @pl.when(i > 0)
    def _():
        a_scr[0:CONV_HALO, :] = a_scr[tm:tm + CONV_HALO, :]

    a_scr[CONV_HALO:CONV_HALO + tm, :] = cu_ref[...].astype(F32) * _sigmoid(cg_ref[...].astype(F32))
    span = CONV_HALO + tm - sub
    for k in range(1, sub):
        sh_scr[k - 1, 0:span, :] = a_scr[k:k + span, :]
    acc = jnp.zeros((tm, a_scr.shape[1]), F32) + b_ref[...]
    first = CONV_HALO - (CONV_WIDTH - 1)
    for w in range(CONV_WIDTH):
        base, k = divmod(first + w, sub)
        src = a_scr if k == 0 else sh_scr.at[k - 1]
        acc = acc + src[base * sub:base * sub + tm, :] * w_ref[w:w + 1, :]
    lane = lax.broadcasted_iota(I32, (1, LANES), 1)
    lo_mask = lane < CONV_GROUP
    halves = []
    for hh in range(acc.shape[1] // LANES):
        xh = acc[:, hh * LANES:(hh + 1) * LANES]
        s_all = jnp.sum(xh, axis=-1, keepdims=True)
        s_lo = jnp.sum(jnp.where(lo_mask, xh, 0.0), axis=-1, keepdims=True)
        d = xh - jnp.where(lo_mask, s_lo, s_all - s_lo) * (1.0 / CONV_GROUP)
        halves.append(_pair_rms(d, lo_mask, CONV_GROUP))
    y = jnp.concatenate(halves, axis=1) * ng_ref[...] + nb_ref[...]
    o_ref[...] = (y * _sigmoid(y)).astype(o_ref.dtype)


def _conv(cy, w, b, ng, nb_, batch, seq, tm):
    n = cy.shape[0]
    ch = cy.shape[1] // 2
    nb = seq // tm
    full = lambda a: pl.BlockSpec(a.shape, lambda bb, i: (0,) * a.ndim)
    return pl.pallas_call(
        functools.partial(_conv_kernel, tm=tm),
        grid=(batch, nb),
        in_specs=[pl.BlockSpec((tm, ch), lambda bb, i: (bb * nb + i, 0)),
                  pl.BlockSpec((tm, ch), lambda bb, i: (bb * nb + i, 1)),
                  full(w), full(b), full(ng), full(nb_)],
        out_specs=pl.BlockSpec((tm, ch), lambda bb, i: (bb * nb + i, 0)),
        out_shape=jax.ShapeDtypeStruct((n, ch), BF16),
        scratch_shapes=[pltpu.VMEM((CONV_HALO + tm, ch), F32),
                        pltpu.VMEM((SUBLANES - 1, CONV_HALO + tm, ch), F32)],
        compiler_params=_cparams("parallel", "arbitrary"),
        name="conv",
    )(cy, cy, w, b, ng, nb_)


def _router_logits_t(t, wr_hi, wr_lo, br):
    th, tl = _split2(t)
    return _dot_nt(wr_hi, th) + _dot_nt(wr_lo, th) + _dot_nt(wr_hi, tl) + br


def _xattn_kernel(fo_ref, ho_ref, co_ref, x_ref, wo_ref, gx_ref, wq_ref, kv_ref, wxo_ref, g_ref,
                  wrh_ref, wrl_ref, br_ref, x2_ref, route_ref, cnt_ref, cnt_scr, *, tm):
    step = pl.program_id(0)

    @pl.when(step == 0)
    def _():
        cnt_scr[...] = jnp.zeros_like(cnt_scr)

    nf, nh = fo_ref.shape[1], ho_ref.shape[1]
    x1 = (_load_rows(x_ref, tm) + _dot(fo_ref[...], wo_ref[0:nf, :]) + _dot(ho_ref[...], wo_ref[nf:nf + nh, :])
          + _dot(co_ref[...], wo_ref[nf + nh:, :]))
    xq = _dot(_rms(x1, gx_ref[...]).astype(BF16), wq_ref[...]).astype(BF16)
    d = xq.shape[1]
    hd = d // XATT_HEADS
    scores = [_dot_nt(xq[:, h * hd:(h + 1) * hd], kv_ref[:, h * hd:(h + 1) * hd]) for h in range(XATT_HEADS)]
    probs = [jnp.exp(s - jnp.max(s, axis=-1, keepdims=True)) for s in scores]
    heads = []
    for h, p in enumerate(probs):
        vh = kv_ref[:, d + h * hd:d + (h + 1) * hd]
        l = jnp.sum(p, axis=-1, keepdims=True)
        heads.append((_dot(p.astype(BF16), vh) * (1.0 / l)).astype(BF16))
    x2 = x1 + _dot(jnp.concatenate(heads, axis=1), wxo_ref[...])
    _store_rows(x2_ref, x2)
    logits = _router_logits_t(_rms(x2, g_ref[...]), wrh_ref[...], wrl_ref[...], br_ref[...])
    gl = [logits[g:g + 1, :] for g in range(N_GROUPS)]
    gmax = jnp.maximum(jnp.maximum(gl[0], gl[1]), jnp.maximum(gl[2], gl[3]))
    gidx = jnp.where(gl[0] >= gmax, 0, jnp.where(gl[1] >= gmax, 1, jnp.where(gl[2] >= gmax, 2, 3)))
    rows8 = lax.broadcasted_iota(I32, (8, tm), 0)
    onehot = jnp.where(rows8 == gidx, 1.0, 0.0)
    r = lax.broadcasted_iota(I32, (tm, tm), 0)
    c = lax.broadcasted_iota(I32, (tm, tm), 1)
    incl = _dot(onehot.astype(BF16), jnp.where(r <= c, 1.0, 0.0).astype(BF16))
    before = cnt_scr[:, 0:1]
    rank = jnp.sum(onehot * (incl - 1.0 + before), axis=0, keepdims=True)
    route_ref[...] = jnp.where(rows8 == 0, gidx, jnp.where(rows8 == 1, rank.astype(I32), 0))
    cnt_scr[...] = cnt_scr[...] + incl[:, tm - 1:tm]
    cnt_ref[...] = cnt_scr[...].astype(I32)


def _xattn(fo, ho, co, x2d, wo, gx, wq, kv, wxo, g, wrh, wrl, br, layer, batch, seq, tm):
    n, d = fo.shape[0], wo.shape[2]
    nb = seq // tm
    mem = kv.shape[2]
    full = lambda a: pl.BlockSpec(a.shape, lambda r: (0,) * a.ndim)
    stacked = lambda a: _layer_block(a, layer)
    rowblk = lambda a: _row_block(a, tm, d if a is x2d else a.shape[1])
    return pl.pallas_call(
        functools.partial(_xattn_kernel, tm=tm),
        grid=(n // tm,),
        in_specs=[rowblk(fo), rowblk(ho), rowblk(co), rowblk(x2d), stacked(wo), full(gx), stacked(wq),
                  pl.BlockSpec((None, None, mem, 2 * d), lambda r: (layer, r // nb, 0, 0)),
                  stacked(wxo), full(g), full(wrh), full(wrl), full(br)],
        out_specs=[pl.BlockSpec((tm * (d // LANES), LANES), lambda r: (r, 0)),
                   pl.BlockSpec((8, tm), lambda r: (0, r)),
                   pl.BlockSpec((8, LANES), lambda r: (0, 0))],
        out_shape=[jax.ShapeDtypeStruct((n * (d // LANES), LANES), F32),
                   jax.ShapeDtypeStruct((8, n), I32),
                   jax.ShapeDtypeStruct((8, LANES), I32)],
        scratch_shapes=[pltpu.VMEM((8, LANES), F32)],
        compiler_params=_cparams("arbitrary"),
        name="xattn",
    )(fo, ho, co, x2d, wo, gx, wq, kv, wxo, g, wrh, wrl, br)


def _plan_kernel(cnt_ref, route_ref, src_ref, dst_ref, tg_ref, dest_vmem, dest_smem, sem, *, n_tok, tmm, sub):
    n_slots = src_ref.shape[0]
    n_tiles = tg_ref.shape[0] - 1
    shift = tmm.bit_length() - 1
    assert tmm == 1 << shift

    def clear(s, _):
        src_ref[s] = 0
        parity = lax.shift_right_logical(s, shift) & 1
        dst_ref[s] = (n_tok + parity * tmm + (s & (tmm - 1))) * sub
        return 0

    off = jnp.int32(0)
    last_group = jnp.int32(0)
    starts, ends = [], []
    for g in range(N_GROUPS):
        cnt = cnt_ref[g]
        padded = lax.shift_left(lax.shift_right_logical(cnt + (tmm - 1), shift), shift)
        lax.fori_loop(off + cnt, off + padded, clear, 0)
        starts.append(off)
        off = off + padded
        ends.append(off)
        last_group = jnp.where(cnt > 0, g, last_group)
    lax.fori_loop(off, n_slots, clear, 0)

    gidx = route_ref[0:1, :]
    rank = route_ref[1:2, :]
    start = jnp.where(gidx == 0, starts[0], jnp.where(gidx == 1, starts[1],
                                                      jnp.where(gidx == 2, starts[2], starts[3])))
    dest_vmem[...] = start + rank
    to_smem = pltpu.make_async_copy(dest_vmem, dest_smem, sem)
    to_smem.start()
    to_smem.wait()

    def place(t, _):
        slot = dest_smem[0, t]
        src_ref[slot] = t * sub
        dst_ref[slot] = t * sub
        return 0

    lax.fori_loop(0, n_tok, place, 0, unroll=8)

    def tile(k, _):
        start = k * tmm
        g = ((start >= ends[0]).astype(I32) + (start >= ends[1]).astype(I32) + (start >= ends[2]).astype(I32))
        tg_ref[k] = jnp.minimum(g, last_group)
        return 0

    lax.fori_loop(0, n_tiles, tile, 0)
    tg_ref[n_tiles] = lax.shift_right_logical(off, shift)


def _route_plan(cnt, route, n_tok, tmm, sub):
    n_tiles = n_tok // tmm + N_GROUPS
    smem = pl.BlockSpec(memory_space=pltpu.SMEM)
    slots = jax.ShapeDtypeStruct(((n_tiles + 1) * tmm,), I32)
    return pl.pallas_call(
        functools.partial(_plan_kernel, n_tok=n_tok, tmm=tmm, sub=sub),
        in_specs=[smem, pl.BlockSpec(memory_space=pltpu.VMEM)],
        out_specs=[smem, smem, smem],
        out_shape=[slots, slots, jax.ShapeDtypeStruct((n_tiles + 1,), I32)],
        scratch_shapes=[pltpu.VMEM((1, n_tok), I32), pltpu.SMEM((1, n_tok), I32), pltpu.SemaphoreType.DMA(())],
        name="route_plan",
    )(cnt, route)


def _moe_kernel(src_ref, dst_ref, tg_ref, x2_hbm, g_ref, wrh_ref, wrl_ref, br_ref, wg_ref, wu_ref, wd_ref,
                x3_hbm, xb, ob, gsem, ssem, zsem, *, tmm, n_tok):
    k = pl.program_id(0)
    n_active = tg_ref[pl.num_programs(0)]
    slot = k % 2

    sub = xb.shape[1] // tmm

    def rows_at(first):
        return pl.ds(pl.multiple_of(first, sub), sub)

    def gather_copy(tile, sl, r):
        return pltpu.make_async_copy(x2_hbm.at[rows_at(src_ref[tile * tmm + r])], xb.at[sl, rows_at(r * sub)],
                                     gsem.at[sl])

    def scatter_copy(tile, sl, r):
        return pltpu.make_async_copy(ob.at[sl, rows_at(r * sub)], x3_hbm.at[rows_at(dst_ref[tile * tmm + r])],
                                     ssem.at[sl])

    def for_rows(fn):
        def body(r8, _):
            for u in range(DMA_UNROLL):
                fn(r8 * DMA_UNROLL + u, u % 2)
            return 0
        lax.fori_loop(0, tmm // DMA_UNROLL, body, 0)

    def dump_fill(half):
        return pltpu.make_async_copy(ob.at[1], x3_hbm.at[pl.ds((n_tok + half * tmm) * sub, tmm * sub)], zsem)

    @pl.when(k == 0)
    def _():
        ob[1] = jnp.zeros(ob.shape[1:], F32)
        dump_fill(0).start()
        dump_fill(1).start()
        for_rows(lambda r, pri: gather_copy(0, 0, r).start(priority=pri))

    for_rows(lambda r, pri: gather_copy(k, slot, r).wait())

    @pl.when(k >= n_active)
    def _():
        for_rows(lambda r, pri: gather_copy(k + 1, 1 - slot, r).start(priority=pri))

    @pl.when(k < n_active)
    def _():
        @pl.when(k >= 2)
        def _():
            for_rows(lambda r, pri: scatter_copy(k - 2, slot, r).wait())

        batch = tmm // MOE_ISSUE_BATCHES

        def issue_next(bi):
            for r in range(bi * batch, (bi + 1) * batch):
                gather_copy(k + 1, 1 - slot, r).start(priority=r % 2)

        y = _moe_tile(_load_rows(xb.at[slot], tmm), tg_ref[k], g_ref, wrh_ref, wrl_ref, br_ref,
                      wg_ref, wu_ref, wd_ref, issue_next)
        _store_rows(ob.at[slot], y)

        @pl.when(k == 0)
        def _():
            dump_fill(0).wait()
            dump_fill(1).wait()

        for_rows(lambda r, pri: scatter_copy(k, slot, r).start(priority=pri))

        @pl.when(k == n_active - 1)
        def _():
            @pl.when(k >= 1)
            def _():
                for_rows(lambda r, pri: scatter_copy(k - 1, 1 - slot, r).wait())
            for_rows(lambda r, pri: scatter_copy(k, slot, r).wait())

    @pl.when(k == pl.num_programs(0) - 1)
    def _():
        for_rows(lambda r, pri: gather_copy(k + 1, 1 - slot, r).wait())


def _moe_tile(x, grp, g_ref, wrh_ref, wrl_ref, br_ref, wg_ref, wu_ref, wd_ref, issue):
    t = _rms(x, g_ref[...])
    tb = t.astype(BF16)
    th, tl = _split2(t)
    logits = _dot(th, wrh_ref[...]) + _dot(th, wrl_ref[...]) + _dot(tl, wrh_ref[...]) + br_ref[...]
    lane = lax.broadcasted_iota(I32, (1, LANES), 1)
    lanef = lane.astype(F32)
    rmax = lambda a: jnp.max(a, axis=-1, keepdims=True)
    rsum = lambda a: jnp.sum(a, axis=-1, keepdims=True)
    gmask = lane < N_GROUPS
    gmax = rmax(jnp.where(gmask, logits, -jnp.inf))
    zg = rsum(jnp.where(gmask, jnp.exp(logits - gmax), 0.0))
    p_group = jnp.exp(rsum(jnp.where(lane == grp, logits, 0.0)) - gmax) / zg
    e_lo = N_GROUPS + EXPERTS_PER_GROUP * grp
    emask = (lane >= e_lo) & (lane < e_lo + EXPERTS_PER_GROUP)
    em = jnp.where(emask, logits, -jnp.inf)
    e1 = rmax(em)
    i1 = jnp.min(jnp.where(em == e1, lanef, 1e9), axis=-1, keepdims=True)
    em2 = jnp.where(lanef == i1, -jnp.inf, em)
    e2 = rmax(em2)
    i2 = jnp.min(jnp.where(em2 == e2, lanef, 1e9), axis=-1, keepdims=True)
    ze = rsum(jnp.where(emask, jnp.exp(logits - e1), 0.0))
    p1 = 1.0 / ze
    p2 = jnp.exp(e2 - e1) / ze
    w1 = p_group * (p1 / (p1 + p2))
    w2 = p_group * (p2 / (p1 + p2))
    ff = wg_ref.shape[1] // EXPERTS_PER_GROUP
    cw_lanes = jnp.where(lanef == i1, w1, 0.0) + jnp.where(lanef == i2, w2, 0.0)
    parts = []
    for e in range(EXPERTS_PER_GROUP):
        cols = slice(e * ff, (e + 1) * ff)
        gate = _dot(tb, wg_ref[:, cols])
        up = _dot(tb, wu_ref[:, cols])
        issue(e)
        cw = rsum(jnp.where(lane == e_lo + e, cw_lanes, 0.0))
        parts.append((gate * _sigmoid(gate) * up * cw).astype(BF16))
    hidden = jnp.concatenate(parts, axis=1)
    outs = []
    width = wd_ref.shape[1] // EXPERTS_PER_GROUP
    for c in range(EXPERTS_PER_GROUP):
        cols = slice(c * width, (c + 1) * width)
        down = _dot(hidden, wd_ref[:, cols])
        issue(EXPERTS_PER_GROUP + c)
        outs.append(x[:, cols] + down)
    return jnp.concatenate(outs, axis=1)


def _moe(src, dst, tg, x2, g, wrh_t, wrl_t, br_row, wg, wu, wd, layer, tmm):
    sub = g.shape[1] // LANES
    n_tok = x2.shape[0] // sub
    n_tiles = tg.shape[0] - 1
    full = lambda a: pl.BlockSpec(a.shape, lambda k, src, dst, tg: (0,) * a.ndim)
    grp = lambda a: pl.BlockSpec((None, None) + a.shape[2:], lambda k, src, dst, tg: (layer, tg[k], 0, 0))
    grid_spec = pltpu.PrefetchScalarGridSpec(
        num_scalar_prefetch=3,
        grid=(n_tiles,),
        in_specs=[pl.BlockSpec(memory_space=pl.ANY), full(g), full(wrh_t), full(wrl_t), full(br_row),
                  grp(wg), grp(wu), grp(wd)],
        out_specs=pl.BlockSpec(memory_space=pl.ANY),
        scratch_shapes=[pltpu.VMEM((2, tmm * sub, LANES), F32), pltpu.VMEM((2, tmm * sub, LANES), F32),
                        pltpu.SemaphoreType.DMA((2,)), pltpu.SemaphoreType.DMA((2,)), pltpu.SemaphoreType.DMA(())],
    )
    return pl.pallas_call(
        functools.partial(_moe_kernel, tmm=tmm, n_tok=n_tok),
        grid_spec=grid_spec,
        out_shape=jax.ShapeDtypeStruct(((n_tok + 2 * tmm) * sub, LANES), F32),
        compiler_params=_cparams("arbitrary"),
        name="moe",
    )(src, dst, tg, x2, g, wrh_t, wrl_t, br_row, wg, wu, wd)


def _final_norm_kernel(x_ref, g_ref, o_ref):
    o_ref[...] = _rms(_load_rows(x_ref, o_ref.shape[0]), g_ref[...])


def _final_norm(xs, g, n, tm):
    d = g.shape[1]
    return pl.pallas_call(
        _final_norm_kernel,
        grid=(n // tm,),
        in_specs=[_row_block(xs, tm, d), pl.BlockSpec((1, d), lambda r: (0, 0))],
        out_specs=pl.BlockSpec((tm, d), lambda r: (r, 0)),
        out_shape=jax.ShapeDtypeStruct((n, d), F32),
        compiler_params=_cparams("arbitrary"),
        name="final_norm",
    )(xs, g)


def _row(v):
    return v.reshape(1, -1)


def kernel(x, mem, mix_norm_g, w_in, fox_f_bias, fox_norm_g, hgrn_lb_logits, hgrn_norm_g, conv_w, conv_b,
           conv_norm_g, conv_norm_b, w_out, xatt_norm_g, mem_norm_g, w_xq, w_xkv, w_xo, ffn_norm_g,
           w_group, b_group, w_router, b_router, w_gate, w_up, w_down, final_norm_g):
    batch, seq, d = x.shape
    depth = w_in.shape[0]
    n = batch * seq
    tm = min(512, seq)
    tmm = min(512, seq)
    fblk = min(FOX_BLOCK, tm)
    fw = FOX_HEADS * FOX_HEAD_DIM
    hw = HGRN_HEADS * HGRN_DIM
    cw = conv_w.shape[-1]
    assert seq % tm == 0 and seq % HGRN_BLOCK == 0 and d == fw + hw + cw

    offs = [0]
    for width in (fw, fw, fw, FOX_HEADS, hw, hw, hw, hw, cw, cw):
        offs.append(offs[-1] + width)
    seg = lambda a, i, j: a[..., offs[i]:offs[j]]
    wqkv = jnp.concatenate([seg(w_in, 0, 1) * (FOX_HEAD_DIM ** -0.5 * LOG2E), seg(w_in, 1, 3)], axis=-1).astype(BF16)
    wff = jnp.swapaxes(seg(w_in, 3, 4), 1, 2).astype(BF16)
    wh = jnp.concatenate([seg(w_in, 4, 5), seg(w_in, 5, 6), seg(w_in, 6, 7), seg(w_in, 7, 8)], axis=-1).astype(BF16)
    wc = seg(w_in, 8, 10).astype(BF16)
    wo = w_out.astype(BF16)
    wxq = (w_xq * ((d // XATT_HEADS) ** -0.5)).astype(BF16)
    wxkv = w_xkv.astype(BF16)
    wxo = w_xo.astype(BF16)
    n_exp = N_GROUPS * EXPERTS_PER_GROUP
    wr = jnp.concatenate([w_group, w_router], axis=-1)
    wr_hi = wr.astype(BF16)
    wr_lo = (wr - wr_hi.astype(F32)).astype(BF16)
    pad_rows = lambda a: jnp.pad(jnp.swapaxes(a, 1, 2), ((0, 0), (0, ROUTER_ROWS - a.shape[2]), (0, 0)))
    pad_cols = lambda a: jnp.pad(a, ((0, 0), (0, 0), (0, LANES - a.shape[2])))
    br = jnp.concatenate([b_group, b_router], axis=-1)
    br_col = jnp.pad(br, ((0, 0), (0, ROUTER_ROWS - br.shape[1])))[:, :, None]
    br_row = jnp.pad(br, ((0, 0), (0, LANES - br.shape[1])))[:, None, :]
    ff = w_gate.shape[-1]
    to_cols = lambda w: jnp.transpose(w, (0, 1, 3, 2, 4)).reshape(depth, N_GROUPS, d, EXPERTS_PER_GROUP * ff).astype(BF16)
    wg, wu = to_cols(w_gate), to_cols(w_up)
    wd = w_down.reshape(depth, N_GROUPS, EXPERTS_PER_GROUP * ff, d).astype(BF16)
    conv_w_pad = jnp.pad(conv_w, ((0, 0), (0, CONV_HALO - CONV_WIDTH), (0, 0)))
    wrh_rows, wrl_rows, wrh_cols, wrl_cols = pad_rows(wr_hi), pad_rows(wr_lo), pad_cols(wr_hi), pad_cols(wr_lo)

    kv_all = _mem_kv(mem.reshape(-1, d), _row(mem_norm_g), wxkv).reshape(depth, batch, mem.shape[1], 2 * d)

    xs = x.reshape(n, d)
    for l in range(depth):
        qkv, hy, cy, lf, qn, kn = _mix_in(xs, _row(mix_norm_g[l]), wqkv, wh, wc, wff,
                                          fox_f_bias[l].reshape(-1, 1), l, batch, seq, tm)
        c, keep = _fox_plan(lf, qn, kn, fblk, tm)
        fo = _fox_attention(keep.reshape(-1), qkv, c, _row(fox_norm_g[l]), batch, seq, fblk)
        ho = _hgrn(hy, hgrn_lb_logits, _row(hgrn_norm_g[l]), l, batch, seq, min(1024, seq))
        co = _conv(cy, conv_w_pad[l], _row(conv_b[l]), _row(conv_norm_g[l]), _row(conv_norm_b[l]), batch, seq, tm)
        x2, route, cnt = _xattn(fo, ho, co, xs, wo, _row(xatt_norm_g[l]), wxq, kv_all, wxo,
                                _row(ffn_norm_g[l]), wrh_rows[l], wrl_rows[l], br_col[l], l, batch, seq, tm)
        src, dst, tg = _route_plan(cnt[:N_GROUPS, 0], route, n, tmm, d // LANES)
        xs = _moe(src, dst, tg, x2, _row(ffn_norm_g[l]), wrh_cols[l], wrl_cols[l], br_row[l],
                  wg, wu, wd, l, tmm)
    return _final_norm(xs, _row(final_norm_g), n, tm).reshape(batch, seq, d)
```

```python
import functools

import jax
import jax.numpy as jnp
from jax import lax
from jax.experimental import pallas as pl
from jax.experimental.pallas import tpu as pltpu

F32 = jnp.float32
BF16 = jnp.bfloat16
I32 = jnp.int32

EPS = 1e-6
LANES = 128
SUBLANES = 8
FOX_HEADS = 8
FOX_HEAD_DIM = 64
LOG2E = 1.4426950408889634
FOX_BLOCK = 512
FOX_SKIP_EXPONENT = -34.0
FOX_NORM_SLACK = 1.02
HGRN_HEADS = 4
HGRN_DIM = 64
HGRN_CHUNK = 16
HGRN_BLOCK = 128
CONV_WIDTH = 31
CONV_HALO = 32
CONV_GROUP = 64
XATT_HEADS = 4
N_GROUPS = 4
EXPERTS_PER_GROUP = 4
ROUTER_ROWS = 32
DMA_UNROLL = 8
VMEM_LIMIT_BYTES = 56 * 1024 * 1024

NT_DIMS = (((1,), (1,)), ((), ()))


def _cparams(*sem):
    return pltpu.CompilerParams(dimension_semantics=sem, vmem_limit_bytes=VMEM_LIMIT_BYTES)


def _dot(a, b):
    return jnp.dot(a, b, preferred_element_type=F32)


def _dot_nt(a, b):
    return lax.dot_general(a, b, NT_DIMS, preferred_element_type=F32)


def _rms(x, g):
    ms = jnp.mean(x * x, axis=-1, keepdims=True)
    return x * lax.rsqrt(ms + EPS) * g


def _sigmoid(x):
    return 1.0 / (1.0 + jnp.exp(-x))


def _split2(x):
    hi = x.astype(BF16)
    lo = (x - hi.astype(F32)).astype(BF16)
    return hi, lo


def _split3(x):
    hi = x.astype(BF16)
    r = x - hi.astype(F32)
    mid = r.astype(BF16)
    lo = (r - mid.astype(F32)).astype(BF16)
    return hi, mid, lo


def _halves_select(lane_lo_mask, a, b):
    return jnp.where(lane_lo_mask, a, b)


def _load_rows(ref, rows):
    if ref.shape[0] == rows:
        return ref[...]
    sub = ref.shape[0] // rows
    return jnp.concatenate([ref[pl.ds(c, rows, stride=sub), :] for c in range(sub)], axis=1)


def _store_rows(ref, val):
    rows = val.shape[0]
    if ref.shape[0] == rows:
        ref[...] = val
    else:
        sub = ref.shape[0] // rows
        for c in range(sub):
            ref[pl.ds(c, rows, stride=sub), :] = val[:, c * LANES:(c + 1) * LANES]


def _pair_rms(o, lo_mask, width):
    o2 = o * o
    s_all = jnp.sum(o2, axis=-1, keepdims=True)
    s_lo = jnp.sum(jnp.where(lo_mask, o2, 0.0), axis=-1, keepdims=True)
    ms = jnp.where(lo_mask, s_lo, s_all - s_lo) * (1.0 / width)
    return o * lax.rsqrt(ms + EPS)


def _mem_kv_kernel(mem_ref, g_ref, w_ref, o_ref):
    t = _rms(mem_ref[...], g_ref[...]).astype(BF16)
    o_ref[...] = _dot(t, w_ref[...]).astype(BF16)


def _mem_kv(mem2d, g, w_xkv):
    depth, d, d2 = w_xkv.shape
    rows = mem2d.shape[0]
    return pl.pallas_call(
        _mem_kv_kernel,
        grid=(depth,),
        in_specs=[pl.BlockSpec((rows, d), lambda l: (0, 0)),
                  pl.BlockSpec((1, d), lambda l: (0, 0)),
                  pl.BlockSpec((None, d, d2), lambda l: (l, 0, 0))],
        out_specs=pl.BlockSpec((None, rows, d2), lambda l: (l, 0, 0)),
        out_shape=jax.ShapeDtypeStruct((depth, rows, d2), BF16),
        compiler_params=_cparams("arbitrary"),
        name="mem_kv",
    )(mem2d, g, w_xkv)


def _head_norm_max(y, ind):
    n2 = _dot((y[:, :ind.shape[0]] * y[:, :ind.shape[0]]).astype(BF16), ind)
    top = jnp.broadcast_to(jnp.max(n2, axis=0, keepdims=True), (FOX_HEADS, LANES))
    row = lax.broadcasted_iota(I32, (FOX_HEADS, LANES), 0)
    lane = lax.broadcasted_iota(I32, (FOX_HEADS, LANES), 1)
    pick = lambda first: jnp.broadcast_to(
        jnp.sqrt(jnp.sum(jnp.where(lane == row + first, top, 0.0), axis=1, keepdims=True)), (FOX_HEADS, LANES))
    return pick(0), pick(FOX_HEADS)


def _mix_in_kernel(x_ref, g_ref, wqkv_ref, wh_ref, wc_ref, wff_ref, fb_ref, ind_ref,
                   qkv_ref, hy_ref, cy_ref, lf_ref, qn_ref, kn_ref, kn_scr, *, nb):
    t = _rms(_load_rows(x_ref, qkv_ref.shape[0]), g_ref[...]).astype(BF16)
    y = _dot(t, wqkv_ref[...])
    qkv_ref[...] = y.astype(BF16)
    hy_ref[...] = _dot(t, wh_ref[...]).astype(BF16)
    cy_ref[...] = _dot(t, wc_ref[...]).astype(BF16)
    z = _dot_nt(wff_ref[...], t) + fb_ref[...]
    lf_ref[...] = (jnp.minimum(z, 0.0) - jnp.log1p(jnp.exp(-jnp.abs(z)))) * LOG2E
    q_norm, k_norm = _head_norm_max(y, ind_ref[...])
    qn_ref[...] = q_norm

    @pl.when(pl.program_id(0) % nb == 0)
    def _():
        kn_scr[...] = jnp.zeros_like(kn_scr)

    kn_scr[...] = jnp.maximum(kn_scr[...], k_norm)
    kn_ref[...] = kn_scr[...]


def _row_block(a, tm, d):
    sub = 1 if a.shape[1] == d else d // a.shape[1]
    return pl.BlockSpec((tm * sub, a.shape[1]), lambda r: (r, 0))


def _layer_block(a, layer):
    return pl.BlockSpec((None,) + a.shape[1:], lambda r: (layer,) + (0,) * (a.ndim - 1))


def _mix_in(x2d, g, wqkv, wh, wc, wff, fb, layer, batch, seq, tm):
    n = batch * seq
    nb = seq // tm
    full = lambda a: pl.BlockSpec(a.shape, lambda r: (0,) * a.ndim)
    stacked = lambda a: _layer_block(a, layer)
    qk_cols = 2 * FOX_HEADS * FOX_HEAD_DIM
    ind = (jnp.arange(qk_cols)[:, None] // FOX_HEAD_DIM == jnp.arange(LANES)[None, :]).astype(BF16)
    return pl.pallas_call(
        functools.partial(_mix_in_kernel, nb=nb),
        grid=(n // tm,),
        in_specs=[_row_block(x2d, tm, wqkv.shape[1]), full(g), stacked(wqkv), stacked(wh), stacked(wc),
                  stacked(wff), full(fb), full(ind)],
        out_specs=[pl.BlockSpec((tm, wqkv.shape[2]), lambda r: (r, 0)),
                   pl.BlockSpec((tm, wh.shape[2]), lambda r: (r, 0)),
                   pl.BlockSpec((tm, wc.shape[2]), lambda r: (r, 0)),
                   pl.BlockSpec((None, FOX_HEADS, tm), lambda r: (r // nb, 0, r % nb)),
                   pl.BlockSpec((None, FOX_HEADS, LANES), lambda r: (r, 0, 0)),
                   pl.BlockSpec((None, FOX_HEADS, LANES), lambda r: (r // nb, 0, 0))],
        out_shape=[jax.ShapeDtypeStruct((n, wqkv.shape[2]), BF16),
                   jax.ShapeDtypeStruct((n, wh.shape[2]), BF16),
                   jax.ShapeDtypeStruct((n, wc.shape[2]), BF16),
                   jax.ShapeDtypeStruct((batch, FOX_HEADS, seq), F32),
                   jax.ShapeDtypeStruct((n // tm, FOX_HEADS, LANES), F32),
                   jax.ShapeDtypeStruct((batch, FOX_HEADS, LANES), F32)],
        scratch_shapes=[pltpu.VMEM((FOX_HEADS, LANES), F32)],
        compiler_params=_cparams("arbitrary"),
        name="mix_in",
    )(x2d, g, wqkv, wh, wc, wff, fb, ind)


def _fox_plan_kernel(lf_ref, qn_ref, kn_ref, c_ref, keep_ref, *, blk, stat_rows):
    seq = lf_ref.shape[-1]
    nblk = seq // blk
    r = lax.broadcasted_iota(I32, (blk, blk), 0)
    c = lax.broadcasted_iota(I32, (blk, blk), 1)
    upper = jnp.where(r <= c, 1.0, 0.0).astype(BF16)
    lane = lax.broadcasted_iota(I32, (FOX_HEADS, LANES), 1)
    carry = jnp.zeros((FOX_HEADS, 1), F32)
    first = jnp.zeros((FOX_HEADS, LANES), F32)
    last = jnp.zeros((FOX_HEADS, LANES), F32)
    qn = jnp.zeros((FOX_HEADS, LANES), F32)
    for b in range(nblk):
        x = lf_ref[:, b * blk:(b + 1) * blk]
        hi, mid, lo = _split3(x)
        cb = _dot(hi, upper) + _dot(mid, upper) + _dot(lo, upper) + carry
        for h in range(FOX_HEADS):
            c_ref[h, :, b * blk:(b + 1) * blk] = cb[h:h + 1, :]
        carry = cb[:, blk - 1:blk]
        first = jnp.where(lane == b, cb[:, 0:1], first)
        last = jnp.where(lane == b, carry, last)
        qn = jnp.where(lane == b, qn_ref[b * blk // stat_rows], qn)
    bound = FOX_NORM_SLACK * 2.0 * qn * kn_ref[...] + first
    keep = jnp.zeros((FOX_HEADS, LANES), I32)
    for i in range(nblk):
        live = (bound[:, i:i + 1] - last >= FOX_SKIP_EXPONENT * LOG2E) & (lane < i)
        keep = jnp.where(lane == i, jnp.sum(live.astype(I32), axis=1, keepdims=True), keep)
    keep_ref[...] = keep


def _fox_plan(lf, qn, kn, blk, stat_rows):
    batch, heads, seq = lf.shape
    assert seq // blk <= LANES
    return pl.pallas_call(
        functools.partial(_fox_plan_kernel, blk=blk, stat_rows=stat_rows),
        grid=(batch,),
        in_specs=[pl.BlockSpec((None, heads, seq), lambda b: (b, 0, 0)),
                  pl.BlockSpec((seq // stat_rows, heads, LANES), lambda b: (b, 0, 0)),
                  pl.BlockSpec((None, heads, LANES), lambda b: (b, 0, 0))],
        out_specs=[pl.BlockSpec((None, heads, 1, seq), lambda b: (b, 0, 0, 0)),
                   pl.BlockSpec((None, heads, LANES), lambda b: (b, 0, 0))],
        out_shape=[jax.ShapeDtypeStruct((batch, heads, 1, seq), F32),
                   jax.ShapeDtypeStruct((batch, heads, LANES), I32)],
        compiler_params=_cparams("arbitrary"),
        name="fox_plan",
    )(lf, qn, kn)


def _fox_kernel(keep_ref, q_ref, k_ref, v_ref, c_ref, g_ref, o_ref, *, tq):
    head0 = (pl.program_id(0) * FOX_HEADS + 2 * pl.program_id(1)) * LANES
    lane = lax.broadcasted_iota(I32, (1, LANES), 1)
    lo_mask = lane < FOX_HEAD_DIM
    causal = lax.broadcasted_iota(I32, (tq, tq), 0) >= lax.broadcasted_iota(I32, (tq, tq), 1)

    i = pl.program_id(2)
    n_prev = jnp.maximum(keep_ref[head0 + i], keep_ref[head0 + LANES + i])
    q = q_ref[...]
    zero = jnp.zeros_like(q)
    qh = (jnp.where(lo_mask, q, zero), jnp.where(lo_mask, zero, q))
    q0 = pl.multiple_of(i * tq, tq)
    cq0 = tuple(c_ref[h, :, pl.ds(q0, LANES)][:, 0:1] for h in range(2))

    def block(j, carry, masked):
        k0 = pl.multiple_of(j * tq, tq)
        kb = k_ref[pl.ds(k0, tq), :]
        vb = v_ref[pl.ds(k0, tq), :]
        s = [_dot_nt(qh[h], kb) + (cq0[h] - c_ref[h, :, pl.ds(k0, tq)]) for h in range(2)]
        out = []
        for h in range(2):
            m, l, acc = carry[h]
            sh = jnp.where(causal, s[h], -jnp.inf) if masked else s[h]
            m_new = jnp.maximum(m, jnp.max(sh, axis=-1, keepdims=True))
            alpha = jnp.exp2(m - m_new)
            p = jnp.exp2(sh - m_new)
            l = alpha * l + jnp.sum(p, axis=-1, keepdims=True)
            acc = alpha * acc + _dot(p.astype(BF16), vb)
            out.append((m_new, l, acc))
        return tuple(out)

    init = tuple((jnp.full((tq, 1), -jnp.inf, F32), jnp.zeros((tq, 1), F32), jnp.zeros((tq, LANES), F32))
                 for _ in range(2))
    carry = block(i, init, True)
    carry = lax.fori_loop(i - n_prev, i, lambda j, c: block(j, c, False), carry)
    (_, l0, a0), (_, l1, a1) = carry
    o = jnp.where(lo_mask, a0 * (1.0 / l0), a1 * (1.0 / l1))
    o_ref[...] = (_pair_rms(o, lo_mask, FOX_HEAD_DIM) * g_ref[...]).astype(o_ref.dtype)


def _fox_attention(keep, qkv, c, gain, batch, seq, tq):
    n = qkv.shape[0]
    pairs = FOX_HEADS // 2
    rows = tq
    nq = seq // rows
    grid_spec = pltpu.PrefetchScalarGridSpec(
        num_scalar_prefetch=1,
        grid=(batch, pairs, nq),
        in_specs=[pl.BlockSpec((rows, LANES), lambda b, p, i, keep: (b * nq + i, p)),
                  pl.BlockSpec((seq, LANES), lambda b, p, i, keep: (b, pairs + p)),
                  pl.BlockSpec((seq, LANES), lambda b, p, i, keep: (b, 2 * pairs + p)),
                  pl.BlockSpec((None, 2, 1, seq), lambda b, p, i, keep: (b, p, 0, 0)),
                  pl.BlockSpec((1, LANES), lambda b, p, i, keep: (0, p))],
        out_specs=pl.BlockSpec((rows, LANES), lambda b, p, i, keep: (b * nq + i, p)),
    )
    return pl.pallas_call(
        functools.partial(_fox_kernel, tq=tq),
        grid_spec=grid_spec,
        out_shape=jax.ShapeDtypeStruct((n, pairs * LANES), BF16),
        compiler_params=_cparams("parallel", "parallel", "arbitrary"),
        name="fox_attention",
    )(keep, qkv, qkv, qkv, c, gain)


def _hgrn_kernel(q_ref, f_ref, v_ref, gate_ref, lbz_ref, gain_ref, o_ref, st_ref, u_scr, prev_scr, *, layer, n_sub):
    T, C = HGRN_BLOCK, HGRN_CHUNK
    nchunk = T // C
    i = pl.program_id(2)

    @pl.when(i == 0)
    def _():
        st_ref[...] = jnp.zeros_like(st_ref)

    z = lbz_ref[...]
    e = jnp.exp(z - jnp.max(z, axis=0, keepdims=True))
    pz = e / jnp.sum(e, axis=0, keepdims=True)
    lb = jnp.zeros((1, LANES), F32)
    for j in range(1, layer + 1):
        lb = lb + pz[j:j + 1, :]

    lane = lax.broadcasted_iota(I32, (1, LANES), 1)
    lo_mask = lane < HGRN_DIM
    r = lax.broadcasted_iota(I32, (T, T), 0)
    c = lax.broadcasted_iota(I32, (T, T), 1)
    same_chunk = (r // C) == (c // C)
    one = lambda m: jnp.where(m, 1.0, 0.0).astype(BF16)
    scan_mat = jnp.concatenate([one(same_chunk & (c <= r)),
                                one(same_chunk & ((c % C) <= C // 2)),
                                one(same_chunk)], axis=0)
    intra_mask = same_chunk & (c <= r)
    chunk_of_row = lax.broadcasted_iota(I32, (T, LANES), 0) // C
    vr = lax.broadcasted_iota(I32, (LANES, nchunk * LANES), 0)
    kc = lax.broadcasted_iota(I32, (LANES, nchunk * LANES), 1)
    same_head = (vr < HGRN_DIM) == ((kc % LANES) < HGRN_DIM)

    subs = range(n_sub)
    rows = [pl.ds(sb * T, T) for sb in subs]
    q = [q_ref[r_, :].astype(F32) for r_ in rows]
    f = [lb + (1.0 - lb) * _sigmoid(f_ref[r_, :].astype(F32)) for r_ in rows]
    kk = [1.0 - f_ for f_ in f]
    parts = []
    for f_ in f:
        parts.extend(_split2(jnp.log(f_)))
    sc = _dot(scan_mat, jnp.concatenate(parts, axis=1))
    sc = [sc[:, (2 * sb) * LANES:(2 * sb + 1) * LANES] + sc[:, (2 * sb + 1) * LANES:(2 * sb + 2) * LANES]
          for sb in subs]
    b = [s_[:T] for s_ in sc]
    b_mid = [s_[T:2 * T] for s_ in sc]
    b_last = [s_[2 * T:] for s_ in sc]
    v = [v_ref[r_, :] for r_ in rows]
    for sb in subs:
        k_out = kk[sb] * jnp.exp(b_last[sb] - b[sb])
        k_exp = jnp.concatenate([jnp.where(chunk_of_row == j, k_out, 0.0) for j in range(nchunk)],
                                axis=1).astype(BF16)
        v_t = v[sb].astype(F32).T.astype(BF16)
        u_scr[sb] = jnp.where(same_head, _dot(v_t, k_exp), 0.0)
    att = []
    for sb in subs:
        q_in = (q[sb] * jnp.exp(b[sb] - b_mid[sb])).astype(BF16)
        k_in = (kk[sb] * jnp.exp(b_mid[sb] - b[sb])).astype(BF16)
        zq = jnp.zeros_like(q_in)
        att.append([jnp.where(intra_mask, _dot_nt(qm, k_in), 0.0).astype(BF16)
                    for qm in (jnp.where(lo_mask, q_in, zq), jnp.where(lo_mask, zq, q_in))])
    o = [jnp.where(lo_mask, _dot(att[sb][0], v[sb]), _dot(att[sb][1], v[sb])) for sb in subs]
    state = st_ref[...]
    for sb in subs:
        decay = jnp.exp(b_last[sb])
        for j in range(nchunk):
            prev_scr[sb, j * LANES:(j + 1) * LANES, :] = state.astype(BF16)
            state = state * decay[j * C:j * C + 1, :] + u_scr[sb, :, j * LANES:(j + 1) * LANES]
    st_ref[...] = state
    for sb in subs:
        q_out = (q[sb] * jnp.exp(b[sb])).astype(BF16)
        o_all = _dot_nt(q_out, prev_scr[sb])
        acc = o[sb]
        for j in range(nchunk):
            acc = acc + jnp.where(chunk_of_row == j, o_all[:, j * LANES:(j + 1) * LANES], 0.0)
        gate = gate_ref[rows[sb], :].astype(F32)
        y = _pair_rms(acc, lo_mask, HGRN_DIM) * gain_ref[...] * (gate * _sigmoid(gate))
        o_ref[rows[sb], :] = y.astype(o_ref.dtype)


def _hgrn(hy, lb_logits, gain, layer, batch, seq, tg):
    n = hy.shape[0]
    pairs = HGRN_HEADS // 2
    nb = seq // tg
    depth = lb_logits.shape[0]
    col = lambda k: pl.BlockSpec((tg, LANES), lambda b, p, i, k=k: (b * nb + i, k * pairs + p))
    n_sub = tg // HGRN_BLOCK
    states = (HGRN_BLOCK // HGRN_CHUNK) * LANES
    return pl.pallas_call(
        functools.partial(_hgrn_kernel, layer=layer, n_sub=n_sub),
        grid=(batch, pairs, nb),
        in_specs=[col(0), col(1), col(2), col(3),
                  pl.BlockSpec((depth, LANES), lambda b, p, i: (0, p)),
                  pl.BlockSpec((1, LANES), lambda b, p, i: (0, p))],
        out_specs=pl.BlockSpec((tg, LANES), lambda b, p, i: (b * nb + i, p)),
        out_shape=jax.ShapeDtypeStruct((n, pairs * LANES), BF16),
        scratch_shapes=[pltpu.VMEM((LANES, LANES), F32),
                        pltpu.VMEM((n_sub, LANES, states), F32),
                        pltpu.VMEM((n_sub, states, LANES), BF16)],
        compiler_params=_cparams("parallel", "parallel", "arbitrary"),
        name="hgrn",
    )(hy, hy, hy, hy, lb_logits, gain)


def _conv_kernel(cu_ref, cg_ref, w_ref, b_ref, ng_ref, nb_ref, o_ref, a_scr, sh_scr, *, tm):
    i = pl.program_id(1)
    sub = SUBLANES

    @pl.when(i == 0)
    def _():
        a_scr[0:CONV_HALO, :] = jnp.zeros((CONV_HALO, a_scr.shape[1]), F32)

    @pl.when(i > 0)
    def _():
        a_scr[0:CONV_HALO, :] = a_scr[tm:tm + CONV_HALO, :]

    a_scr[CONV_HALO:CONV_HALO + tm, :] = cu_ref[...].astype(F32) * _sigmoid(cg_ref[...].astype(F32))
    span = CONV_HALO + tm - sub
    for k in range(1, sub):
        sh_scr[k - 1, 0:span, :] = a_scr[k:k + span, :]
    acc = jnp.zeros((tm, a_scr.shape[1]), F32) + b_ref[...]
    first = CONV_HALO - (CONV_WIDTH - 1)
    for w in range(CONV_WIDTH):
        base, k = divmod(first + w, sub)
        src = a_scr if k == 0 else sh_scr.at[k - 1]
        acc = acc + src[base * sub:base * sub + tm, :] * w_ref[w:w + 1, :]
    lane = lax.broadcasted_iota(I32, (1, LANES), 1)
    lo_mask = lane < CONV_GROUP
    halves = []
    for hh in range(acc.shape[1] // LANES):
        xh = acc[:, hh * LANES:(hh + 1) * LANES]
        s_all = jnp.sum(xh, axis=-1, keepdims=True)
        s_lo = jnp.sum(jnp.where(lo_mask, xh, 0.0), axis=-1, keepdims=True)
        d = xh - jnp.where(lo_mask, s_lo, s_all - s_lo) * (1.0 / CONV_GROUP)
        halves.append(_pair_rms(d, lo_mask, CONV_GROUP))
    y = jnp.concatenate(halves, axis=1) * ng_ref[...] + nb_ref[...]
    o_ref[...] = (y * _sigmoid(y)).astype(o_ref.dtype)


def _conv(cy, w, b, ng, nb_, batch, seq, tm):
    n = cy.shape[0]
    ch = cy.shape[1] // 2
    nb = seq // tm
    full = lambda a: pl.BlockSpec(a.shape, lambda bb, i: (0,) * a.ndim)
    return pl.pallas_call(
        functools.partial(_conv_kernel, tm=tm),
        grid=(batch, nb),
        in_specs=[pl.BlockSpec((tm, ch), lambda bb, i: (bb * nb + i, 0)),
                  pl.BlockSpec((tm, ch), lambda bb, i: (bb * nb + i, 1)),
                  full(w), full(b), full(ng), full(nb_)],
        out_specs=pl.BlockSpec((tm, ch), lambda bb, i: (bb * nb + i, 0)),
        out_shape=jax.ShapeDtypeStruct((n, ch), BF16),
        scratch_shapes=[pltpu.VMEM((CONV_HALO + tm, ch), F32),
                        pltpu.VMEM((SUBLANES - 1, CONV_HALO + tm, ch), F32)],
        compiler_params=_cparams("parallel", "arbitrary"),
        name="conv",
    )(cy, cy, w, b, ng, nb_)


def _router_logits_t(t, wr_hi, wr_lo, br):
    th, tl = _split2(t)
    return _dot_nt(wr_hi, th) + _dot_nt(wr_lo, th) + _dot_nt(wr_hi, tl) + br


def _xattn_kernel(fo_ref, ho_ref, co_ref, x_ref, wo_ref, gx_ref, wq_ref, kv_ref, wxo_ref, g_ref,
                  wrh_ref, wrl_ref, br_ref, x2_ref, route_ref, cnt_ref, cnt_scr, *, tm):
    step = pl.program_id(0)

    @pl.when(step == 0)
    def _():
        cnt_scr[...] = jnp.zeros_like(cnt_scr)

    nf, nh = fo_ref.shape[1], ho_ref.shape[1]
    x1 = (_load_rows(x_ref, tm) + _dot(fo_ref[...], wo_ref[0:nf, :]) + _dot(ho_ref[...], wo_ref[nf:nf + nh, :])
          + _dot(co_ref[...], wo_ref[nf + nh:, :]))
    xq = _dot(_rms(x1, gx_ref[...]).astype(BF16), wq_ref[...]).astype(BF16)
    d = xq.shape[1]
    hd = d // XATT_HEADS
    scores = [_dot_nt(xq[:, h * hd:(h + 1) * hd], kv_ref[:, h * hd:(h + 1) * hd]) for h in range(XATT_HEADS)]
    probs = [jnp.exp(s - jnp.max(s, axis=-1, keepdims=True)) for s in scores]
    heads = []
    for h, p in enumerate(probs):
        vh = kv_ref[:, d + h * hd:d + (h + 1) * hd]
        l = jnp.sum(p, axis=-1, keepdims=True)
        heads.append((_dot(p.astype(BF16), vh) * (1.0 / l)).astype(BF16))
    x2 = x1 + _dot(jnp.concatenate(heads, axis=1), wxo_ref[...])
    _store_rows(x2_ref, x2)
    logits = _router_logits_t(_rms(x2, g_ref[...]), wrh_ref[...], wrl_ref[...], br_ref[...])
    gl = [logits[g:g + 1, :] for g in range(N_GROUPS)]
    gmax = jnp.maximum(jnp.maximum(gl[0], gl[1]), jnp.maximum(gl[2], gl[3]))
    gidx = jnp.where(gl[0] >= gmax, 0, jnp.where(gl[1] >= gmax, 1, jnp.where(gl[2] >= gmax, 2, 3)))
    rows8 = lax.broadcasted_iota(I32, (8, tm), 0)
    onehot = jnp.where(rows8 == gidx, 1.0, 0.0)
    r = lax.broadcasted_iota(I32, (tm, tm), 0)
    c = lax.broadcasted_iota(I32, (tm, tm), 1)
    incl = _dot(onehot.astype(BF16), jnp.where(r <= c, 1.0, 0.0).astype(BF16))
    before = cnt_scr[:, 0:1]
    rank = jnp.sum(onehot * (incl - 1.0 + before), axis=0, keepdims=True)
    route_ref[...] = jnp.where(rows8 == 0, gidx, jnp.where(rows8 == 1, rank.astype(I32), 0))
    cnt_scr[...] = cnt_scr[...] + incl[:, tm - 1:tm]
    cnt_ref[...] = cnt_scr[...].astype(I32)


def _xattn(fo, ho, co, x2d, wo, gx, wq, kv, wxo, g, wrh, wrl, br, layer, batch, seq, tm):
    n, d = fo.shape[0], wo.shape[2]
    nb = seq // tm
    mem = kv.shape[2]
    full = lambda a: pl.BlockSpec(a.shape, lambda r: (0,) * a.ndim)
    stacked = lambda a: _layer_block(a, layer)
    rowblk = lambda a: _row_block(a, tm, d if a is x2d else a.shape[1])
    return pl.pallas_call(
        functools.partial(_xattn_kernel, tm=tm),
        grid=(n // tm,),
        in_specs=[rowblk(fo), rowblk(ho), rowblk(co), rowblk(x2d), stacked(wo), full(gx), stacked(wq),
                  pl.BlockSpec((None, None, mem, 2 * d), lambda r: (layer, r // nb, 0, 0)),
                  stacked(wxo), full(g), full(wrh), full(wrl), full(br)],
        out_specs=[pl.BlockSpec((tm * (d // LANES), LANES), lambda r: (r, 0)),
                   pl.BlockSpec((8, tm), lambda r: (0, r)),
                   pl.BlockSpec((8, LANES), lambda r: (0, 0))],
        out_shape=[jax.ShapeDtypeStruct((n * (d // LANES), LANES), F32),
                   jax.ShapeDtypeStruct((8, n), I32),
                   jax.ShapeDtypeStruct((8, LANES), I32)],
        scratch_shapes=[pltpu.VMEM((8, LANES), F32)],
        compiler_params=_cparams("arbitrary"),
        name="xattn",
    )(fo, ho, co, x2d, wo, gx, wq, kv, wxo, g, wrh, wrl, br)


def _plan_kernel(cnt_ref, route_ref, src_ref, dst_ref, tg_ref, dest_vmem, dest_smem, sem, *, n_tok, tmm, sub):
    n_slots = src_ref.shape[0]
    n_tiles = tg_ref.shape[0] - 1
    shift = tmm.bit_length() - 1
    assert tmm == 1 << shift

    def clear(s, _):
        src_ref[s] = 0
        parity = lax.shift_right_logical(s, shift) & 1
        dst_ref[s] = (n_tok + parity * tmm + (s & (tmm - 1))) * sub
        return 0

    off = jnp.int32(0)
    last_group = jnp.int32(0)
    starts, ends = [], []
    for g in range(N_GROUPS):
        cnt = cnt_ref[g]
        padded = lax.shift_left(lax.shift_right_logical(cnt + (tmm - 1), shift), shift)
        lax.fori_loop(off + cnt, off + padded, clear, 0)
        starts.append(off)
        off = off + padded
        ends.append(off)
        last_group = jnp.where(cnt > 0, g, last_group)
    lax.fori_loop(off, n_slots, clear, 0)

    gidx = route_ref[0:1, :]
    rank = route_ref[1:2, :]
    start = jnp.where(gidx == 0, starts[0], jnp.where(gidx == 1, starts[1],
                                                      jnp.where(gidx == 2, starts[2], starts[3])))
    dest_vmem[...] = start + rank
    to_smem = pltpu.make_async_copy(dest_vmem, dest_smem, sem)
    to_smem.start()
    to_smem.wait()

    def place(t, _):
        slot = dest_smem[0, t]
        src_ref[slot] = t * sub
        dst_ref[slot] = t * sub
        return 0

    lax.fori_loop(0, n_tok, place, 0, unroll=8)

    def tile(k, _):
        start = k * tmm
        g = ((start >= ends[0]).astype(I32) + (start >= ends[1]).astype(I32) + (start >= ends[2]).astype(I32))
        tg_ref[k] = jnp.minimum(g, last_group)
        return 0

    lax.fori_loop(0, n_tiles, tile, 0)
    tg_ref[n_tiles] = lax.shift_right_logical(off, shift)


def _route_plan(cnt, route, n_tok, tmm, sub):
    n_tiles = n_tok // tmm + N_GROUPS
    smem = pl.BlockSpec(memory_space=pltpu.SMEM)
    slots = jax.ShapeDtypeStruct((n_tiles * tmm,), I32)
    return pl.pallas_call(
        functools.partial(_plan_kernel, n_tok=n_tok, tmm=tmm, sub=sub),
        in_specs=[smem, pl.BlockSpec(memory_space=pltpu.VMEM)],
        out_specs=[smem, smem, smem],
        out_shape=[slots, slots, jax.ShapeDtypeStruct((n_tiles + 1,), I32)],
        scratch_shapes=[pltpu.VMEM((1, n_tok), I32), pltpu.SMEM((1, n_tok), I32), pltpu.SemaphoreType.DMA(())],
        name="route_plan",
    )(cnt, route)


def _moe_kernel(src_ref, dst_ref, tg_ref, x2_hbm, g_ref, wrh_ref, wrl_ref, br_ref, wg_ref, wu_ref, wd_ref,
                x3_hbm, xb, ob, wg_s, wu_s, wd_s, gsem, ssem, zsem, *, tmm, n_tok):
    k = pl.program_id(0)
    n_active = tg_ref[pl.num_programs(0)]
    slot = k % 2

    @pl.when((k == 0) | (tg_ref[k] != tg_ref[jnp.maximum(k - 1, 0)]))
    def _():
        ff = wd_ref.shape[1]
        for e in range(EXPERTS_PER_GROUP):
            wg_s[e] = wg_ref[e].astype(BF16)
            wu_s[e] = wu_ref[e].astype(BF16)
            wd_s[e * ff:(e + 1) * ff, :] = wd_ref[e].astype(BF16)

    sub = xb.shape[1] // tmm

    def rows_at(first):
        return pl.ds(pl.multiple_of(first, sub), sub)

    def gather_copy(tile, sl, r):
        return pltpu.make_async_copy(x2_hbm.at[rows_at(src_ref[tile * tmm + r])], xb.at[sl, rows_at(r * sub)],
                                     gsem.at[sl])

    def scatter_copy(tile, sl, r):
        return pltpu.make_async_copy(ob.at[sl, rows_at(r * sub)], x3_hbm.at[rows_at(dst_ref[tile * tmm + r])],
                                     ssem.at[sl])

    def for_rows(fn):
        def body(r8, _):
            for u in range(DMA_UNROLL):
                fn(r8 * DMA_UNROLL + u, u % 2)
            return 0
        lax.fori_loop(0, tmm // DMA_UNROLL, body, 0)

    def dump_fill(half):
        return pltpu.make_async_copy(ob.at[1], x3_hbm.at[pl.ds((n_tok + half * tmm) * sub, tmm * sub)], zsem)

    @pl.when(k == 0)
    def _():
        ob[1] = jnp.zeros(ob.shape[1:], F32)
        dump_fill(0).start()
        dump_fill(1).start()
        for_rows(lambda r, pri: gather_copy(0, 0, r).start(priority=pri))

    @pl.when(k < n_active)
    def _():
        for_rows(lambda r, pri: gather_copy(k, slot, r).wait())

        @pl.when(k + 1 < n_active)
        def _():
            for_rows(lambda r, pri: gather_copy(k + 1, 1 - slot, r).start(priority=pri))

        @pl.when(k >= 2)
        def _():
            for_rows(lambda r, pri: scatter_copy(k - 2, slot, r).wait())

        y = _moe_tile(_load_rows(xb.at[slot], tmm), tg_ref[k], g_ref, wrh_ref, wrl_ref, br_ref,
                      wg_s, wu_s, wd_s)
        _store_rows(ob.at[slot], y)

        @pl.when(k == 0)
        def _():
            dump_fill(0).wait()
            dump_fill(1).wait()

        for_rows(lambda r, pri: scatter_copy(k, slot, r).start(priority=pri))

        @pl.when(k == n_active - 1)
        def _():
            @pl.when(k >= 1)
            def _():
                for_rows(lambda r, pri: scatter_copy(k - 1, 1 - slot, r).wait())
            for_rows(lambda r, pri: scatter_copy(k, slot, r).wait())


def _moe_tile(x, grp, g_ref, wrh_ref, wrl_ref, br_ref, wg_ref, wu_ref, wd_ref):
    t = _rms(x, g_ref[...])
    tb = t.astype(BF16)
    th, tl = _split2(t)
    logits = _dot(th, wrh_ref[...]) + _dot(th, wrl_ref[...]) + _dot(tl, wrh_ref[...]) + br_ref[...]
    lane = lax.broadcasted_iota(I32, (1, LANES), 1)
    lanef = lane.astype(F32)
    rmax = lambda a: jnp.max(a, axis=-1, keepdims=True)
    rsum = lambda a: jnp.sum(a, axis=-1, keepdims=True)
    gmask = lane < N_GROUPS
    gmax = rmax(jnp.where(gmask, logits, -jnp.inf))
    zg = rsum(jnp.where(gmask, jnp.exp(logits - gmax), 0.0))
    p_group = jnp.exp(rsum(jnp.where(lane == grp, logits, 0.0)) - gmax) / zg
    e_lo = N_GROUPS + EXPERTS_PER_GROUP * grp
    emask = (lane >= e_lo) & (lane < e_lo + EXPERTS_PER_GROUP)
    em = jnp.where(emask, logits, -jnp.inf)
    e1 = rmax(em)
    i1 = jnp.min(jnp.where(em == e1, lanef, 1e9), axis=-1, keepdims=True)
    em2 = jnp.where(lanef == i1, -jnp.inf, em)
    e2 = rmax(em2)
    i2 = jnp.min(jnp.where(em2 == e2, lanef, 1e9), axis=-1, keepdims=True)
    ze = rsum(jnp.where(emask, jnp.exp(logits - e1), 0.0))
    p1 = 1.0 / ze
    p2 = jnp.exp(e2 - e1) / ze
    w1 = p_group * (p1 / (p1 + p2))
    w2 = p_group * (p2 / (p1 + p2))
    gates = [_dot(tb, wg_ref[e]) for e in range(EXPERTS_PER_GROUP)]
    ups = [_dot(tb, wu_ref[e]) for e in range(EXPERTS_PER_GROUP)]
    cw_lanes = jnp.where(lanef == i1, w1, 0.0) + jnp.where(lanef == i2, w2, 0.0)
    parts = []
    for e in range(EXPERTS_PER_GROUP):
        cw = rsum(jnp.where(lane == e_lo + e, cw_lanes, 0.0))
        parts.append((gates[e] * _sigmoid(gates[e]) * ups[e] * cw).astype(BF16))
    return x + _dot(jnp.concatenate(parts, axis=1), wd_ref[...])


def _moe(src, dst, tg, x2, g, wrh_t, wrl_t, br_row, wg, wu, wd, layer, tmm):
    sub = g.shape[1] // LANES
    n_tok = x2.shape[0] // sub
    n_tiles = tg.shape[0] - 1
    full = lambda a: pl.BlockSpec(a.shape, lambda k, src, dst, tg: (0,) * a.ndim)
    grp = lambda a: pl.BlockSpec((None, None) + a.shape[2:], lambda k, src, dst, tg: (layer, tg[k], 0, 0, 0))
    n_exp, d, ff = wg.shape[2:]
    grid_spec = pltpu.PrefetchScalarGridSpec(
        num_scalar_prefetch=3,
        grid=(n_tiles,),
        in_specs=[pl.BlockSpec(memory_space=pl.ANY), full(g), full(wrh_t), full(wrl_t), full(br_row),
                  grp(wg), grp(wu), grp(wd)],
        out_specs=pl.BlockSpec(memory_space=pl.ANY),
        scratch_shapes=[pltpu.VMEM((2, tmm * sub, LANES), F32), pltpu.VMEM((2, tmm * sub, LANES), F32),
                        pltpu.VMEM((n_exp, d, ff), BF16), pltpu.VMEM((n_exp, d, ff), BF16),
                        pltpu.VMEM((n_exp * ff, d), BF16),
                        pltpu.SemaphoreType.DMA((2,)), pltpu.SemaphoreType.DMA((2,)), pltpu.SemaphoreType.DMA(())],
    )
    return pl.pallas_call(
        functools.partial(_moe_kernel, tmm=tmm, n_tok=n_tok),
        grid_spec=grid_spec,
        out_shape=jax.ShapeDtypeStruct(((n_tok + 2 * tmm) * sub, LANES), F32),
        compiler_params=_cparams("arbitrary"),
        name="moe",
    )(src, dst, tg, x2, g, wrh_t, wrl_t, br_row, wg, wu, wd)


def _final_norm_kernel(x_ref, g_ref, o_ref):
    o_ref[...] = _rms(_load_rows(x_ref, o_ref.shape[0]), g_ref[...])


def _final_norm(xs, g, n, tm):
    d = g.shape[1]
    return pl.pallas_call(
        _final_norm_kernel,
        grid=(n // tm,),
        in_specs=[_row_block(xs, tm, d), pl.BlockSpec((1, d), lambda r: (0, 0))],
        out_specs=pl.BlockSpec((tm, d), lambda r: (r, 0)),
        out_shape=jax.ShapeDtypeStruct((n, d), F32),
        compiler_params=_cparams("arbitrary"),
        name="final_norm",
    )(xs, g)


def _row(v):
    return v.reshape(1, -1)


def kernel(x, mem, mix_norm_g, w_in, fox_f_bias, fox_norm_g, hgrn_lb_logits, hgrn_norm_g, conv_w, conv_b,
           conv_norm_g, conv_norm_b, w_out, xatt_norm_g, mem_norm_g, w_xq, w_xkv, w_xo, ffn_norm_g,
           w_group, b_group, w_router, b_router, w_gate, w_up, w_down, final_norm_g):
    batch, seq, d = x.shape
    depth = w_in.shape[0]
    n = batch * seq
    tm = min(512, seq)
    tmm = min(512, seq)
    fblk = min(FOX_BLOCK, tm)
    fw = FOX_HEADS * FOX_HEAD_DIM
    hw = HGRN_HEADS * HGRN_DIM
    cw = conv_w.shape[-1]
    assert seq % tm == 0 and seq % HGRN_BLOCK == 0 and d == fw + hw + cw

    offs = [0]
    for width in (fw, fw, fw, FOX_HEADS, hw, hw, hw, hw, cw, cw):
        offs.append(offs[-1] + width)
    seg = lambda a, i, j: a[..., offs[i]:offs[j]]
    wqkv = jnp.concatenate([seg(w_in, 0, 1) * (FOX_HEAD_DIM ** -0.5 * LOG2E), seg(w_in, 1, 3)], axis=-1).astype(BF16)
    wff = jnp.swapaxes(seg(w_in, 3, 4), 1, 2).astype(BF16)
    wh = jnp.concatenate([seg(w_in, 4, 5), seg(w_in, 5, 6), seg(w_in, 6, 7), seg(w_in, 7, 8)], axis=-1).astype(BF16)
    wc = seg(w_in, 8, 10).astype(BF16)
    wo = w_out.astype(BF16)
    wxq = (w_xq * ((d // XATT_HEADS) ** -0.5)).astype(BF16)
    wxkv = w_xkv.astype(BF16)
    wxo = w_xo.astype(BF16)
    n_exp = N_GROUPS * EXPERTS_PER_GROUP
    wr = jnp.concatenate([w_group, w_router], axis=-1)
    wr_hi = wr.astype(BF16)
    wr_lo = (wr - wr_hi.astype(F32)).astype(BF16)
    pad_rows = lambda a: jnp.pad(jnp.swapaxes(a, 1, 2), ((0, 0), (0, ROUTER_ROWS - a.shape[2]), (0, 0)))
    pad_cols = lambda a: jnp.pad(a, ((0, 0), (0, 0), (0, LANES - a.shape[2])))
    br = jnp.concatenate([b_group, b_router], axis=-1)
    br_col = jnp.pad(br, ((0, 0), (0, ROUTER_ROWS - br.shape[1])))[:, :, None]
    br_row = jnp.pad(br, ((0, 0), (0, LANES - br.shape[1])))[:, None, :]
    conv_w_pad = jnp.pad(conv_w, ((0, 0), (0, CONV_HALO - CONV_WIDTH), (0, 0)))
    wrh_rows, wrl_rows, wrh_cols, wrl_cols = pad_rows(wr_hi), pad_rows(wr_lo), pad_cols(wr_hi), pad_cols(wr_lo)

    kv_all = _mem_kv(mem.reshape(-1, d), _row(mem_norm_g), wxkv).reshape(depth, batch, mem.shape[1], 2 * d)

    xs = x.reshape(n, d)
    for l in range(depth):
        qkv, hy, cy, lf, qn, kn = _mix_in(xs, _row(mix_norm_g[l]), wqkv, wh, wc, wff,
                                          fox_f_bias[l].reshape(-1, 1), l, batch, seq, tm)
        c, keep = _fox_plan(lf, qn, kn, fblk, tm)
        fo = _fox_attention(keep.reshape(-1), qkv, c, _row(fox_norm_g[l]), batch, seq, fblk)
        ho = _hgrn(hy, hgrn_lb_logits, _row(hgrn_norm_g[l]), l, batch, seq, min(1024, seq))
        co = _conv(cy, conv_w_pad[l], _row(conv_b[l]), _row(conv_norm_g[l]), _row(conv_norm_b[l]), batch, seq, tm)
        x2, route, cnt = _xattn(fo, ho, co, xs, wo, _row(xatt_norm_g[l]), wxq, kv_all, wxo,
                                _row(ffn_norm_g[l]), wrh_rows[l], wrl_rows[l], br_col[l], l, batch, seq, tm)
        src, dst, tg = _route_plan(cnt[:N_GROUPS, 0], route, n, tmm, d // LANES)
        xs = _moe(src, dst, tg, x2, _row(ffn_norm_g[l]), wrh_cols[l], wrl_cols[l], br_row[l],
                  w_gate, w_up, w_down, l, tmm)
    return _final_norm(xs, _row(final_norm_g), n, tm).reshape(batch, seq, d)
```

```python
import functools

import jax
import jax.numpy as jnp
from jax import lax
from jax.experimental import pallas as pl
from jax.experimental.pallas import tpu as pltpu

F32 = jnp.float32
BF16 = jnp.bfloat16
I32 = jnp.int32

EPS = 1e-6
LANES = 128
SUBLANES = 8
FOX_HEADS = 8
FOX_HEAD_DIM = 64
LOG2E = 1.4426950408889634
FOX_BLOCK = 512
FOX_SKIP_EXPONENT = -34.0
FOX_NORM_SLACK = 1.02
HGRN_HEADS = 4
HGRN_DIM = 64
HGRN_CHUNK = 16
HGRN_BLOCK = 128
CONV_WIDTH = 31
CONV_HALO = 32
CONV_GROUP = 64
XATT_HEADS = 4
N_GROUPS = 4
EXPERTS_PER_GROUP = 4
ROUTER_ROWS = 32
DMA_UNROLL = 8
VMEM_LIMIT_BYTES = 56 * 1024 * 1024

NT_DIMS = (((1,), (1,)), ((), ()))


def _cparams(*sem):
    return pltpu.CompilerParams(dimension_semantics=sem, vmem_limit_bytes=VMEM_LIMIT_BYTES)


def _dot(a, b):
    return jnp.dot(a, b, preferred_element_type=F32)


def _dot_nt(a, b):
    return lax.dot_general(a, b, NT_DIMS, preferred_element_type=F32)


def _rms(x, g):
    ms = jnp.mean(x * x, axis=-1, keepdims=True)
    return x * lax.rsqrt(ms + EPS) * g


def _sigmoid(x):
    return 1.0 / (1.0 + jnp.exp(-x))


def _split2(x):
    hi = x.astype(BF16)
    lo = (x - hi.astype(F32)).astype(BF16)
    return hi, lo


def _split3(x):
    hi = x.astype(BF16)
    r = x - hi.astype(F32)
    mid = r.astype(BF16)
    lo = (r - mid.astype(F32)).astype(BF16)
    return hi, mid, lo


def _halves_select(lane_lo_mask, a, b):
    return jnp.where(lane_lo_mask, a, b)


def _load_rows(ref, rows):
    if ref.shape[0] == rows:
        return ref[...]
    sub = ref.shape[0] // rows
    return jnp.concatenate([ref[pl.ds(c, rows, stride=sub), :] for c in range(sub)], axis=1)


def _store_rows(ref, val):
    rows = val.shape[0]
    if ref.shape[0] == rows:
        ref[...] = val
    else:
        sub = ref.shape[0] // rows
        for c in range(sub):
            ref[pl.ds(c, rows, stride=sub), :] = val[:, c * LANES:(c + 1) * LANES]


def _pair_rms(o, lo_mask, width):
    o2 = o * o
    s_all = jnp.sum(o2, axis=-1, keepdims=True)
    s_lo = jnp.sum(jnp.where(lo_mask, o2, 0.0), axis=-1, keepdims=True)
    ms = jnp.where(lo_mask, s_lo, s_all - s_lo) * (1.0 / width)
    return o * lax.rsqrt(ms + EPS)


def _mem_kv_kernel(mem_ref, g_ref, w_ref, o_ref):
    t = _rms(mem_ref[...], g_ref[...]).astype(BF16)
    o_ref[...] = _dot(t, w_ref[...]).astype(BF16)


def _mem_kv(mem2d, g, w_xkv):
    depth, d, d2 = w_xkv.shape
    rows = mem2d.shape[0]
    return pl.pallas_call(
        _mem_kv_kernel,
        grid=(depth,),
        in_specs=[pl.BlockSpec((rows, d), lambda l: (0, 0)),
                  pl.BlockSpec((1, d), lambda l: (0, 0)),
                  pl.BlockSpec((None, d, d2), lambda l: (l, 0, 0))],
        out_specs=pl.BlockSpec((None, rows, d2), lambda l: (l, 0, 0)),
        out_shape=jax.ShapeDtypeStruct((depth, rows, d2), BF16),
        compiler_params=_cparams("arbitrary"),
        name="mem_kv",
    )(mem2d, g, w_xkv)


def _head_norm_max(y, ind):
    n2 = _dot((y[:, :ind.shape[0]] * y[:, :ind.shape[0]]).astype(BF16), ind)
    top = jnp.broadcast_to(jnp.max(n2, axis=0, keepdims=True), (FOX_HEADS, LANES))
    row = lax.broadcasted_iota(I32, (FOX_HEADS, LANES), 0)
    lane = lax.broadcasted_iota(I32, (FOX_HEADS, LANES), 1)
    pick = lambda first: jnp.broadcast_to(
        jnp.sqrt(jnp.sum(jnp.where(lane == row + first, top, 0.0), axis=1, keepdims=True)), (FOX_HEADS, LANES))
    return pick(0), pick(FOX_HEADS)


def _mix_in_kernel(x_ref, g_ref, wqkv_ref, wh_ref, wc_ref, wff_ref, fb_ref, ind_ref, cw_ref, cb_ref, cng_ref, cnb_ref,
                   qkv_ref, hy_ref, co_ref, lf_ref, qn_ref, kn_ref, kn_scr, a_scr, sh_scr, *, nb):
    tm = qkv_ref.shape[0]
    _conv_halo(pl.program_id(0) % nb, a_scr, tm)
    t = _rms(_load_rows(x_ref, tm), g_ref[...]).astype(BF16)
    c = _dot(t, wc_ref[...])
    ch = c.shape[1] // 2
    acc = _conv_stage(c[:, :ch], c[:, ch:], cb_ref, a_scr, sh_scr, tm)
    acc = _conv_taps(acc, range(CONV_WIDTH), cw_ref, a_scr, sh_scr, tm)
    co_ref[...] = _conv_finish(acc, cng_ref, cnb_ref).astype(co_ref.dtype)
    y = _dot(t, wqkv_ref[...])
    qkv_ref[...] = y.astype(BF16)
    hy_ref[...] = _dot(t, wh_ref[...]).astype(BF16)
    z = _dot_nt(wff_ref[...], t) + fb_ref[...]
    lf_ref[...] = (jnp.minimum(z, 0.0) - jnp.log1p(jnp.exp(-jnp.abs(z)))) * LOG2E
    q_norm, k_norm = _head_norm_max(y, ind_ref[...])
    qn_ref[...] = q_norm

    @pl.when(pl.program_id(0) % nb == 0)
    def _():
        kn_scr[...] = jnp.zeros_like(kn_scr)

    kn_scr[...] = jnp.maximum(kn_scr[...], k_norm)
    kn_ref[...] = kn_scr[...]


def _row_block(a, tm, d):
    sub = 1 if a.shape[1] == d else d // a.shape[1]
    return pl.BlockSpec((tm * sub, a.shape[1]), lambda r: (r, 0))


def _layer_block(a, layer):
    return pl.BlockSpec((None,) + a.shape[1:], lambda r: (layer,) + (0,) * (a.ndim - 1))


def _mix_in(x2d, g, wqkv, wh, wc, wff, fb, conv_w, conv_b, conv_ng, conv_nb, layer, batch, seq, tm):
    n = batch * seq
    nb = seq // tm
    ch = wc.shape[2] // 2
    full = lambda a: pl.BlockSpec(a.shape, lambda r: (0,) * a.ndim)
    stacked = lambda a: _layer_block(a, layer)
    qk_cols = 2 * FOX_HEADS * FOX_HEAD_DIM
    ind = (jnp.arange(qk_cols)[:, None] // FOX_HEAD_DIM == jnp.arange(LANES)[None, :]).astype(BF16)
    return pl.pallas_call(
        functools.partial(_mix_in_kernel, nb=nb),
        grid=(n // tm,),
        in_specs=[_row_block(x2d, tm, wqkv.shape[1]), full(g), stacked(wqkv), stacked(wh), stacked(wc),
                  stacked(wff), full(fb), full(ind), full(conv_w), full(conv_b), full(conv_ng), full(conv_nb)],
        out_specs=[pl.BlockSpec((tm, wqkv.shape[2]), lambda r: (r, 0)),
                   pl.BlockSpec((tm, wh.shape[2]), lambda r: (r, 0)),
                   pl.BlockSpec((tm, ch), lambda r: (r, 0)),
                   pl.BlockSpec((None, FOX_HEADS, tm), lambda r: (r // nb, 0, r % nb)),
                   pl.BlockSpec((None, FOX_HEADS, LANES), lambda r: (r, 0, 0)),
                   pl.BlockSpec((None, FOX_HEADS, LANES), lambda r: (r // nb, 0, 0))],
        out_shape=[jax.ShapeDtypeStruct((n, wqkv.shape[2]), BF16),
                   jax.ShapeDtypeStruct((n, wh.shape[2]), BF16),
                   jax.ShapeDtypeStruct((n, ch), BF16),
                   jax.ShapeDtypeStruct((batch, FOX_HEADS, seq), F32),
                   jax.ShapeDtypeStruct((n // tm, FOX_HEADS, LANES), F32),
                   jax.ShapeDtypeStruct((batch, FOX_HEADS, LANES), F32)],
        scratch_shapes=[pltpu.VMEM((FOX_HEADS, LANES), F32),
                        pltpu.VMEM((CONV_HALO + tm, ch), F32),
                        pltpu.VMEM((SUBLANES - 1, CONV_HALO + tm, ch), F32)],
        compiler_params=_cparams("arbitrary"),
        name="mix_in",
    )(x2d, g, wqkv, wh, wc, wff, fb, ind, conv_w, conv_b, conv_ng, conv_nb)


def _fox_plan_kernel(lf_ref, qn_ref, kn_ref, c_ref, keep_ref, *, blk, stat_rows):
    seq = lf_ref.shape[-1]
    nblk = seq // blk
    r = lax.broadcasted_iota(I32, (blk, blk), 0)
    c = lax.broadcasted_iota(I32, (blk, blk), 1)
    upper = jnp.where(r <= c, 1.0, 0.0).astype(BF16)
    lane = lax.broadcasted_iota(I32, (FOX_HEADS, LANES), 1)
    carry = jnp.zeros((FOX_HEADS, 1), F32)
    first = jnp.zeros((FOX_HEADS, LANES), F32)
    last = jnp.zeros((FOX_HEADS, LANES), F32)
    qn = jnp.zeros((FOX_HEADS, LANES), F32)
    for b in range(nblk):
        x = lf_ref[:, b * blk:(b + 1) * blk]
        hi, mid, lo = _split3(x)
        cb = _dot(hi, upper) + _dot(mid, upper) + _dot(lo, upper) + carry
        for h in range(FOX_HEADS):
            c_ref[h, :, b * blk:(b + 1) * blk] = cb[h:h + 1, :]
        carry = cb[:, blk - 1:blk]
        first = jnp.where(lane == b, cb[:, 0:1], first)
        last = jnp.where(lane == b, carry, last)
        qn = jnp.where(lane == b, qn_ref[b * blk // stat_rows], qn)
    bound = FOX_NORM_SLACK * 2.0 * qn * kn_ref[...] + first
    keep = jnp.zeros((FOX_HEADS, LANES), I32)
    for i in range(nblk):
        live = (bound[:, i:i + 1] - last >= FOX_SKIP_EXPONENT * LOG2E) & (lane < i)
        keep = jnp.where(lane == i, jnp.sum(live.astype(I32), axis=1, keepdims=True), keep)
    keep_ref[...] = keep


def _fox_plan(lf, qn, kn, blk, stat_rows):
    batch, heads, seq = lf.shape
    assert seq // blk <= LANES
    return pl.pallas_call(
        functools.partial(_fox_plan_kernel, blk=blk, stat_rows=stat_rows),
        grid=(batch,),
        in_specs=[pl.BlockSpec((None, heads, seq), lambda b: (b, 0, 0)),
                  pl.BlockSpec((seq // stat_rows, heads, LANES), lambda b: (b, 0, 0)),
                  pl.BlockSpec((None, heads, LANES), lambda b: (b, 0, 0))],
        out_specs=[pl.BlockSpec((None, heads, 1, seq), lambda b: (b, 0, 0, 0)),
                   pl.BlockSpec((None, heads, LANES), lambda b: (b, 0, 0))],
        out_shape=[jax.ShapeDtypeStruct((batch, heads, 1, seq), F32),
                   jax.ShapeDtypeStruct((batch, heads, LANES), I32)],
        compiler_params=_cparams("arbitrary"),
        name="fox_plan",
    )(lf, qn, kn)


def _fox_kernel(keep_ref, q_ref, k_ref, v_ref, c_ref, g_ref, o_ref, *, tq):
    head0 = (pl.program_id(0) * FOX_HEADS + 2 * pl.program_id(1)) * LANES
    lane = lax.broadcasted_iota(I32, (1, LANES), 1)
    lo_mask = lane < FOX_HEAD_DIM
    causal = lax.broadcasted_iota(I32, (tq, tq), 0) >= lax.broadcasted_iota(I32, (tq, tq), 1)

    i = pl.program_id(2)
    n_prev = jnp.maximum(keep_ref[head0 + i], keep_ref[head0 + LANES + i])
    q = q_ref[...]
    zero = jnp.zeros_like(q)
    qh = (jnp.where(lo_mask, q, zero), jnp.where(lo_mask, zero, q))
    q0 = pl.multiple_of(i * tq, tq)
    cq0 = tuple(c_ref[h, :, pl.ds(q0, LANES)][:, 0:1] for h in range(2))

    def block(j, carry, masked):
        k0 = pl.multiple_of(j * tq, tq)
        kb = k_ref[pl.ds(k0, tq), :]
        vb = v_ref[pl.ds(k0, tq), :]
        s = [_dot_nt(qh[h], kb) + (cq0[h] - c_ref[h, :, pl.ds(k0, tq)]) for h in range(2)]
        out = []
        for h in range(2):
            m, l, acc = carry[h]
            sh = jnp.where(causal, s[h], -jnp.inf) if masked else s[h]
            m_new = jnp.maximum(m, jnp.max(sh, axis=-1, keepdims=True))
            alpha = jnp.exp2(m - m_new)
            p = jnp.exp2(sh - m_new)
            l = alpha * l + jnp.sum(p, axis=-1, keepdims=True)
            acc = alpha * acc + _dot(p.astype(BF16), vb)
            out.append((m_new, l, acc))
        return tuple(out)

    init = tuple((jnp.full((tq, 1), -jnp.inf, F32), jnp.zeros((tq, 1), F32), jnp.zeros((tq, LANES), F32))
                 for _ in range(2))
    carry = block(i, init, True)
    carry = lax.fori_loop(i - n_prev, i, lambda j, c: block(j, c, False), carry)
    (_, l0, a0), (_, l1, a1) = carry
    o = jnp.where(lo_mask, a0 * (1.0 / l0), a1 * (1.0 / l1))
    o_ref[...] = (_pair_rms(o, lo_mask, FOX_HEAD_DIM) * g_ref[...]).astype(o_ref.dtype)


def _fox_attention(keep, qkv, c, gain, batch, seq, tq):
    n = qkv.shape[0]
    pairs = FOX_HEADS // 2
    rows = tq
    nq = seq // rows
    grid_spec = pltpu.PrefetchScalarGridSpec(
        num_scalar_prefetch=1,
        grid=(batch, pairs, nq),
        in_specs=[pl.BlockSpec((rows, LANES), lambda b, p, i, keep: (b * nq + i, p)),
                  pl.BlockSpec((seq, LANES), lambda b, p, i, keep: (b, pairs + p)),
                  pl.BlockSpec((seq, LANES), lambda b, p, i, keep: (b, 2 * pairs + p)),
                  pl.BlockSpec((None, 2, 1, seq), lambda b, p, i, keep: (b, p, 0, 0)),
                  pl.BlockSpec((1, LANES), lambda b, p, i, keep: (0, p))],
        out_specs=pl.BlockSpec((rows, LANES), lambda b, p, i, keep: (b * nq + i, p)),
    )
    return pl.pallas_call(
        functools.partial(_fox_kernel, tq=tq),
        grid_spec=grid_spec,
        out_shape=jax.ShapeDtypeStruct((n, pairs * LANES), BF16),
        compiler_params=_cparams("parallel", "parallel", "arbitrary"),
        name="fox_attention",
    )(keep, qkv, qkv, qkv, c, gain)


def _hgrn_kernel(q_ref, f_ref, v_ref, gate_ref, lbz_ref, gain_ref, o_ref, st_ref, u_scr, prev_scr, *, layer, n_sub):
    T, C = HGRN_BLOCK, HGRN_CHUNK
    nchunk = T // C
    i = pl.program_id(2)

    @pl.when(i == 0)
    def _():
        st_ref[...] = jnp.zeros_like(st_ref)

    z = lbz_ref[...]
    e = jnp.exp(z - jnp.max(z, axis=0, keepdims=True))
    pz = e / jnp.sum(e, axis=0, keepdims=True)
    lb = jnp.zeros((1, LANES), F32)
    for j in range(1, layer + 1):
        lb = lb + pz[j:j + 1, :]

    lane = lax.broadcasted_iota(I32, (1, LANES), 1)
    lo_mask = lane < HGRN_DIM
    r = lax.broadcasted_iota(I32, (T, T), 0)
    c = lax.broadcasted_iota(I32, (T, T), 1)
    same_chunk = (r // C) == (c // C)
    one = lambda m: jnp.where(m, 1.0, 0.0).astype(BF16)
    scan_mat = jnp.concatenate([one(same_chunk & (c <= r)),
                                one(same_chunk & ((c % C) <= C // 2)),
                                one(same_chunk)], axis=0)
    intra_mask = same_chunk & (c <= r)
    chunk_of_row = lax.broadcasted_iota(I32, (T, LANES), 0) // C
    vr = lax.broadcasted_iota(I32, (LANES, nchunk * LANES), 0)
    kc = lax.broadcasted_iota(I32, (LANES, nchunk * LANES), 1)
    same_head = (vr < HGRN_DIM) == ((kc % LANES) < HGRN_DIM)

    subs = range(n_sub)
    rows = [pl.ds(sb * T, T) for sb in subs]
    q = [q_ref[r_, :].astype(F32) for r_ in rows]
    f = [lb + (1.0 - lb) * _sigmoid(f_ref[r_, :].astype(F32)) for r_ in rows]
    kk = [1.0 - f_ for f_ in f]
    parts = []
    for f_ in f:
        parts.extend(_split2(jnp.log(f_)))
    sc = _dot(scan_mat, jnp.concatenate(parts, axis=1))
    sc = [sc[:, (2 * sb) * LANES:(2 * sb + 1) * LANES] + sc[:, (2 * sb + 1) * LANES:(2 * sb + 2) * LANES]
          for sb in subs]
    b = [s_[:T] for s_ in sc]
    b_mid = [s_[T:2 * T] for s_ in sc]
    b_last = [s_[2 * T:] for s_ in sc]
    v = [v_ref[r_, :] for r_ in rows]
    for sb in subs:
        k_out = kk[sb] * jnp.exp(b_last[sb] - b[sb])
        k_exp = jnp.concatenate([jnp.where(chunk_of_row == j, k_out, 0.0) for j in range(nchunk)],
                                axis=1).astype(BF16)
        v_t = v[sb].astype(F32).T.astype(BF16)
        u_scr[sb] = jnp.where(same_head, _dot(v_t, k_exp), 0.0)
    att = []
    for sb in subs:
        q_in = (q[sb] * jnp.exp(b[sb] - b_mid[sb])).astype(BF16)
        k_in = (kk[sb] * jnp.exp(b_mid[sb] - b[sb])).astype(BF16)
        zq = jnp.zeros_like(q_in)
        att.append([jnp.where(intra_mask, _dot_nt(qm, k_in), 0.0).astype(BF16)
                    for qm in (jnp.where(lo_mask, q_in, zq), jnp.where(lo_mask, zq, q_in))])
    o = [jnp.where(lo_mask, _dot(att[sb][0], v[sb]), _dot(att[sb][1], v[sb])) for sb in subs]
    state = st_ref[...]
    for sb in subs:
        decay = jnp.exp(b_last[sb])
        for j in range(nchunk):
            prev_scr[sb, j * LANES:(j + 1) * LANES, :] = state.astype(BF16)
            state = state * decay[j * C:j * C + 1, :] + u_scr[sb, :, j * LANES:(j + 1) * LANES]
    st_ref[...] = state
    for sb in subs:
        q_out = (q[sb] * jnp.exp(b[sb])).astype(BF16)
        o_all = _dot_nt(q_out, prev_scr[sb])
        acc = o[sb]
        for j in range(nchunk):
            acc = acc + jnp.where(chunk_of_row == j, o_all[:, j * LANES:(j + 1) * LANES], 0.0)
        gate = gate_ref[rows[sb], :].astype(F32)
        y = _pair_rms(acc, lo_mask, HGRN_DIM) * gain_ref[...] * (gate * _sigmoid(gate))
        o_ref[rows[sb], :] = y.astype(o_ref.dtype)


def _hgrn(hy, lb_logits, gain, layer, batch, seq, tg):
    n = hy.shape[0]
    pairs = HGRN_HEADS // 2
    nb = seq // tg
    depth = lb_logits.shape[0]
    col = lambda k: pl.BlockSpec((tg, LANES), lambda b, p, i, k=k: (b * nb + i, k * pairs + p))
    n_sub = tg // HGRN_BLOCK
    states = (HGRN_BLOCK // HGRN_CHUNK) * LANES
    return pl.pallas_call(
        functools.partial(_hgrn_kernel, layer=layer, n_sub=n_sub),
        grid=(batch, pairs, nb),
        in_specs=[col(0), col(1), col(2), col(3),
                  pl.BlockSpec((depth, LANES), lambda b, p, i: (0, p)),
                  pl.BlockSpec((1, LANES), lambda b, p, i: (0, p))],
        out_specs=pl.BlockSpec((tg, LANES), lambda b, p, i: (b * nb + i, p)),
        out_shape=jax.ShapeDtypeStruct((n, pairs * LANES), BF16),
        scratch_shapes=[pltpu.VMEM((LANES, LANES), F32),
                        pltpu.VMEM((n_sub, LANES, states), F32),
                        pltpu.VMEM((n_sub, states, LANES), BF16)],
        compiler_params=_cparams("parallel", "parallel", "arbitrary"),
        name="hgrn",
    )(hy, hy, hy, hy, lb_logits, gain)


def _conv_halo(i, a_scr, tm):
    @pl.when(i == 0)
    def _():
        a_scr[0:CONV_HALO, :] = jnp.zeros((CONV_HALO, a_scr.shape[1]), F32)

    @pl.when(i > 0)
    def _():
        a_scr[0:CONV_HALO, :] = a_scr[tm:tm + CONV_HALO, :]


def _conv_stage(cu, cg, b_ref, a_scr, sh_scr, tm):
    a_scr[CONV_HALO:CONV_HALO + tm, :] = cu * _sigmoid(cg)
    span = CONV_HALO + tm - SUBLANES
    for k in range(1, SUBLANES):
        sh_scr[k - 1, 0:span, :] = a_scr[k:k + span, :]
    return jnp.zeros((tm, a_scr.shape[1]), F32) + b_ref[...]


def _conv_taps(acc, taps, w_ref, a_scr, sh_scr, tm):
    first = CONV_HALO - (CONV_WIDTH - 1)
    for w in taps:
        base, k = divmod(first + w, SUBLANES)
        src = a_scr if k == 0 else sh_scr.at[k - 1]
        acc = acc + src[base * SUBLANES:base * SUBLANES + tm, :] * w_ref[w:w + 1, :]
    return acc


def _conv_finish(acc, ng_ref, nb_ref):
    lane = lax.broadcasted_iota(I32, (1, LANES), 1)
    lo_mask = lane < CONV_GROUP
    halves = []
    for hh in range(acc.shape[1] // LANES):
        xh = acc[:, hh * LANES:(hh + 1) * LANES]
        s_all = jnp.sum(xh, axis=-1, keepdims=True)
        s_lo = jnp.sum(jnp.where(lo_mask, xh, 0.0), axis=-1, keepdims=True)
        d = xh - jnp.where(lo_mask, s_lo, s_all - s_lo) * (1.0 / CONV_GROUP)
        halves.append(_pair_rms(d, lo_mask, CONV_GROUP))
    y = jnp.concatenate(halves, axis=1) * ng_ref[...] + nb_ref[...]
    return y * _sigmoid(y)


def _router_logits_t(t, wr_hi, wr_lo, br):
    th, tl = _split2(t)
    return _dot_nt(wr_hi, th) + _dot_nt(wr_lo, th) + _dot_nt(wr_hi, tl) + br


def _xattn_kernel(fo_ref, ho_ref, co_ref, x_ref, wo_ref, gx_ref, wq_ref, kv_ref, wxo_ref, g_ref,
                  wrh_ref, wrl_ref, br_ref, x2_ref, route_ref, cnt_ref, cnt_scr, *, tm):
    step = pl.program_id(0)

    @pl.when(step == 0)
    def _():
        cnt_scr[...] = jnp.zeros_like(cnt_scr)

    nf, nh = fo_ref.shape[1], ho_ref.shape[1]
    x1 = (_load_rows(x_ref, tm) + _dot(fo_ref[...], wo_ref[0:nf, :]) + _dot(ho_ref[...], wo_ref[nf:nf + nh, :])
          + _dot(co_ref[...], wo_ref[nf + nh:, :]))
    xq = _dot(_rms(x1, gx_ref[...]).astype(BF16), wq_ref[...]).astype(BF16)
    d = xq.shape[1]
    hd = d // XATT_HEADS
    scores = [_dot_nt(xq[:, h * hd:(h + 1) * hd], kv_ref[:, h * hd:(h + 1) * hd]) for h in range(XATT_HEADS)]
    probs = [jnp.exp(s - jnp.max(s, axis=-1, keepdims=True)) for s in scores]
    heads = []
    for h, p in enumerate(probs):
        vh = kv_ref[:, d + h * hd:d + (h + 1) * hd]
        l = jnp.sum(p, axis=-1, keepdims=True)
        heads.append((_dot(p.astype(BF16), vh) * (1.0 / l)).astype(BF16))
    x2 = x1 + _dot(jnp.concatenate(heads, axis=1), wxo_ref[...])
    _store_rows(x2_ref, x2)
    logits = _router_logits_t(_rms(x2, g_ref[...]), wrh_ref[...], wrl_ref[...], br_ref[...])
    gl = [logits[g:g + 1, :] for g in range(N_GROUPS)]
    gmax = jnp.maximum(jnp.maximum(gl[0], gl[1]), jnp.maximum(gl[2], gl[3]))
    gidx = jnp.where(gl[0] >= gmax, 0, jnp.where(gl[1] >= gmax, 1, jnp.where(gl[2] >= gmax, 2, 3)))
    rows8 = lax.broadcasted_iota(I32, (8, tm), 0)
    onehot = jnp.where(rows8 == gidx, 1.0, 0.0)
    r = lax.broadcasted_iota(I32, (tm, tm), 0)
    c = lax.broadcasted_iota(I32, (tm, tm), 1)
    incl = _dot(onehot.astype(BF16), jnp.where(r <= c, 1.0, 0.0).astype(BF16))
    before = cnt_scr[:, 0:1]
    rank = jnp.sum(onehot * (incl - 1.0 + before), axis=0, keepdims=True)
    route_ref[...] = jnp.where(rows8 == 0, gidx, jnp.where(rows8 == 1, rank.astype(I32), 0))
    cnt_scr[...] = cnt_scr[...] + incl[:, tm - 1:tm]
    cnt_ref[...] = cnt_scr[...].astype(I32)


def _xattn(fo, ho, co, x2d, wo, gx, wq, kv, wxo, g, wrh, wrl, br, layer, batch, seq, tm):
    n, d = fo.shape[0], wo.shape[2]
    nb = seq // tm
    mem = kv.shape[2]
    full = lambda a: pl.BlockSpec(a.shape, lambda r: (0,) * a.ndim)
    stacked = lambda a: _layer_block(a, layer)
    rowblk = lambda a: _row_block(a, tm, d if a is x2d else a.shape[1])
    return pl.pallas_call(
        functools.partial(_xattn_kernel, tm=tm),
        grid=(n // tm,),
        in_specs=[rowblk(fo), rowblk(ho), rowblk(co), rowblk(x2d), stacked(wo), full(gx), stacked(wq),
                  pl.BlockSpec((None, None, mem, 2 * d), lambda r: (layer, r // nb, 0, 0)),
                  stacked(wxo), full(g), full(wrh), full(wrl), full(br)],
        out_specs=[pl.BlockSpec((tm * (d // LANES), LANES), lambda r: (r, 0)),
                   pl.BlockSpec((8, tm), lambda r: (0, r)),
                   pl.BlockSpec((8, LANES), lambda r: (0, 0))],
        out_shape=[jax.ShapeDtypeStruct((n * (d // LANES), LANES), F32),
                   jax.ShapeDtypeStruct((8, n), I32),
                   jax.ShapeDtypeStruct((8, LANES), I32)],
        scratch_shapes=[pltpu.VMEM((8, LANES), F32)],
        compiler_params=_cparams("arbitrary"),
        name="xattn",
    )(fo, ho, co, x2d, wo, gx, wq, kv, wxo, g, wrh, wrl, br)


def _plan_kernel(cnt_ref, route_ref, src_ref, dst_ref, tg_ref, dest_vmem, dest_smem, sem, *, n_tok, tmm, sub):
    n_slots = src_ref.shape[0]
    n_tiles = tg_ref.shape[0] - 1
    shift = tmm.bit_length() - 1
    assert tmm == 1 << shift

    def clear(s, _):
        src_ref[s] = 0
        parity = lax.shift_right_logical(s, shift) & 1
        dst_ref[s] = (n_tok + parity * tmm + (s & (tmm - 1))) * sub
        return 0

    off = jnp.int32(0)
    last_group = jnp.int32(0)
    starts, ends = [], []
    for g in range(N_GROUPS):
        cnt = cnt_ref[g]
        padded = lax.shift_left(lax.shift_right_logical(cnt + (tmm - 1), shift), shift)
        lax.fori_loop(off + cnt, off + padded, clear, 0)
        starts.append(off)
        off = off + padded
        ends.append(off)
        last_group = jnp.where(cnt > 0, g, last_group)
    lax.fori_loop(off, n_slots, clear, 0)

    gidx = route_ref[0:1, :]
    rank = route_ref[1:2, :]
    start = jnp.where(gidx == 0, starts[0], jnp.where(gidx == 1, starts[1],
                                                      jnp.where(gidx == 2, starts[2], starts[3])))
    dest_vmem[...] = start + rank
    to_smem = pltpu.make_async_copy(dest_vmem, dest_smem, sem)
    to_smem.start()
    to_smem.wait()

    def place(t, _):
        slot = dest_smem[0, t]
        src_ref[slot] = t * sub
        dst_ref[slot] = t * sub
        return 0

    lax.fori_loop(0, n_tok, place, 0, unroll=8)

    def tile(k, _):
        start = k * tmm
        g = ((start >= ends[0]).astype(I32) + (start >= ends[1]).astype(I32) + (start >= ends[2]).astype(I32))
        tg_ref[k] = jnp.minimum(g, last_group)
        return 0

    lax.fori_loop(0, n_tiles, tile, 0)
    tg_ref[n_tiles] = lax.shift_right_logical(off, shift)


def _route_plan(cnt, route, n_tok, tmm, sub):
    n_tiles = n_tok // tmm + N_GROUPS
    smem = pl.BlockSpec(memory_space=pltpu.SMEM)
    slots = jax.ShapeDtypeStruct((n_tiles * tmm,), I32)
    return pl.pallas_call(
        functools.partial(_plan_kernel, n_tok=n_tok, tmm=tmm, sub=sub),
        in_specs=[smem, pl.BlockSpec(memory_space=pltpu.VMEM)],
        out_specs=[smem, smem, smem],
        out_shape=[slots, slots, jax.ShapeDtypeStruct((n_tiles + 1,), I32)],
        scratch_shapes=[pltpu.VMEM((1, n_tok), I32), pltpu.SMEM((1, n_tok), I32), pltpu.SemaphoreType.DMA(())],
        name="route_plan",
    )(cnt, route)


def _moe_kernel(src_ref, dst_ref, tg_ref, x2_hbm, g_ref, wrh_ref, wrl_ref, br_ref, wg_ref, wu_ref, wd_ref,
                x3_hbm, xb, ob, wg_s, wu_s, wd_s, gsem, ssem, zsem, *, tmm, n_tok):
    k = pl.program_id(0)
    n_active = tg_ref[pl.num_programs(0)]
    slot = k % 2

    @pl.when((k == 0) | (tg_ref[k] != tg_ref[jnp.maximum(k - 1, 0)]))
    def _():
        ff = wd_ref.shape[1]
        for e in range(EXPERTS_PER_GROUP):
            wg_s[e] = wg_ref[e].astype(BF16)
            wu_s[e] = wu_ref[e].astype(BF16)
            wd_s[e * ff:(e + 1) * ff, :] = wd_ref[e].astype(BF16)

    sub = xb.shape[1] // tmm

    def rows_at(first):
        return pl.ds(pl.multiple_of(first, sub), sub)

    def gather_copy(tile, sl, r):
        return pltpu.make_async_copy(x2_hbm.at[rows_at(src_ref[tile * tmm + r])], xb.at[sl, rows_at(r * sub)],
                                     gsem.at[sl])

    def scatter_copy(tile, sl, r):
        return pltpu.make_async_copy(ob.at[sl, rows_at(r * sub)], x3_hbm.at[rows_at(dst_ref[tile * tmm + r])],
                                     ssem.at[sl])

    def for_rows(fn):
        def body(r8, _):
            for u in range(DMA_UNROLL):
                fn(r8 * DMA_UNROLL + u, u % 2)
            return 0
        lax.fori_loop(0, tmm // DMA_UNROLL, body, 0)

    def dump_fill(half):
        return pltpu.make_async_copy(ob.at[1], x3_hbm.at[pl.ds((n_tok + half * tmm) * sub, tmm * sub)], zsem)

    @pl.when(k == 0)
    def _():
        ob[1] = jnp.zeros(ob.shape[1:], F32)
        dump_fill(0).start()
        dump_fill(1).start()
        for_rows(lambda r, pri: gather_copy(0, 0, r).start(priority=pri))

    @pl.when(k < n_active)
    def _():
        for_rows(lambda r, pri: gather_copy(k, slot, r).wait())

        @pl.when(k + 1 < n_active)
        def _():
            for_rows(lambda r, pri: gather_copy(k + 1, 1 - slot, r).start(priority=pri))

        @pl.when(k >= 2)
        def _():
            for_rows(lambda r, pri: scatter_copy(k - 2, slot, r).wait())

        y = _moe_tile(_load_rows(xb.at[slot], tmm), tg_ref[k], g_ref, wrh_ref, wrl_ref, br_ref,
                      wg_s, wu_s, wd_s)
        _store_rows(ob.at[slot], y)

        @pl.when(k == 0)
        def _():
            dump_fill(0).wait()
            dump_fill(1).wait()

        for_rows(lambda r, pri: scatter_copy(k, slot, r).start(priority=pri))

        @pl.when(k == n_active - 1)
        def _():
            @pl.when(k >= 1)
            def _():
                for_rows(lambda r, pri: scatter_copy(k - 1, 1 - slot, r).wait())
            for_rows(lambda r, pri: scatter_copy(k, slot, r).wait())


def _moe_tile(x, grp, g_ref, wrh_ref, wrl_ref, br_ref, wg_ref, wu_ref, wd_ref):
    t = _rms(x, g_ref[...])
    tb = t.astype(BF16)
    th, tl = _split2(t)
    logits = _dot(th, wrh_ref[...]) + _dot(th, wrl_ref[...]) + _dot(tl, wrh_ref[...]) + br_ref[...]
    lane = lax.broadcasted_iota(I32, (1, LANES), 1)
    lanef = lane.astype(F32)
    rmax = lambda a: jnp.max(a, axis=-1, keepdims=True)
    rsum = lambda a: jnp.sum(a, axis=-1, keepdims=True)
    gmask = lane < N_GROUPS
    gmax = rmax(jnp.where(gmask, logits, -jnp.inf))
    zg = rsum(jnp.where(gmask, jnp.exp(logits - gmax), 0.0))
    p_group = jnp.exp(rsum(jnp.where(lane == grp, logits, 0.0)) - gmax) / zg
    e_lo = N_GROUPS + EXPERTS_PER_GROUP * grp
    emask = (lane >= e_lo) & (lane < e_lo + EXPERTS_PER_GROUP)
    em = jnp.where(emask, logits, -jnp.inf)
    e1 = rmax(em)
    i1 = jnp.min(jnp.where(em == e1, lanef, 1e9), axis=-1, keepdims=True)
    em2 = jnp.where(lanef == i1, -jnp.inf, em)
    e2 = rmax(em2)
    i2 = jnp.min(jnp.where(em2 == e2, lanef, 1e9), axis=-1, keepdims=True)
    ze = rsum(jnp.where(emask, jnp.exp(logits - e1), 0.0))
    p1 = 1.0 / ze
    p2 = jnp.exp(e2 - e1) / ze
    w1 = p_group * (p1 / (p1 + p2))
    w2 = p_group * (p2 / (p1 + p2))
    gates = [_dot(tb, wg_ref[e]) for e in range(EXPERTS_PER_GROUP)]
    ups = [_dot(tb, wu_ref[e]) for e in range(EXPERTS_PER_GROUP)]
    cw_lanes = jnp.where(lanef == i1, w1, 0.0) + jnp.where(lanef == i2, w2, 0.0)
    parts = []
    for e in range(EXPERTS_PER_GROUP):
        cw = rsum(jnp.where(lane == e_lo + e, cw_lanes, 0.0))
        parts.append((gates[e] * _sigmoid(gates[e]) * ups[e] * cw).astype(BF16))
    return x + _dot(jnp.concatenate(parts, axis=1), wd_ref[...])


def _moe(src, dst, tg, x2, g, wrh_t, wrl_t, br_row, wg, wu, wd, layer, tmm):
    sub = g.shape[1] // LANES
    n_tok = x2.shape[0] // sub
    n_tiles = tg.shape[0] - 1
    full = lambda a: pl.BlockSpec(a.shape, lambda k, src, dst, tg: (0,) * a.ndim)
    grp = lambda a: pl.BlockSpec((None, None) + a.shape[2:], lambda k, src, dst, tg: (layer, tg[k], 0, 0, 0))
    n_exp, d, ff = wg.shape[2:]
    grid_spec = pltpu.PrefetchScalarGridSpec(
        num_scalar_prefetch=3,
        grid=(n_tiles,),
        in_specs=[pl.BlockSpec(memory_space=pl.ANY), full(g), full(wrh_t), full(wrl_t), full(br_row),
                  grp(wg), grp(wu), grp(wd)],
        out_specs=pl.BlockSpec(memory_space=pl.ANY),
        scratch_shapes=[pltpu.VMEM((2, tmm * sub, LANES), F32), pltpu.VMEM((2, tmm * sub, LANES), F32),
                        pltpu.VMEM((n_exp, d, ff), BF16), pltpu.VMEM((n_exp, d, ff), BF16),
                        pltpu.VMEM((n_exp * ff, d), BF16),
                        pltpu.SemaphoreType.DMA((2,)), pltpu.SemaphoreType.DMA((2,)), pltpu.SemaphoreType.DMA(())],
    )
    return pl.pallas_call(
        functools.partial(_moe_kernel, tmm=tmm, n_tok=n_tok),
        grid_spec=grid_spec,
        out_shape=jax.ShapeDtypeStruct(((n_tok + 2 * tmm) * sub, LANES), F32),
        compiler_params=_cparams("arbitrary"),
        name="moe",
    )(src, dst, tg, x2, g, wrh_t, wrl_t, br_row, wg, wu, wd)


def _final_norm_kernel(x_ref, g_ref, o_ref):
    o_ref[...] = _rms(_load_rows(x_ref, o_ref.shape[0]), g_ref[...])


def _final_norm(xs, g, n, tm):
    d = g.shape[1]
    return pl.pallas_call(
        _final_norm_kernel,
        grid=(n // tm,),
        in_specs=[_row_block(xs, tm, d), pl.BlockSpec((1, d), lambda r: (0, 0))],
        out_specs=pl.BlockSpec((tm, d), lambda r: (r, 0)),
        out_shape=jax.ShapeDtypeStruct((n, d), F32),
        compiler_params=_cparams("arbitrary"),
        name="final_norm",
    )(xs, g)


def _row(v):
    return v.reshape(1, -1)


def kernel(x, mem, mix_norm_g, w_in, fox_f_bias, fox_norm_g, hgrn_lb_logits, hgrn_norm_g, conv_w, conv_b,
           conv_norm_g, conv_norm_b, w_out, xatt_norm_g, mem_norm_g, w_xq, w_xkv, w_xo, ffn_norm_g,
           w_group, b_group, w_router, b_router, w_gate, w_up, w_down, final_norm_g):
    batch, seq, d = x.shape
    depth = w_in.shape[0]
    n = batch * seq
    tm = min(512, seq)
    tmm = min(512, seq)
    fblk = min(FOX_BLOCK, tm)
    fw = FOX_HEADS * FOX_HEAD_DIM
    hw = HGRN_HEADS * HGRN_DIM
    cw = conv_w.shape[-1]
    assert seq % tm == 0 and seq % HGRN_BLOCK == 0 and d == fw + hw + cw

    offs = [0]
    for width in (fw, fw, fw, FOX_HEADS, hw, hw, hw, hw, cw, cw):
        offs.append(offs[-1] + width)
    seg = lambda a, i, j: a[..., offs[i]:offs[j]]
    wqkv = jnp.concatenate([seg(w_in, 0, 1) * (FOX_HEAD_DIM ** -0.5 * LOG2E), seg(w_in, 1, 3)], axis=-1).astype(BF16)
    wff = jnp.swapaxes(seg(w_in, 3, 4), 1, 2).astype(BF16)
    wh = jnp.concatenate([seg(w_in, 4, 5), seg(w_in, 5, 6), seg(w_in, 6, 7), seg(w_in, 7, 8)], axis=-1).astype(BF16)
    wc = seg(w_in, 8, 10).astype(BF16)
    wo = w_out.astype(BF16)
    wxq = (w_xq * ((d // XATT_HEADS) ** -0.5)).astype(BF16)
    wxkv = w_xkv.astype(BF16)
    wxo = w_xo.astype(BF16)
    n_exp = N_GROUPS * EXPERTS_PER_GROUP
    wr = jnp.concatenate([w_group, w_router], axis=-1)
    wr_hi = wr.astype(BF16)
    wr_lo = (wr - wr_hi.astype(F32)).astype(BF16)
    pad_rows = lambda a: jnp.pad(jnp.swapaxes(a, 1, 2), ((0, 0), (0, ROUTER_ROWS - a.shape[2]), (0, 0)))
    pad_cols = lambda a: jnp.pad(a, ((0, 0), (0, 0), (0, LANES - a.shape[2])))
    br = jnp.concatenate([b_group, b_router], axis=-1)
    br_col = jnp.pad(br, ((0, 0), (0, ROUTER_ROWS - br.shape[1])))[:, :, None]
    br_row = jnp.pad(br, ((0, 0), (0, LANES - br.shape[1])))[:, None, :]
    conv_w_pad = jnp.pad(conv_w, ((0, 0), (0, CONV_HALO - CONV_WIDTH), (0, 0)))
    wrh_rows, wrl_rows, wrh_cols, wrl_cols = pad_rows(wr_hi), pad_rows(wr_lo), pad_cols(wr_hi), pad_cols(wr_lo)

    kv_all = _mem_kv(mem.reshape(-1, d), _row(mem_norm_g), wxkv).reshape(depth, batch, mem.shape[1], 2 * d)

    xs = x.reshape(n, d)
    for l in range(depth):
        qkv, hy, co, lf, qn, kn = _mix_in(xs, _row(mix_norm_g[l]), wqkv, wh, wc, wff, fox_f_bias[l].reshape(-1, 1),
                                          conv_w_pad[l], _row(conv_b[l]), _row(conv_norm_g[l]),
                                          _row(conv_norm_b[l]), l, batch, seq, tm)
        c, keep = _fox_plan(lf, qn, kn, fblk, tm)
        fo = _fox_attention(keep.reshape(-1), qkv, c, _row(fox_norm_g[l]), batch, seq, fblk)
        ho = _hgrn(hy, hgrn_lb_logits, _row(hgrn_norm_g[l]), l, batch, seq, min(1024, seq))
        x2, route, cnt = _xattn(fo, ho, co, xs, wo, _row(xatt_norm_g[l]), wxq, kv_all, wxo,
                                _row(ffn_norm_g[l]), wrh_rows[l], wrl_rows[l], br_col[l], l, batch, seq, tm)
        src, dst, tg = _route_plan(cnt[:N_GROUPS, 0], route, n, tmm, d // LANES)
        xs = _moe(src, dst, tg, x2, _row(ffn_norm_g[l]), wrh_cols[l], wrl_cols[l], br_row[l],
                  w_gate, w_up, w_down, l, tmm)
    return _final_norm(xs, _row(final_norm_g), n, tm).reshape(batch, seq, d)
```

```python
import functools

import jax
import jax.numpy as jnp
from jax import lax
from jax.experimental import pallas as pl
from jax.experimental.pallas import tpu as pltpu

F32 = jnp.float32
BF16 = jnp.bfloat16
I32 = jnp.int32

EPS = 1e-6
LANES = 128
SUBLANES = 8
FOX_HEADS = 8
FOX_HEAD_DIM = 64
LOG2E = 1.4426950408889634
FOX_BLOCK = 512
FOX_SKIP_EXPONENT = -34.0
FOX_NORM_SLACK = 1.02
HGRN_HEADS = 4
HGRN_DIM = 64
HGRN_CHUNK = 16
HGRN_BLOCK = 128
CONV_WIDTH = 31
CONV_HALO = 32
CONV_GROUP = 64
MIX_DOT_COLS = 512
XATT_HEADS = 4
N_GROUPS = 4
EXPERTS_PER_GROUP = 4
ROUTER_ROWS = 32
DMA_UNROLL = 8
VMEM_LIMIT_BYTES = 56 * 1024 * 1024

NT_DIMS = (((1,), (1,)), ((), ()))


def _cparams(*sem):
    return pltpu.CompilerParams(dimension_semantics=sem, vmem_limit_bytes=VMEM_LIMIT_BYTES)


def _dot(a, b):
    return jnp.dot(a, b, preferred_element_type=F32)


def _dot_nt(a, b):
    return lax.dot_general(a, b, NT_DIMS, preferred_element_type=F32)


def _rms(x, g):
    ms = jnp.mean(x * x, axis=-1, keepdims=True)
    return x * lax.rsqrt(ms + EPS) * g


def _sigmoid(x):
    return 1.0 / (1.0 + jnp.exp(-x))


def _split2(x):
    hi = x.astype(BF16)
    lo = (x - hi.astype(F32)).astype(BF16)
    return hi, lo


def _split3(x):
    hi = x.astype(BF16)
    r = x - hi.astype(F32)
    mid = r.astype(BF16)
    lo = (r - mid.astype(F32)).astype(BF16)
    return hi, mid, lo


def _halves_select(lane_lo_mask, a, b):
    return jnp.where(lane_lo_mask, a, b)


def _load_rows(ref, rows):
    if ref.shape[0] == rows:
        return ref[...]
    sub = ref.shape[0] // rows
    return jnp.concatenate([ref[pl.ds(c, rows, stride=sub), :] for c in range(sub)], axis=1)


def _store_rows(ref, val):
    rows = val.shape[0]
    if ref.shape[0] == rows:
        ref[...] = val
    else:
        sub = ref.shape[0] // rows
        for c in range(sub):
            ref[pl.ds(c, rows, stride=sub), :] = val[:, c * LANES:(c + 1) * LANES]


def _pair_rms(o, lo_mask, width):
    o2 = o * o
    s_all = jnp.sum(o2, axis=-1, keepdims=True)
    s_lo = jnp.sum(jnp.where(lo_mask, o2, 0.0), axis=-1, keepdims=True)
    ms = jnp.where(lo_mask, s_lo, s_all - s_lo) * (1.0 / width)
    return o * lax.rsqrt(ms + EPS)


def _mem_kv_kernel(mem_ref, g_ref, w_ref, o_ref):
    t = _rms(mem_ref[...], g_ref[...]).astype(BF16)
    o_ref[...] = _dot(t, w_ref[...]).astype(BF16)


def _mem_kv(mem2d, g, w_xkv):
    depth, d, d2 = w_xkv.shape
    rows = mem2d.shape[0]
    return pl.pallas_call(
        _mem_kv_kernel,
        grid=(depth,),
        in_specs=[pl.BlockSpec((rows, d), lambda l: (0, 0)),
                  pl.BlockSpec((1, d), lambda l: (0, 0)),
                  pl.BlockSpec((None, d, d2), lambda l: (l, 0, 0))],
        out_specs=pl.BlockSpec((None, rows, d2), lambda l: (l, 0, 0)),
        out_shape=jax.ShapeDtypeStruct((depth, rows, d2), BF16),
        compiler_params=_cparams("arbitrary"),
        name="mem_kv",
    )(mem2d, g, w_xkv)


def _head_norm_max(y, ind):
    n2 = _dot((y[:, :ind.shape[0]] * y[:, :ind.shape[0]]).astype(BF16), ind)
    top = jnp.broadcast_to(jnp.max(n2, axis=0, keepdims=True), (FOX_HEADS, LANES))
    row = lax.broadcasted_iota(I32, (FOX_HEADS, LANES), 0)
    lane = lax.broadcasted_iota(I32, (FOX_HEADS, LANES), 1)
    pick = lambda first: jnp.broadcast_to(
        jnp.sqrt(jnp.sum(jnp.where(lane == row + first, top, 0.0), axis=1, keepdims=True)), (FOX_HEADS, LANES))
    return pick(0), pick(FOX_HEADS)


def _mix_in_kernel(x_ref, g_ref, wqkv_ref, wh_ref, wc_ref, wff_ref, fb_ref, ind_ref, cw_ref, cb_ref, cng_ref, cnb_ref,
                   qkv_ref, hy_ref, co_ref, lf_ref, qn_ref, kn_ref, kn_scr, a_scr, sh_scr, *, nb):
    tm = qkv_ref.shape[0]
    _conv_halo(pl.program_id(0) % nb, a_scr, tm)
    t = _rms(_load_rows(x_ref, tm), g_ref[...]).astype(BF16)
    c = _dot(t, wc_ref[...])
    ch = c.shape[1] // 2
    acc = _conv_stage(c[:, :ch], c[:, ch:], cb_ref, a_scr, sh_scr, tm)
    pieces = ([(wqkv_ref, qkv_ref, j) for j in range(wqkv_ref.shape[1] // MIX_DOT_COLS)]
              + [(wh_ref, hy_ref, j) for j in range(wh_ref.shape[1] // MIX_DOT_COLS)])
    per = -(-CONV_WIDTH // len(pieces))
    ys = []
    for n_piece, (w_ref, o_ref, j) in enumerate(pieces):
        cols = slice(j * MIX_DOT_COLS, (j + 1) * MIX_DOT_COLS)
        part = _dot(t, w_ref[:, cols])
        o_ref[:, cols] = part.astype(BF16)
        if w_ref is wqkv_ref:
            ys.append(part)
        half = jnp.uint32(16)
        bits = lax.shift_right_logical(lax.shift_right_logical(pltpu.bitcast(part[:, :ch], jnp.uint32), half), half)
        acc = _conv_taps(acc + pltpu.bitcast(bits, F32), range(n_piece * per, min((n_piece + 1) * per, CONV_WIDTH)),
                         cw_ref, a_scr, sh_scr, tm)
    co_ref[...] = _conv_finish(acc, cng_ref, cnb_ref).astype(co_ref.dtype)
    y = jnp.concatenate(ys, axis=1)
    z = _dot_nt(wff_ref[...], t) + fb_ref[...]
    lf_ref[...] = (jnp.minimum(z, 0.0) - jnp.log1p(jnp.exp(-jnp.abs(z)))) * LOG2E
    q_norm, k_norm = _head_norm_max(y, ind_ref[...])
    qn_ref[...] = q_norm

    @pl.when(pl.program_id(0) % nb == 0)
    def _():
        kn_scr[...] = jnp.zeros_like(kn_scr)

    kn_scr[...] = jnp.maximum(kn_scr[...], k_norm)
    kn_ref[...] = kn_scr[...]


def _row_block(a, tm, d):
    sub = 1 if a.shape[1] == d else d // a.shape[1]
    return pl.BlockSpec((tm * sub, a.shape[1]), lambda r: (r, 0))


def _layer_block(a, layer):
    return pl.BlockSpec((None,) + a.shape[1:], lambda r: (layer,) + (0,) * (a.ndim - 1))


def _mix_in(x2d, g, wqkv, wh, wc, wff, fb, conv_w, conv_b, conv_ng, conv_nb, layer, batch, seq, tm):
    n = batch * seq
    nb = seq // tm
    ch = wc.shape[2] // 2
    full = lambda a: pl.BlockSpec(a.shape, lambda r: (0,) * a.ndim)
    stacked = lambda a: _layer_block(a, layer)
    qk_cols = 2 * FOX_HEADS * FOX_HEAD_DIM
    ind = (jnp.arange(qk_cols)[:, None] // FOX_HEAD_DIM == jnp.arange(LANES)[None, :]).astype(BF16)
    return pl.pallas_call(
        functools.partial(_mix_in_kernel, nb=nb),
        grid=(n // tm,),
        in_specs=[_row_block(x2d, tm, wqkv.shape[1]), full(g), stacked(wqkv), stacked(wh), stacked(wc),
                  stacked(wff), full(fb), full(ind), full(conv_w), full(conv_b), full(conv_ng), full(conv_nb)],
        out_specs=[pl.BlockSpec((tm, wqkv.shape[2]), lambda r: (r, 0)),
                   pl.BlockSpec((tm, wh.shape[2]), lambda r: (r, 0)),
                   pl.BlockSpec((tm, ch), lambda r: (r, 0)),
                   pl.BlockSpec((None, FOX_HEADS, tm), lambda r: (r // nb, 0, r % nb)),
                   pl.BlockSpec((None, FOX_HEADS, LANES), lambda r: (r, 0, 0)),
                   pl.BlockSpec((None, FOX_HEADS, LANES), lambda r: (r // nb, 0, 0))],
        out_shape=[jax.ShapeDtypeStruct((n, wqkv.shape[2]), BF16),
                   jax.ShapeDtypeStruct((n, wh.shape[2]), BF16),
                   jax.ShapeDtypeStruct((n, ch), BF16),
                   jax.ShapeDtypeStruct((batch, FOX_HEADS, seq), F32),
                   jax.ShapeDtypeStruct((n // tm, FOX_HEADS, LANES), F32),
                   jax.ShapeDtypeStruct((batch, FOX_HEADS, LANES), F32)],
        scratch_shapes=[pltpu.VMEM((FOX_HEADS, LANES), F32),
                        pltpu.VMEM((CONV_HALO + tm, ch), F32),
                        pltpu.VMEM((SUBLANES - 1, CONV_HALO + tm, ch), F32)],
        compiler_params=_cparams("arbitrary"),
        name="mix_in",
    )(x2d, g, wqkv, wh, wc, wff, fb, ind, conv_w, conv_b, conv_ng, conv_nb)


def _fox_plan_kernel(lf_ref, qn_ref, kn_ref, c_ref, keep_ref, *, blk, stat_rows):
    seq = lf_ref.shape[-1]
    nblk = seq // blk
    r = lax.broadcasted_iota(I32, (blk, blk), 0)
    c = lax.broadcasted_iota(I32, (blk, blk), 1)
    upper = jnp.where(r <= c, 1.0, 0.0).astype(BF16)
    lane = lax.broadcasted_iota(I32, (FOX_HEADS, LANES), 1)
    carry = jnp.zeros((FOX_HEADS, 1), F32)
    first = jnp.zeros((FOX_HEADS, LANES), F32)
    last = jnp.zeros((FOX_HEADS, LANES), F32)
    qn = jnp.zeros((FOX_HEADS, LANES), F32)
    for b in range(nblk):
        x = lf_ref[:, b * blk:(b + 1) * blk]
        hi, mid, lo = _split3(x)
        cb = _dot(hi, upper) + _dot(mid, upper) + _dot(lo, upper) + carry
        for h in range(FOX_HEADS):
            c_ref[h, :, b * blk:(b + 1) * blk] = cb[h:h + 1, :]
        carry = cb[:, blk - 1:blk]
        first = jnp.where(lane == b, cb[:, 0:1], first)
        last = jnp.where(lane == b, carry, last)
        qn = jnp.where(lane == b, qn_ref[b * blk // stat_rows], qn)
    bound = FOX_NORM_SLACK * 2.0 * qn * kn_ref[...] + first
    keep = jnp.zeros((FOX_HEADS, LANES), I32)
    for i in range(nblk):
        live = (bound[:, i:i + 1] - last >= FOX_SKIP_EXPONENT * LOG2E) & (lane < i)
        keep = jnp.where(lane == i, jnp.sum(live.astype(I32), axis=1, keepdims=True), keep)
    keep_ref[...] = keep


def _fox_plan(lf, qn, kn, blk, stat_rows):
    batch, heads, seq = lf.shape
    assert seq // blk <= LANES
    return pl.pallas_call(
        functools.partial(_fox_plan_kernel, blk=blk, stat_rows=stat_rows),
        grid=(batch,),
        in_specs=[pl.BlockSpec((None, heads, seq), lambda b: (b, 0, 0)),
                  pl.BlockSpec((seq // stat_rows, heads, LANES), lambda b: (b, 0, 0)),
                  pl.BlockSpec((None, heads, LANES), lambda b: (b, 0, 0))],
        out_specs=[pl.BlockSpec((None, heads, 1, seq), lambda b: (b, 0, 0, 0)),
                   pl.BlockSpec((None, heads, LANES), lambda b: (b, 0, 0))],
        out_shape=[jax.ShapeDtypeStruct((batch, heads, 1, seq), F32),
                   jax.ShapeDtypeStruct((batch, heads, LANES), I32)],
        compiler_params=_cparams("arbitrary"),
        name="fox_plan",
    )(lf, qn, kn)


def _fox_kernel(keep_ref, q_ref, k_ref, v_ref, c_ref, g_ref, o_ref, *, tq):
    head0 = (pl.program_id(0) * FOX_HEADS + 2 * pl.program_id(1)) * LANES
    lane = lax.broadcasted_iota(I32, (1, LANES), 1)
    lo_mask = lane < FOX_HEAD_DIM
    causal = lax.broadcasted_iota(I32, (tq, tq), 0) >= lax.broadcasted_iota(I32, (tq, tq), 1)

    i = pl.program_id(2)
    n_prev = jnp.maximum(keep_ref[head0 + i], keep_ref[head0 + LANES + i])
    q = q_ref[...]
    zero = jnp.zeros_like(q)
    qh = (jnp.where(lo_mask, q, zero), jnp.where(lo_mask, zero, q))
    q0 = pl.multiple_of(i * tq, tq)
    cq0 = tuple(c_ref[h, :, pl.ds(q0, LANES)][:, 0:1] for h in range(2))

    def block(j, carry, masked):
        k0 = pl.multiple_of(j * tq, tq)
        kb = k_ref[pl.ds(k0, tq), :]
        vb = v_ref[pl.ds(k0, tq), :]
        s = [_dot_nt(qh[h], kb) + (cq0[h] - c_ref[h, :, pl.ds(k0, tq)]) for h in range(2)]
        out = []
        for h in range(2):
            m, l, acc = carry[h]
            sh = jnp.where(causal, s[h], -jnp.inf) if masked else s[h]
            m_new = jnp.maximum(m, jnp.max(sh, axis=-1, keepdims=True))
            alpha = jnp.exp2(m - m_new)
            p = jnp.exp2(sh - m_new)
            l = alpha * l + jnp.sum(p, axis=-1, keepdims=True)
            acc = alpha * acc + _dot(p.astype(BF16), vb)
            out.append((m_new, l, acc))
        return tuple(out)

    init = tuple((jnp.full((tq, 1), -jnp.inf, F32), jnp.zeros((tq, 1), F32), jnp.zeros((tq, LANES), F32))
                 for _ in range(2))
    carry = block(i, init, True)
    carry = lax.fori_loop(i - n_prev, i, lambda j, c: block(j, c, False), carry)
    (_, l0, a0), (_, l1, a1) = carry
    o = jnp.where(lo_mask, a0 * (1.0 / l0), a1 * (1.0 / l1))
    o_ref[...] = (_pair_rms(o, lo_mask, FOX_HEAD_DIM) * g_ref[...]).astype(o_ref.dtype)


def _fox_attention(keep, qkv, c, gain, batch, seq, tq):
    n = qkv.shape[0]
    pairs = FOX_HEADS // 2
    rows = tq
    nq = seq // rows
    grid_spec = pltpu.PrefetchScalarGridSpec(
        num_scalar_prefetch=1,
        grid=(batch, pairs, nq),
        in_specs=[pl.BlockSpec((rows, LANES), lambda b, p, i, keep: (b * nq + i, p)),
                  pl.BlockSpec((seq, LANES), lambda b, p, i, keep: (b, pairs + p)),
                  pl.BlockSpec((seq, LANES), lambda b, p, i, keep: (b, 2 * pairs + p)),
                  pl.BlockSpec((None, 2, 1, seq), lambda b, p, i, keep: (b, p, 0, 0)),
                  pl.BlockSpec((1, LANES), lambda b, p, i, keep: (0, p))],
        out_specs=pl.BlockSpec((rows, LANES), lambda b, p, i, keep: (b * nq + i, p)),
    )
    return pl.pallas_call(
        functools.partial(_fox_kernel, tq=tq),
        grid_spec=grid_spec,
        out_shape=jax.ShapeDtypeStruct((n, pairs * LANES), BF16),
        compiler_params=_cparams("parallel", "parallel", "arbitrary"),
        name="fox_attention",
    )(keep, qkv, qkv, qkv, c, gain)


def _hgrn_kernel(q_ref, f_ref, v_ref, gate_ref, lbz_ref, gain_ref, o_ref, st_ref, u_scr, prev_scr, *, layer, n_sub):
    T, C = HGRN_BLOCK, HGRN_CHUNK
    nchunk = T // C
    i = pl.program_id(2)

    @pl.when(i == 0)
    def _():
        st_ref[...] = jnp.zeros_like(st_ref)

    z = lbz_ref[...]
    e = jnp.exp(z - jnp.max(z, axis=0, keepdims=True))
    pz = e / jnp.sum(e, axis=0, keepdims=True)
    lb = jnp.zeros((1, LANES), F32)
    for j in range(1, layer + 1):
        lb = lb + pz[j:j + 1, :]

    lane = lax.broadcasted_iota(I32, (1, LANES), 1)
    lo_mask = lane < HGRN_DIM
    r = lax.broadcasted_iota(I32, (T, T), 0)
    c = lax.broadcasted_iota(I32, (T, T), 1)
    same_chunk = (r // C) == (c // C)
    one = lambda m: jnp.where(m, 1.0, 0.0).astype(BF16)
    scan_mat = jnp.concatenate([one(same_chunk & (c <= r)),
                                one(same_chunk & ((c % C) <= C // 2)),
                                one(same_chunk)], axis=0)
    intra_mask = same_chunk & (c <= r)
    chunk_of_row = lax.broadcasted_iota(I32, (T, LANES), 0) // C
    vr = lax.broadcasted_iota(I32, (LANES, nchunk * LANES), 0)
    kc = lax.broadcasted_iota(I32, (LANES, nchunk * LANES), 1)
    same_head = (vr < HGRN_DIM) == ((kc % LANES) < HGRN_DIM)

    subs = range(n_sub)
    rows = [pl.ds(sb * T, T) for sb in subs]
    q = [q_ref[r_, :].astype(F32) for r_ in rows]
    f = [lb + (1.0 - lb) * _sigmoid(f_ref[r_, :].astype(F32)) for r_ in rows]
    kk = [1.0 - f_ for f_ in f]
    parts = []
    for f_ in f:
        parts.extend(_split2(jnp.log(f_)))
    sc = _dot(scan_mat, jnp.concatenate(parts, axis=1))
    sc = [sc[:, (2 * sb) * LANES:(2 * sb + 1) * LANES] + sc[:, (2 * sb + 1) * LANES:(2 * sb + 2) * LANES]
          for sb in subs]
    b = [s_[:T] for s_ in sc]
    b_mid = [s_[T:2 * T] for s_ in sc]
    b_last = [s_[2 * T:] for s_ in sc]
    v = [v_ref[r_, :] for r_ in rows]
    for sb in subs:
        k_out = kk[sb] * jnp.exp(b_last[sb] - b[sb])
        k_exp = jnp.concatenate([jnp.where(chunk_of_row == j, k_out, 0.0) for j in range(nchunk)],
                                axis=1).astype(BF16)
        v_t = v[sb].astype(F32).T.astype(BF16)
        u_scr[sb] = jnp.where(same_head, _dot(v_t, k_exp), 0.0)
    att = []
    for sb in subs:
        q_in = (q[sb] * jnp.exp(b[sb] - b_mid[sb])).astype(BF16)
        k_in = (kk[sb] * jnp.exp(b_mid[sb] - b[sb])).astype(BF16)
        zq = jnp.zeros_like(q_in)
        att.append([jnp.where(intra_mask, _dot_nt(qm, k_in), 0.0).astype(BF16)
                    for qm in (jnp.where(lo_mask, q_in, zq), jnp.where(lo_mask, zq, q_in))])
    o = [jnp.where(lo_mask, _dot(att[sb][0], v[sb]), _dot(att[sb][1], v[sb])) for sb in subs]
    state = st_ref[...]
    for sb in subs:
        decay = jnp.exp(b_last[sb])
        for j in range(nchunk):
            prev_scr[sb, j * LANES:(j + 1) * LANES, :] = state.astype(BF16)
            state = state * decay[j * C:j * C + 1, :] + u_scr[sb, :, j * LANES:(j + 1) * LANES]
    st_ref[...] = state
    for sb in subs:
        q_out = (q[sb] * jnp.exp(b[sb])).astype(BF16)
        o_all = _dot_nt(q_out, prev_scr[sb])
        acc = o[sb]
        for j in range(nchunk):
            acc = acc + jnp.where(chunk_of_row == j, o_all[:, j * LANES:(j + 1) * LANES], 0.0)
        gate = gate_ref[rows[sb], :].astype(F32)
        y = _pair_rms(acc, lo_mask, HGRN_DIM) * gain_ref[...] * (gate * _sigmoid(gate))
        o_ref[rows[sb], :] = y.astype(o_ref.dtype)


def _hgrn(hy, lb_logits, gain, layer, batch, seq, tg):
    n = hy.shape[0]
    pairs = HGRN_HEADS // 2
    nb = seq // tg
    depth = lb_logits.shape[0]
    col = lambda k: pl.BlockSpec((tg, LANES), lambda b, p, i, k=k: (b * nb + i, k * pairs + p))
    n_sub = tg // HGRN_BLOCK
    states = (HGRN_BLOCK // HGRN_CHUNK) * LANES
    return pl.pallas_call(
        functools.partial(_hgrn_kernel, layer=layer, n_sub=n_sub),
        grid=(batch, pairs, nb),
        in_specs=[col(0), col(1), col(2), col(3),
                  pl.BlockSpec((depth, LANES), lambda b, p, i: (0, p)),
                  pl.BlockSpec((1, LANES), lambda b, p, i: (0, p))],
        out_specs=pl.BlockSpec((tg, LANES), lambda b, p, i: (b * nb + i, p)),
        out_shape=jax.ShapeDtypeStruct((n, pairs * LANES), BF16),
        scratch_shapes=[pltpu.VMEM((LANES, LANES), F32),
                        pltpu.VMEM((n_sub, LANES, states), F32),
                        pltpu.VMEM((n_sub, states, LANES), BF16)],
        compiler_params=_cparams("parallel", "parallel", "arbitrary"),
        name="hgrn",
    )(hy, hy, hy, hy, lb_logits, gain)


def _conv_halo(i, a_scr, tm):
    @pl.when(i == 0)
    def _():
        a_scr[0:CONV_HALO, :] = jnp.zeros((CONV_HALO, a_scr.shape[1]), F32)

    @pl.when(i > 0)
    def _():
        a_scr[0:CONV_HALO, :] = a_scr[tm:tm + CONV_HALO, :]


def _conv_stage(cu, cg, b_ref, a_scr, sh_scr, tm):
    a_scr[CONV_HALO:CONV_HALO + tm, :] = cu * _sigmoid(cg)
    span = CONV_HALO + tm - SUBLANES
    for k in range(1, SUBLANES):
        sh_scr[k - 1, 0:span, :] = a_scr[k:k + span, :]
    return jnp.zeros((tm, a_scr.shape[1]), F32) + b_ref[...]


def _conv_taps(acc, taps, w_ref, a_scr, sh_scr, tm):
    first = CONV_HALO - (CONV_WIDTH - 1)
    for w in taps:
        base, k = divmod(first + w, SUBLANES)
        src = a_scr if k == 0 else sh_scr.at[k - 1]
        acc = acc + src[base * SUBLANES:base * SUBLANES + tm, :] * w_ref[w:w + 1, :]
    return acc


def _conv_finish(acc, ng_ref, nb_ref):
    lane = lax.broadcasted_iota(I32, (1, LANES), 1)
    lo_mask = lane < CONV_GROUP
    halves = []
    for hh in range(acc.shape[1] // LANES):
        xh = acc[:, hh * LANES:(hh + 1) * LANES]
        s_all = jnp.sum(xh, axis=-1, keepdims=True)
        s_lo = jnp.sum(jnp.where(lo_mask, xh, 0.0), axis=-1, keepdims=True)
        d = xh - jnp.where(lo_mask, s_lo, s_all - s_lo) * (1.0 / CONV_GROUP)
        halves.append(_pair_rms(d, lo_mask, CONV_GROUP))
    y = jnp.concatenate(halves, axis=1) * ng_ref[...] + nb_ref[...]
    return y * _sigmoid(y)


def _router_logits_t(t, wr_hi, wr_lo, br):
    th, tl = _split2(t)
    return _dot_nt(wr_hi, th) + _dot_nt(wr_lo, th) + _dot_nt(wr_hi, tl) + br


def _xattn_kernel(fo_ref, ho_ref, co_ref, x_ref, wo_ref, gx_ref, wq_ref, kv_ref, wxo_ref, g_ref,
                  wrh_ref, wrl_ref, br_ref, x2_ref, route_ref, cnt_ref, cnt_scr, *, tm):
    step = pl.program_id(0)

    @pl.when(step == 0)
    def _():
        cnt_scr[...] = jnp.zeros_like(cnt_scr)

    nf, nh = fo_ref.shape[1], ho_ref.shape[1]
    x1 = (_load_rows(x_ref, tm) + _dot(fo_ref[...], wo_ref[0:nf, :]) + _dot(ho_ref[...], wo_ref[nf:nf + nh, :])
          + _dot(co_ref[...], wo_ref[nf + nh:, :]))
    xq = _dot(_rms(x1, gx_ref[...]).astype(BF16), wq_ref[...]).astype(BF16)
    d = xq.shape[1]
    hd = d // XATT_HEADS
    scores = [_dot_nt(xq[:, h * hd:(h + 1) * hd], kv_ref[:, h * hd:(h + 1) * hd]) for h in range(XATT_HEADS)]
    probs = [jnp.exp(s - jnp.max(s, axis=-1, keepdims=True)) for s in scores]
    heads = []
    for h, p in enumerate(probs):
        vh = kv_ref[:, d + h * hd:d + (h + 1) * hd]
        l = jnp.sum(p, axis=-1, keepdims=True)
        heads.append((_dot(p.astype(BF16), vh) * (1.0 / l)).astype(BF16))
    x2 = x1 + _dot(jnp.concatenate(heads, axis=1), wxo_ref[...])
    _store_rows(x2_ref, x2)
    logits = _router_logits_t(_rms(x2, g_ref[...]), wrh_ref[...], wrl_ref[...], br_ref[...])
    gl = [logits[g:g + 1, :] for g in range(N_GROUPS)]
    gmax = jnp.maximum(jnp.maximum(gl[0], gl[1]), jnp.maximum(gl[2], gl[3]))
    gidx = jnp.where(gl[0] >= gmax, 0, jnp.where(gl[1] >= gmax, 1, jnp.where(gl[2] >= gmax, 2, 3)))
    rows8 = lax.broadcasted_iota(I32, (8, tm), 0)
    onehot = jnp.where(rows8 == gidx, 1.0, 0.0)
    r = lax.broadcasted_iota(I32, (tm, tm), 0)
    c = lax.broadcasted_iota(I32, (tm, tm), 1)
    incl = _dot(onehot.astype(BF16), jnp.where(r <= c, 1.0, 0.0).astype(BF16))
    before = cnt_scr[:, 0:1]
    rank = jnp.sum(onehot * (incl - 1.0 + before), axis=0, keepdims=True)
    route_ref[...] = jnp.where(rows8 == 0, gidx, jnp.where(rows8 == 1, rank.astype(I32), 0))
    cnt_scr[...] = cnt_scr[...] + incl[:, tm - 1:tm]
    cnt_ref[...] = cnt_scr[...].astype(I32)


def _xattn(fo, ho, co, x2d, wo, gx, wq, kv, wxo, g, wrh, wrl, br, layer, batch, seq, tm):
    n, d = fo.shape[0], wo.shape[2]
    nb = seq // tm
    mem = kv.shape[2]
    full = lambda a: pl.BlockSpec(a.shape, lambda r: (0,) * a.ndim)
    stacked = lambda a: _layer_block(a, layer)
    rowblk = lambda a: _row_block(a, tm, d if a is x2d else a.shape[1])
    return pl.pallas_call(
        functools.partial(_xattn_kernel, tm=tm),
        grid=(n // tm,),
        in_specs=[rowblk(fo), rowblk(ho), rowblk(co), rowblk(x2d), stacked(wo), full(gx), stacked(wq),
                  pl.BlockSpec((None, None, mem, 2 * d), lambda r: (layer, r // nb, 0, 0)),
                  stacked(wxo), full(g), full(wrh), full(wrl), full(br)],
        out_specs=[pl.BlockSpec((tm * (d // LANES), LANES), lambda r: (r, 0)),
                   pl.BlockSpec((8, tm), lambda r: (0, r)),
                   pl.BlockSpec((8, LANES), lambda r: (0, 0))],
        out_shape=[jax.ShapeDtypeStruct((n * (d // LANES), LANES), F32),
                   jax.ShapeDtypeStruct((8, n), I32),
                   jax.ShapeDtypeStruct((8, LANES), I32)],
        scratch_shapes=[pltpu.VMEM((8, LANES), F32)],
        compiler_params=_cparams("arbitrary"),
        name="xattn",
    )(fo, ho, co, x2d, wo, gx, wq, kv, wxo, g, wrh, wrl, br)


def _plan_kernel(cnt_ref, route_ref, src_ref, dst_ref, tg_ref, dest_vmem, dest_smem, sem, *, n_tok, tmm, sub):
    n_slots = src_ref.shape[0]
    n_tiles = tg_ref.shape[0] - 1
    shift = tmm.bit_length() - 1
    assert tmm == 1 << shift

    def clear(s, _):
        src_ref[s] = 0
        parity = lax.shift_right_logical(s, shift) & 1
        dst_ref[s] = (n_tok + parity * tmm + (s & (tmm - 1))) * sub
        return 0

    off = jnp.int32(0)
    last_group = jnp.int32(0)
    starts, ends = [], []
    for g in range(N_GROUPS):
        cnt = cnt_ref[g]
        padded = lax.shift_left(lax.shift_right_logical(cnt + (tmm - 1), shift), shift)
        lax.fori_loop(off + cnt, off + padded, clear, 0)
        starts.append(off)
        off = off + padded
        ends.append(off)
        last_group = jnp.where(cnt > 0, g, last_group)
    lax.fori_loop(off, n_slots, clear, 0)

    gidx = route_ref[0:1, :]
    rank = route_ref[1:2, :]
    start = jnp.where(gidx == 0, starts[0], jnp.where(gidx == 1, starts[1],
                                                      jnp.where(gidx == 2, starts[2], starts[3])))
    dest_vmem[...] = start + rank
    to_smem = pltpu.make_async_copy(dest_vmem, dest_smem, sem)
    to_smem.start()
    to_smem.wait()

    def place(t, _):
        slot = dest_smem[0, t]
        src_ref[slot] = t * sub
        dst_ref[slot] = t * sub
        return 0

    lax.fori_loop(0, n_tok, place, 0, unroll=8)

    def tile(k, _):
        start = k * tmm
        g = ((start >= ends[0]).astype(I32) + (start >= ends[1]).astype(I32) + (start >= ends[2]).astype(I32))
        tg_ref[k] = jnp.minimum(g, last_group)
        return 0

    lax.fori_loop(0, n_tiles, tile, 0)
    tg_ref[n_tiles] = lax.shift_right_logical(off, shift)


def _route_plan(cnt, route, n_tok, tmm, sub):
    n_tiles = n_tok // tmm + N_GROUPS
    smem = pl.BlockSpec(memory_space=pltpu.SMEM)
    slots = jax.ShapeDtypeStruct((n_tiles * tmm,), I32)
    return pl.pallas_call(
        functools.partial(_plan_kernel, n_tok=n_tok, tmm=tmm, sub=sub),
        in_specs=[smem, pl.BlockSpec(memory_space=pltpu.VMEM)],
        out_specs=[smem, smem, smem],
        out_shape=[slots, slots, jax.ShapeDtypeStruct((n_tiles + 1,), I32)],
        scratch_shapes=[pltpu.VMEM((1, n_tok), I32), pltpu.SMEM((1, n_tok), I32), pltpu.SemaphoreType.DMA(())],
        name="route_plan",
    )(cnt, route)


def _moe_kernel(src_ref, dst_ref, tg_ref, x2_hbm, g_ref, wrh_ref, wrl_ref, br_ref, wg_ref, wu_ref, wd_ref,
                x3_hbm, xb, ob, wg_s, wu_s, wd_s, gsem, ssem, zsem, *, tmm, n_tok):
    k = pl.program_id(0)
    n_active = tg_ref[pl.num_programs(0)]
    slot = k % 2

    @pl.when((k == 0) | (tg_ref[k] != tg_ref[jnp.maximum(k - 1, 0)]))
    def _():
        ff = wd_ref.shape[1]
        for e in range(EXPERTS_PER_GROUP):
            wg_s[e] = wg_ref[e].astype(BF16)
            wu_s[e] = wu_ref[e].astype(BF16)
            wd_s[e * ff:(e + 1) * ff, :] = wd_ref[e].astype(BF16)

    sub = xb.shape[1] // tmm

    def rows_at(first):
        return pl.ds(pl.multiple_of(first, sub), sub)

    def gather_copy(tile, sl, r):
        return pltpu.make_async_copy(x2_hbm.at[rows_at(src_ref[tile * tmm + r])], xb.at[sl, rows_at(r * sub)],
                                     gsem.at[sl])

    def scatter_copy(tile, sl, r):
        return pltpu.make_async_copy(ob.at[sl, rows_at(r * sub)], x3_hbm.at[rows_at(dst_ref[tile * tmm + r])],
                                     ssem.at[sl])

    def for_rows(fn):
        def body(r8, _):
            for u in range(DMA_UNROLL):
                fn(r8 * DMA_UNROLL + u, u % 2)
            return 0
        lax.fori_loop(0, tmm // DMA_UNROLL, body, 0)

    def dump_fill(half):
        return pltpu.make_async_copy(ob.at[1], x3_hbm.at[pl.ds((n_tok + half * tmm) * sub, tmm * sub)], zsem)

    @pl.when(k == 0)
    def _():
        ob[1] = jnp.zeros(ob.shape[1:], F32)
        dump_fill(0).start()
        dump_fill(1).start()
        for_rows(lambda r, pri: gather_copy(0, 0, r).start(priority=pri))

    @pl.when(k < n_active)
    def _():
        for_rows(lambda r, pri: gather_copy(k, slot, r).wait())

        @pl.when(k + 1 < n_active)
        def _():
            for_rows(lambda r, pri: gather_copy(k + 1, 1 - slot, r).start(priority=pri))

        @pl.when(k >= 2)
        def _():
            for_rows(lambda r, pri: scatter_copy(k - 2, slot, r).wait())

        y = _moe_tile(_load_rows(xb.at[slot], tmm), tg_ref[k], g_ref, wrh_ref, wrl_ref, br_ref,
                      wg_s, wu_s, wd_s)
        _store_rows(ob.at[slot], y)

        @pl.when(k == 0)
        def _():
            dump_fill(0).wait()
            dump_fill(1).wait()

        for_rows(lambda r, pri: scatter_copy(k, slot, r).start(priority=pri))

        @pl.when(k == n_active - 1)
        def _():
            @pl.when(k >= 1)
            def _():
                for_rows(lambda r, pri: scatter_copy(k - 1, 1 - slot, r).wait())
            for_rows(lambda r, pri: scatter_copy(k, slot, r).wait())


def _moe_tile(x, grp, g_ref, wrh_ref, wrl_ref, br_ref, wg_ref, wu_ref, wd_ref):
    t = _rms(x, g_ref[...])
    tb = t.astype(BF16)
    th, tl = _split2(t)
    logits = _dot(th, wrh_ref[...]) + _dot(th, wrl_ref[...]) + _dot(tl, wrh_ref[...]) + br_ref[...]
    lane = lax.broadcasted_iota(I32, (1, LANES), 1)
    lanef = lane.astype(F32)
    rmax = lambda a: jnp.max(a, axis=-1, keepdims=True)
    rsum = lambda a: jnp.sum(a, axis=-1, keepdims=True)
    gmask = lane < N_GROUPS
    gmax = rmax(jnp.where(gmask, logits, -jnp.inf))
    zg = rsum(jnp.where(gmask, jnp.exp(logits - gmax), 0.0))
    p_group = jnp.exp(rsum(jnp.where(lane == grp, logits, 0.0)) - gmax) / zg
    e_lo = N_GROUPS + EXPERTS_PER_GROUP * grp
    emask = (lane >= e_lo) & (lane < e_lo + EXPERTS_PER_GROUP)
    em = jnp.where(emask, logits, -jnp.inf)
    e1 = rmax(em)
    i1 = jnp.min(jnp.where(em == e1, lanef, 1e9), axis=-1, keepdims=True)
    em2 = jnp.where(lanef == i1, -jnp.inf, em)
    e2 = rmax(em2)
    i2 = jnp.min(jnp.where(em2 == e2, lanef, 1e9), axis=-1, keepdims=True)
    ze = rsum(jnp.where(emask, jnp.exp(logits - e1), 0.0))
    p1 = 1.0 / ze
    p2 = jnp.exp(e2 - e1) / ze
    w1 = p_group * (p1 / (p1 + p2))
    w2 = p_group * (p2 / (p1 + p2))
    gates = [_dot(tb, wg_ref[e]) for e in range(EXPERTS_PER_GROUP)]
    ups = [_dot(tb, wu_ref[e]) for e in range(EXPERTS_PER_GROUP)]
    cw_lanes = jnp.where(lanef == i1, w1, 0.0) + jnp.where(lanef == i2, w2, 0.0)
    parts = []
    for e in range(EXPERTS_PER_GROUP):
        cw = rsum(jnp.where(lane == e_lo + e, cw_lanes, 0.0))
        parts.append((gates[e] * _sigmoid(gates[e]) * ups[e] * cw).astype(BF16))
    return x + _dot(jnp.concatenate(parts, axis=1), wd_ref[...])


def _moe(src, dst, tg, x2, g, wrh_t, wrl_t, br_row, wg, wu, wd, layer, tmm):
    sub = g.shape[1] // LANES
    n_tok = x2.shape[0] // sub
    n_tiles = tg.shape[0] - 1
    full = lambda a: pl.BlockSpec(a.shape, lambda k, src, dst, tg: (0,) * a.ndim)
    grp = lambda a: pl.BlockSpec((None, None) + a.shape[2:], lambda k, src, dst, tg: (layer, tg[k], 0, 0, 0))
    n_exp, d, ff = wg.shape[2:]
    grid_spec = pltpu.PrefetchScalarGridSpec(
        num_scalar_prefetch=3,
        grid=(n_tiles,),
        in_specs=[pl.BlockSpec(memory_space=pl.ANY), full(g), full(wrh_t), full(wrl_t), full(br_row),
                  grp(wg), grp(wu), grp(wd)],
        out_specs=pl.BlockSpec(memory_space=pl.ANY),
        scratch_shapes=[pltpu.VMEM((2, tmm * sub, LANES), F32), pltpu.VMEM((2, tmm * sub, LANES), F32),
                        pltpu.VMEM((n_exp, d, ff), BF16), pltpu.VMEM((n_exp, d, ff), BF16),
                        pltpu.VMEM((n_exp * ff, d), BF16),
                        pltpu.SemaphoreType.DMA((2,)), pltpu.SemaphoreType.DMA((2,)), pltpu.SemaphoreType.DMA(())],
    )
    return pl.pallas_call(
        functools.partial(_moe_kernel, tmm=tmm, n_tok=n_tok),
        grid_spec=grid_spec,
        out_shape=jax.ShapeDtypeStruct(((n_tok + 2 * tmm) * sub, LANES), F32),
        compiler_params=_cparams("arbitrary"),
        name="moe",
    )(src, dst, tg, x2, g, wrh_t, wrl_t, br_row, wg, wu, wd)


def _final_norm_kernel(x_ref, g_ref, o_ref):
    o_ref[...] = _rms(_load_rows(x_ref, o_ref.shape[0]), g_ref[...])


def _final_norm(xs, g, n, tm):
    d = g.shape[1]
    return pl.pallas_call(
        _final_norm_kernel,
        grid=(n // tm,),
        in_specs=[_row_block(xs, tm, d), pl.BlockSpec((1, d), lambda r: (0, 0))],
        out_specs=pl.BlockSpec((tm, d), lambda r: (r, 0)),
        out_shape=jax.ShapeDtypeStruct((n, d), F32),
        compiler_params=_cparams("arbitrary"),
        name="final_norm",
    )(xs, g)


def _row(v):
    return v.reshape(1, -1)


def kernel(x, mem, mix_norm_g, w_in, fox_f_bias, fox_norm_g, hgrn_lb_logits, hgrn_norm_g, conv_w, conv_b,
           conv_norm_g, conv_norm_b, w_out, xatt_norm_g, mem_norm_g, w_xq, w_xkv, w_xo, ffn_norm_g,
           w_group, b_group, w_router, b_router, w_gate, w_up, w_down, final_norm_g):
    batch, seq, d = x.shape
    depth = w_in.shape[0]
    n = batch * seq
    tm = min(512, seq)
    tmm = min(512, seq)
    fblk = min(FOX_BLOCK, tm)
    fw = FOX_HEADS * FOX_HEAD_DIM
    hw = HGRN_HEADS * HGRN_DIM
    cw = conv_w.shape[-1]
    assert seq % tm == 0 and seq % HGRN_BLOCK == 0 and d == fw + hw + cw

    offs = [0]
    for width in (fw, fw, fw, FOX_HEADS, hw, hw, hw, hw, cw, cw):
        offs.append(offs[-1] + width)
    seg = lambda a, i, j: a[..., offs[i]:offs[j]]
    wqkv = jnp.concatenate([seg(w_in, 0, 1) * (FOX_HEAD_DIM ** -0.5 * LOG2E), seg(w_in, 1, 3)], axis=-1).astype(BF16)
    wff = jnp.swapaxes(seg(w_in, 3, 4), 1, 2).astype(BF16)
    wh = jnp.concatenate([seg(w_in, 4, 5), seg(w_in, 5, 6), seg(w_in, 6, 7), seg(w_in, 7, 8)], axis=-1).astype(BF16)
    wc = seg(w_in, 8, 10).astype(BF16)
    wo = w_out.astype(BF16)
    wxq = (w_xq * ((d // XATT_HEADS) ** -0.5)).astype(BF16)
    wxkv = w_xkv.astype(BF16)
    wxo = w_xo.astype(BF16)
    n_exp = N_GROUPS * EXPERTS_PER_GROUP
    wr = jnp.concatenate([w_group, w_router], axis=-1)
    wr_hi = wr.astype(BF16)
    wr_lo = (wr - wr_hi.astype(F32)).astype(BF16)
    pad_rows = lambda a: jnp.pad(jnp.swapaxes(a, 1, 2), ((0, 0), (0, ROUTER_ROWS - a.shape[2]), (0, 0)))
    pad_cols = lambda a: jnp.pad(a, ((0, 0), (0, 0), (0, LANES - a.shape[2])))
    br = jnp.concatenate([b_group, b_router], axis=-1)
    br_col = jnp.pad(br, ((0, 0), (0, ROUTER_ROWS - br.shape[1])))[:, :, None]
    br_row = jnp.pad(br, ((0, 0), (0, LANES - br.shape[1])))[:, None, :]
    conv_w_pad = jnp.pad(conv_w, ((0, 0), (0, CONV_HALO - CONV_WIDTH), (0, 0)))
    wrh_rows, wrl_rows, wrh_cols, wrl_cols = pad_rows(wr_hi), pad_rows(wr_lo), pad_cols(wr_hi), pad_cols(wr_lo)

    kv_all = _mem_kv(mem.reshape(-1, d), _row(mem_norm_g), wxkv).reshape(depth, batch, mem.shape[1], 2 * d)

    xs = x.reshape(n, d)
    for l in range(depth):
        qkv, hy, co, lf, qn, kn = _mix_in(xs, _row(mix_norm_g[l]), wqkv, wh, wc, wff, fox_f_bias[l].reshape(-1, 1),
                                          conv_w_pad[l], _row(conv_b[l]), _row(conv_norm_g[l]),
                                          _row(conv_norm_b[l]), l, batch, seq, tm)
        c, keep = _fox_plan(lf, qn, kn, fblk, tm)
        fo = _fox_attention(keep.reshape(-1), qkv, c, _row(fox_norm_g[l]), batch, seq, fblk)
        ho = _hgrn(hy, hgrn_lb_logits, _row(hgrn_norm_g[l]), l, batch, seq, min(1024, seq))
        x2, route, cnt = _xattn(fo, ho, co, xs, wo, _row(xatt_norm_g[l]), wxq, kv_all, wxo,
                                _row(ffn_norm_g[l]), wrh_rows[l], wrl_rows[l], br_col[l], l, batch, seq, tm)
        src, dst, tg = _route_plan(cnt[:N_GROUPS, 0], route, n, tmm, d // LANES)
        xs = _moe(src, dst, tg, x2, _row(ffn_norm_g[l]), wrh_cols[l], wrl_cols[l], br_row[l],
                  w_gate, w_up, w_down, l, tmm)
    return _final_norm(xs, _row(final_norm_g), n, tm).reshape(batch, seq, d)
```
